```python
import jax, jax.numpy as jnp
from jax import lax
import numpy as np

D_MODEL = 2048
BATCH = 8
SEQ = 8192
DEPTH = 2

RMS_EPS = 1e-6
ROPE_THETA = 10000.0
D_FF = 5632
Q_BLOCK = 128

MLA_HEADS = 8
MLA_Q_LORA = 512
MLA_KV_LORA = 256
MLA_NOPE_DIM = 128
MLA_ROPE_DIM = 64
MLA_V_DIM = 128

SWA_HEADS = 8
SWA_KV_HEADS = 2
SWA_HEAD_DIM = 64
WINDOW = 128

FOX_HEADS = 8
FOX_HEAD_DIM = 64

IN_SPLITS = [
    MLA_Q_LORA,
    MLA_KV_LORA,
    MLA_ROPE_DIM,
    SWA_HEADS * SWA_HEAD_DIM,
    SWA_KV_HEADS * SWA_HEAD_DIM,
    SWA_KV_HEADS * SWA_HEAD_DIM,
    FOX_HEADS * FOX_HEAD_DIM,
    FOX_HEADS * FOX_HEAD_DIM,
    FOX_HEADS * FOX_HEAD_DIM,
    FOX_HEADS,
]
IN_COLS = int(sum(IN_SPLITS))
IN_OFFSETS = [int(o) for o in np.cumsum(IN_SPLITS)[:-1]]
MIX_WIDTH = MLA_HEADS * MLA_V_DIM + SWA_HEADS * SWA_HEAD_DIM + FOX_HEADS * FOX_HEAD_DIM

kernel_name = "hybrid_mla_swa_sink_fox_macaron"


def rmsnorm(x, g):
    x32 = x.astype(jnp.float32)
    y = x32 * lax.rsqrt(jnp.mean(x32 * x32, axis=-1, keepdims=True) + RMS_EPS)
    return (y * g.astype(jnp.float32)).astype(x.dtype)


def swiglu(h, w_gate, w_up, w_down):
    return (jax.nn.silu(h @ w_gate) * (h @ w_up)) @ w_down


def rope_tables(positions, dim):
    inv_freq = ROPE_THETA ** (-jnp.arange(0, dim, 2, dtype=jnp.float32) / dim)
    ang = positions.astype(jnp.float32)[..., None] * inv_freq
    return jnp.cos(ang), jnp.sin(ang)


def apply_rope(x, cos, sin):
    half = x.shape[-1] // 2
    x1, x2 = x[..., :half], x[..., half:]
    c, s = cos[:, :, None, :], sin[:, :, None, :]
    return jnp.concatenate([x1 * c - x2 * s, x2 * c + x1 * s], axis=-1).astype(x.dtype)


def causal_block_attention(q, k, v, scale, log_f_cum=None):
    B, S, H, dq = q.shape
    dv = v.shape[-1]
    n = S // Q_BLOCK
    qb = q.reshape(B, n, Q_BLOCK, H, dq).transpose(1, 0, 2, 3, 4)
    key_pos = jnp.arange(S)
    idx = jnp.arange(n)

    def scores_for(i, qi):
        s = jnp.einsum('bqhd,bkhd->bhqk', qi, k).astype(jnp.float32) * scale
        q_pos = i * Q_BLOCK + jnp.arange(Q_BLOCK)
        return s, key_pos[None, :] <= q_pos[:, None]

    def finish(s, mask):
        s = jnp.where(mask[None, None], s, -jnp.inf)
        p = jax.nn.softmax(s, axis=-1).astype(v.dtype)
        return jnp.einsum('bhqk,bkhd->bqhd', p, v)

    if log_f_cum is None:
        def step(args):
            i, qi = args
            s, mask = scores_for(i, qi)
            return finish(s, mask)
        out = lax.map(step, (idx, qb))
    else:
        c_all = log_f_cum.transpose(0, 2, 1)
        cb = log_f_cum.reshape(B, n, Q_BLOCK, H).transpose(1, 0, 3, 2)

        def step(args):
            i, qi, ci = args
            s, mask = scores_for(i, qi)
            s = s + (ci[..., :, None] - c_all[:, :, None, :])
            return finish(s, mask)
        out = lax.map(step, (idx, qb, cb))
    return out.transpose(1, 0, 2, 3, 4).reshape(B, S, H, dv)


def sliding_window_sink_attention(q, k, v, sinks):
    B, S, H, d = q.shape
    Hkv = k.shape[2]
    G = H // Hkv
    n = S // WINDOW
    qb = q.reshape(B, n, WINDOW, Hkv, G, d)
    pad = jnp.zeros((B, WINDOW, Hkv, d), k.dtype)
    kp = jnp.concatenate([pad, k], axis=1).reshape(B, n + 1, WINDOW, Hkv, d)
    vp = jnp.concatenate([pad.astype(v.dtype), v], axis=1).reshape(B, n + 1, WINDOW, Hkv, d)
    kw = jnp.concatenate([kp[:, :-1], kp[:, 1:]], axis=2)
    vw = jnp.concatenate([vp[:, :-1], vp[:, 1:]], axis=2)
    s = jnp.einsum('bnqhgd,bnkhd->bnhgqk', qb, kw).astype(jnp.float32) * (d ** -0.5)
    blk = jnp.arange(n)[:, None, None]
    q_pos = blk * WINDOW + jnp.arange(WINDOW)[None, :, None]
    k_pos = (blk - 1) * WINDOW + jnp.arange(2 * WINDOW)[None, None, :]
    mask = (k_pos <= q_pos) & (k_pos > q_pos - WINDOW) & (k_pos >= 0)
    s = jnp.where(mask[None, :, None, None], s, -jnp.inf)
    sink = jnp.broadcast_to(sinks.astype(jnp.float32).reshape(Hkv, G)[None, None, :, :, None, None],
                            s.shape[:-1] + (1,))
    p = jax.nn.softmax(jnp.concatenate([s, sink], axis=-1), axis=-1)[..., :-1]
    o = jnp.einsum('bnhgqk,bnkhd->bnqhgd', p.astype(v.dtype), vw)
    return o.reshape(B, S, H * d)


def hybrid_mixer(h, cos_m, sin_m, cos_s, sin_s, w_in, q_norm, w_q_b, kv_norm, w_kv_b,
                 sinks, forget_bias, w_out):
    B, S, _ = h.shape
    proj = h @ w_in
    (c_q, c_kv, k_rope, q_s, k_s, v_s, q_f, k_f, v_f, f_logit) = jnp.split(proj, IN_OFFSETS, axis=-1)

    q = (rmsnorm(c_q, q_norm) @ w_q_b).reshape(B, S, MLA_HEADS, MLA_NOPE_DIM + MLA_ROPE_DIM)
    q_nope, q_pe = q[..., :MLA_NOPE_DIM], apply_rope(q[..., MLA_NOPE_DIM:], cos_m, sin_m)
    kv = (rmsnorm(c_kv, kv_norm) @ w_kv_b).reshape(B, S, MLA_HEADS, MLA_NOPE_DIM + MLA_V_DIM)
    k_nope, v_m = kv[..., :MLA_NOPE_DIM], kv[..., MLA_NOPE_DIM:]
    k_pe = apply_rope(k_rope[:, :, None, :], cos_m, sin_m)
    q_m = jnp.concatenate([q_nope, q_pe], axis=-1)
    k_m = jnp.concatenate([k_nope, jnp.broadcast_to(k_pe, (B, S, MLA_HEADS, MLA_ROPE_DIM))], axis=-1)
    o_mla = causal_block_attention(q_m, k_m, v_m, (MLA_NOPE_DIM + MLA_ROPE_DIM) ** -0.5)

    q_s = apply_rope(q_s.reshape(B, S, SWA_HEADS, SWA_HEAD_DIM), cos_s, sin_s)
    k_s = apply_rope(k_s.reshape(B, S, SWA_KV_HEADS, SWA_HEAD_DIM), cos_s, sin_s)
    v_s = v_s.reshape(B, S, SWA_KV_HEADS, SWA_HEAD_DIM)
    o_swa = sliding_window_sink_attention(q_s, k_s, v_s, sinks)

    log_f = jax.nn.log_sigmoid(f_logit.astype(jnp.float32) + forget_bias.astype(jnp.float32))
    c = jnp.cumsum(log_f, axis=1)
    o_fox = causal_block_attention(q_f.reshape(B, S, FOX_HEADS, FOX_HEAD_DIM),
                                   k_f.reshape(B, S, FOX_HEADS, FOX_HEAD_DIM),
                                   v_f.reshape(B, S, FOX_HEADS, FOX_HEAD_DIM),
                                   FOX_HEAD_DIM ** -0.5, c)

    mixed = jnp.concatenate([o_mla.reshape(B, S, -1), o_swa, o_fox.reshape(B, S, -1)], axis=-1)
    return mixed @ w_out


def _fwd_setup_inputs(seed: int = 0) -> dict:
    key = jax.random.key(seed)
    ks = jax.random.split(key, 24)
    f32 = jnp.float32

    def w(k, fan_in, fan_out):
        return jax.random.normal(k, (DEPTH, fan_in, fan_out), f32) * fan_in ** -0.5

    def gain(k, dim):
        return 1.0 + 0.1 * jax.random.normal(k, (DEPTH, dim), f32)

    return {
        "x": jax.random.normal(ks[0], (BATCH, SEQ, D_MODEL), f32),
        "positions": jnp.broadcast_to(jnp.arange(SEQ, dtype=jnp.int32), (BATCH, SEQ)),
        "ffn1_norm": gain(ks[1], D_MODEL),
        "ffn1_w_gate": w(ks[2], D_MODEL, D_FF),
        "ffn1_w_up": w(ks[3], D_MODEL, D_FF),
        "ffn1_w_down": w(ks[4], D_FF, D_MODEL),
        "mix_norm": gain(ks[5], D_MODEL),
        "w_in": w(ks[6], D_MODEL, IN_COLS),
        "mla_q_norm": gain(ks[7], MLA_Q_LORA),
        "mla_w_q_b": w(ks[8], MLA_Q_LORA, MLA_HEADS * (MLA_NOPE_DIM + MLA_ROPE_DIM)),
        "mla_kv_norm": gain(ks[9], MLA_KV_LORA),
        "mla_w_kv_b": w(ks[10], MLA_KV_LORA, MLA_HEADS * (MLA_NOPE_DIM + MLA_V_DIM)),
        "swa_sinks": 0.5 * jax.random.normal(ks[11], (DEPTH, SWA_HEADS), f32),
        "fox_forget_bias": 0.1 * jax.random.normal(ks[12], (DEPTH, FOX_HEADS), f32),
        "w_out": w(ks[13], MIX_WIDTH, D_MODEL),
        "ffn2_norm": gain(ks[14], D_MODEL),
        "ffn2_w_gate": w(ks[15], D_MODEL, D_FF),
        "ffn2_w_up": w(ks[16], D_MODEL, D_FF),
        "ffn2_w_down": w(ks[17], D_FF, D_MODEL),
        "final_norm": 1.0 + 0.1 * jax.random.normal(ks[18], (D_MODEL,), f32),
    }


def _fwd_reference(x, positions, ffn1_norm, ffn1_w_gate, ffn1_w_up, ffn1_w_down, mix_norm, w_in,
              mla_q_norm, mla_w_q_b, mla_kv_norm, mla_w_kv_b, swa_sinks, fox_forget_bias, w_out,
              ffn2_norm, ffn2_w_gate, ffn2_w_up, ffn2_w_down, final_norm):
    cos_m, sin_m = rope_tables(positions, MLA_ROPE_DIM)
    cos_s, sin_s = rope_tables(positions, SWA_HEAD_DIM)
    for l in range(DEPTH):
        x = x + 0.5 * swiglu(rmsnorm(x, ffn1_norm[l]), ffn1_w_gate[l], ffn1_w_up[l], ffn1_w_down[l])
        x = x + hybrid_mixer(rmsnorm(x, mix_norm[l]), cos_m, sin_m, cos_s, sin_s, w_in[l],
                             mla_q_norm[l], mla_w_q_b[l], mla_kv_norm[l], mla_w_kv_b[l],
                             swa_sinks[l], fox_forget_bias[l], w_out[l])
        x = x + 0.5 * swiglu(rmsnorm(x, ffn2_norm[l]), ffn2_w_gate[l], ffn2_w_up[l], ffn2_w_down[l])
    return rmsnorm(x, final_norm)


import jax as _jax
import jax.numpy as _jnp

TWIN_FORMAT = 'train_step'
FWD_PARAMS = ['x', 'positions', 'ffn1_norm', 'ffn1_w_gate', 'ffn1_w_up', 'ffn1_w_down', 'mix_norm', 'w_in', 'mla_q_norm', 'mla_w_q_b', 'mla_kv_norm', 'mla_w_kv_b', 'swa_sinks', 'fox_forget_bias', 'w_out', 'ffn2_norm', 'ffn2_w_gate', 'ffn2_w_up', 'ffn2_w_down', 'final_norm']
TWIN_WEIGHTS = ['ffn1_norm', 'ffn1_w_gate', 'ffn1_w_up', 'ffn1_w_down', 'mix_norm', 'w_in', 'mla_q_norm', 'mla_w_q_b', 'mla_kv_norm', 'mla_w_kv_b', 'swa_sinks', 'fox_forget_bias', 'w_out', 'ffn2_norm', 'ffn2_w_gate', 'ffn2_w_up', 'ffn2_w_down', 'final_norm']
TWIN_DIFF_INPUT = 'x'
TWIN_INPUTS = ['x', 'positions', 'ffn1_norm', 'ffn1_w_gate', 'ffn1_w_up', 'ffn1_w_down', 'mix_norm', 'w_in', 'mla_q_norm', 'mla_w_q_b', 'mla_kv_norm', 'mla_w_kv_b', 'swa_sinks', 'fox_forget_bias', 'w_out', 'ffn2_norm', 'ffn2_w_gate', 'ffn2_w_up', 'ffn2_w_down', 'final_norm', 'loss_target', 'm_ffn1_norm', 'm_ffn1_w_gate', 'm_ffn1_w_up', 'm_ffn1_w_down', 'm_mix_norm', 'm_w_in', 'm_mla_q_norm', 'm_mla_w_q_b', 'm_mla_kv_norm', 'm_mla_w_kv_b', 'm_swa_sinks', 'm_fox_forget_bias', 'm_w_out', 'm_ffn2_norm', 'm_ffn2_w_gate', 'm_ffn2_w_up', 'm_ffn2_w_down', 'm_final_norm', 'v_ffn1_norm', 'v_ffn1_w_gate', 'v_ffn1_w_up', 'v_ffn1_w_down', 'v_mix_norm', 'v_w_in', 'v_mla_q_norm', 'v_mla_w_q_b', 'v_mla_kv_norm', 'v_mla_w_kv_b', 'v_swa_sinks', 'v_fox_forget_bias', 'v_w_out', 'v_ffn2_norm', 'v_ffn2_w_gate', 'v_ffn2_w_up', 'v_ffn2_w_down', 'v_final_norm']
TWIN_OUTPUTS = ['loss', 'grad_x', 'grad_ffn1_norm', 'grad_ffn1_w_gate', 'grad_ffn1_w_up', 'grad_ffn1_w_down', 'grad_mix_norm', 'grad_w_in', 'grad_mla_q_norm', 'grad_mla_w_q_b', 'grad_mla_kv_norm', 'grad_mla_w_kv_b', 'grad_swa_sinks', 'grad_fox_forget_bias', 'grad_w_out', 'grad_ffn2_norm', 'grad_ffn2_w_gate', 'grad_ffn2_w_up', 'grad_ffn2_w_down', 'grad_final_norm', 'delta_ffn1_norm', 'delta_ffn1_w_gate', 'delta_ffn1_w_up', 'delta_ffn1_w_down', 'delta_mix_norm', 'delta_w_in', 'delta_mla_q_norm', 'delta_mla_w_q_b', 'delta_mla_kv_norm', 'delta_mla_w_kv_b', 'delta_swa_sinks', 'delta_fox_forget_bias', 'delta_w_out', 'delta_ffn2_norm', 'delta_ffn2_w_gate', 'delta_ffn2_w_up', 'delta_ffn2_w_down', 'delta_final_norm', 'new_m_ffn1_norm', 'new_m_ffn1_w_gate', 'new_m_ffn1_w_up', 'new_m_ffn1_w_down', 'new_m_mix_norm', 'new_m_w_in', 'new_m_mla_q_norm', 'new_m_mla_w_q_b', 'new_m_mla_kv_norm', 'new_m_mla_w_kv_b', 'new_m_swa_sinks', 'new_m_fox_forget_bias', 'new_m_w_out', 'new_m_ffn2_norm', 'new_m_ffn2_w_gate', 'new_m_ffn2_w_up', 'new_m_ffn2_w_down', 'new_m_final_norm', 'new_v_ffn1_norm', 'new_v_ffn1_w_gate', 'new_v_ffn1_w_up', 'new_v_ffn1_w_down', 'new_v_mix_norm', 'new_v_w_in', 'new_v_mla_q_norm', 'new_v_mla_w_q_b', 'new_v_mla_kv_norm', 'new_v_mla_w_kv_b', 'new_v_swa_sinks', 'new_v_fox_forget_bias', 'new_v_w_out', 'new_v_ffn2_norm', 'new_v_ffn2_w_gate', 'new_v_ffn2_w_up', 'new_v_ffn2_w_down', 'new_v_final_norm']
TWIN_LEAF_KINDS = {'loss': 'loss', 'grad_x': 'grad_x', 'grad_ffn1_norm': 'grad_w', 'grad_ffn1_w_gate': 'grad_w', 'grad_ffn1_w_up': 'grad_w', 'grad_ffn1_w_down': 'grad_w', 'grad_mix_norm': 'grad_w', 'grad_w_in': 'grad_w', 'grad_mla_q_norm': 'grad_w', 'grad_mla_w_q_b': 'grad_w', 'grad_mla_kv_norm': 'grad_w', 'grad_mla_w_kv_b': 'grad_w', 'grad_swa_sinks': 'grad_w', 'grad_fox_forget_bias': 'grad_w', 'grad_w_out': 'grad_w', 'grad_ffn2_norm': 'grad_w', 'grad_ffn2_w_gate': 'grad_w', 'grad_ffn2_w_up': 'grad_w', 'grad_ffn2_w_down': 'grad_w', 'grad_final_norm': 'grad_w', 'delta_ffn1_norm': 'delta_w', 'delta_ffn1_w_gate': 'delta_w', 'delta_ffn1_w_up': 'delta_w', 'delta_ffn1_w_down': 'delta_w', 'delta_mix_norm': 'delta_w', 'delta_w_in': 'delta_w', 'delta_mla_q_norm': 'delta_w', 'delta_mla_w_q_b': 'delta_w', 'delta_mla_kv_norm': 'delta_w', 'delta_mla_w_kv_b': 'delta_w', 'delta_swa_sinks': 'delta_w', 'delta_fox_forget_bias': 'delta_w', 'delta_w_out': 'delta_w', 'delta_ffn2_norm': 'delta_w', 'delta_ffn2_w_gate': 'delta_w', 'delta_ffn2_w_up': 'delta_w', 'delta_ffn2_w_down': 'delta_w', 'delta_final_norm': 'delta_w', 'new_m_ffn1_norm': 'new_m', 'new_m_ffn1_w_gate': 'new_m', 'new_m_ffn1_w_up': 'new_m', 'new_m_ffn1_w_down': 'new_m', 'new_m_mix_norm': 'new_m', 'new_m_w_in': 'new_m', 'new_m_mla_q_norm': 'new_m', 'new_m_mla_w_q_b': 'new_m', 'new_m_mla_kv_norm': 'new_m', 'new_m_mla_w_kv_b': 'new_m', 'new_m_swa_sinks': 'new_m', 'new_m_fox_forget_bias': 'new_m', 'new_m_w_out': 'new_m', 'new_m_ffn2_norm': 'new_m', 'new_m_ffn2_w_gate': 'new_m', 'new_m_ffn2_w_up': 'new_m', 'new_m_ffn2_w_down': 'new_m', 'new_m_final_norm': 'new_m', 'new_v_ffn1_norm': 'new_v', 'new_v_ffn1_w_gate': 'new_v', 'new_v_ffn1_w_up': 'new_v', 'new_v_ffn1_w_down': 'new_v', 'new_v_mix_norm': 'new_v', 'new_v_w_in': 'new_v', 'new_v_mla_q_norm': 'new_v', 'new_v_mla_w_q_b': 'new_v', 'new_v_mla_kv_norm': 'new_v', 'new_v_mla_w_kv_b': 'new_v', 'new_v_swa_sinks': 'new_v', 'new_v_fox_forget_bias': 'new_v', 'new_v_w_out': 'new_v', 'new_v_ffn2_norm': 'new_v', 'new_v_ffn2_w_gate': 'new_v', 'new_v_ffn2_w_up': 'new_v', 'new_v_ffn2_w_down': 'new_v', 'new_v_final_norm': 'new_v'}


def _forward(args):
    return _fwd_reference(*[args[k] for k in FWD_PARAMS])


def _output_shape():
    def fwd():
        inp = _fwd_setup_inputs(0)
        return _fwd_reference(*[inp[k] for k in FWD_PARAMS])
    out = _jax.eval_shape(fwd)
    return out.shape, out.dtype

N_MICROBATCH = 1
ADAM_LR = 0.001
ADAM_B1 = 0.9
ADAM_B2 = 0.999
ADAM_EPS = 1e-08
ADAM_WD = 0.01
ADAM_STEP = 10
PER_EXAMPLE_BATCH_AXIS = {'x': 0, 'positions': 0, 'loss_target': 0}
SHARED_INPUTS = []
_WEIGHT_DTYPES = {'ffn1_norm': _jnp.float32, 'ffn1_w_gate': _jnp.float32, 'ffn1_w_up': _jnp.float32, 'ffn1_w_down': _jnp.float32, 'mix_norm': _jnp.float32, 'w_in': _jnp.float32, 'mla_q_norm': _jnp.float32, 'mla_w_q_b': _jnp.float32, 'mla_kv_norm': _jnp.float32, 'mla_w_kv_b': _jnp.float32, 'swa_sinks': _jnp.float32, 'fox_forget_bias': _jnp.float32, 'w_out': _jnp.float32, 'ffn2_norm': _jnp.float32, 'ffn2_w_gate': _jnp.float32, 'ffn2_w_up': _jnp.float32, 'ffn2_w_down': _jnp.float32, 'final_norm': _jnp.float32}
MOMENT_SCALE = {'ffn1_norm': 5.783482e-02, 'ffn1_w_gate': 2.479591e-02, 'ffn1_w_up': 2.405064e-02, 'ffn1_w_down': 3.986378e-02, 'mix_norm': 6.318347e-02, 'w_in': 5.089112e-02, 'mla_q_norm': 3.072477e-02, 'mla_w_q_b': 1.743487e-02, 'mla_kv_norm': 7.059493e-02, 'mla_w_kv_b': 2.382955e-02, 'swa_sinks': 2.211983e-02, 'fox_forget_bias': 6.416510e-01, 'w_out': 4.774859e-02, 'ffn2_norm': 4.951873e-02, 'ffn2_w_gate': 2.139300e-02, 'ffn2_w_up': 2.105022e-02, 'ffn2_w_down': 3.499282e-02, 'final_norm': 3.221509e+01}


def _to_microbatches(a, axis):
    t = _jnp.moveaxis(a, axis, 0)
    t = t.reshape((N_MICROBATCH, t.shape[0] // N_MICROBATCH) + t.shape[1:])
    return _jnp.moveaxis(t, 1, axis + 1)


def setup_inputs(seed: int = 0) -> dict:
    inp = _fwd_setup_inputs(seed)
    key = _jax.random.fold_in(_jax.random.key(seed), 7919)
    shape, _ = _output_shape()
    out = dict(inp)
    out["loss_target"] = _jax.random.normal(_jax.random.fold_in(key, 0), shape, _jnp.float32)
    for i, name in enumerate(TWIN_WEIGHTS):
        w = inp[name].astype(_jnp.float32)
        if MOMENT_SCALE is None:
            s = _jnp.sqrt(_jnp.mean(_jnp.square(w)) + 1e-30)
        else:
            s = MOMENT_SCALE[name]
        km, kv = _jax.random.split(_jax.random.fold_in(key, i + 1))
        out[name] = w
        out["m_" + name] = s * _jax.random.normal(km, w.shape, _jnp.float32)
        out["v_" + name] = (s * s) * _jax.random.uniform(kv, w.shape, _jnp.float32, 0.5, 1.5)
    if N_MICROBATCH > 1:
        for name, axis in PER_EXAMPLE_BATCH_AXIS.items():
            out[name] = _to_microbatches(out[name], axis)
    return {'x': out['x'], 'positions': out['positions'], 'ffn1_norm': out['ffn1_norm'], 'ffn1_w_gate': out['ffn1_w_gate'], 'ffn1_w_up': out['ffn1_w_up'], 'ffn1_w_down': out['ffn1_w_down'], 'mix_norm': out['mix_norm'], 'w_in': out['w_in'], 'mla_q_norm': out['mla_q_norm'], 'mla_w_q_b': out['mla_w_q_b'], 'mla_kv_norm': out['mla_kv_norm'], 'mla_w_kv_b': out['mla_w_kv_b'], 'swa_sinks': out['swa_sinks'], 'fox_forget_bias': out['fox_forget_bias'], 'w_out': out['w_out'], 'ffn2_norm': out['ffn2_norm'], 'ffn2_w_gate': out['ffn2_w_gate'], 'ffn2_w_up': out['ffn2_w_up'], 'ffn2_w_down': out['ffn2_w_down'], 'final_norm': out['final_norm'], 'loss_target': out['loss_target'], 'm_ffn1_norm': out['m_ffn1_norm'], 'm_ffn1_w_gate': out['m_ffn1_w_gate'], 'm_ffn1_w_up': out['m_ffn1_w_up'], 'm_ffn1_w_down': out['m_ffn1_w_down'], 'm_mix_norm': out['m_mix_norm'], 'm_w_in': out['m_w_in'], 'm_mla_q_norm': out['m_mla_q_norm'], 'm_mla_w_q_b': out['m_mla_w_q_b'], 'm_mla_kv_norm': out['m_mla_kv_norm'], 'm_mla_w_kv_b': out['m_mla_w_kv_b'], 'm_swa_sinks': out['m_swa_sinks'], 'm_fox_forget_bias': out['m_fox_forget_bias'], 'm_w_out': out['m_w_out'], 'm_ffn2_norm': out['m_ffn2_norm'], 'm_ffn2_w_gate': out['m_ffn2_w_gate'], 'm_ffn2_w_up': out['m_ffn2_w_up'], 'm_ffn2_w_down': out['m_ffn2_w_down'], 'm_final_norm': out['m_final_norm'], 'v_ffn1_norm': out['v_ffn1_norm'], 'v_ffn1_w_gate': out['v_ffn1_w_gate'], 'v_ffn1_w_up': out['v_ffn1_w_up'], 'v_ffn1_w_down': out['v_ffn1_w_down'], 'v_mix_norm': out['v_mix_norm'], 'v_w_in': out['v_w_in'], 'v_mla_q_norm': out['v_mla_q_norm'], 'v_mla_w_q_b': out['v_mla_w_q_b'], 'v_mla_kv_norm': out['v_mla_kv_norm'], 'v_mla_w_kv_b': out['v_mla_w_kv_b'], 'v_swa_sinks': out['v_swa_sinks'], 'v_fox_forget_bias': out['v_fox_forget_bias'], 'v_w_out': out['v_w_out'], 'v_ffn2_norm': out['v_ffn2_norm'], 'v_ffn2_w_gate': out['v_ffn2_w_gate'], 'v_ffn2_w_up': out['v_ffn2_w_up'], 'v_ffn2_w_down': out['v_ffn2_w_down'], 'v_final_norm': out['v_final_norm']}


def _loss(weights, diff, rest, loss_target):
    with _jax.named_scope("forward"):
        args = {**rest, TWIN_DIFF_INPUT: diff, **{k: w.astype(_WEIGHT_DTYPES[k]) for k, w in weights.items()}}
        y = _forward(args)
    with _jax.named_scope("loss_head"):
        err = _jnp.square(y.astype(_jnp.float32) - loss_target)
        return 0.5 * _jnp.sum(_jnp.mean(err, axis=-1)) if err.ndim else 0.5 * err


def _adamw(w, g, m, v):
    m = ADAM_B1 * m + (1.0 - ADAM_B1) * g
    v = ADAM_B2 * v + (1.0 - ADAM_B2) * _jnp.square(g)
    m_hat = m / (1.0 - ADAM_B1 ** ADAM_STEP)
    v_hat = v / (1.0 - ADAM_B2 ** ADAM_STEP)
    delta = -ADAM_LR * (m_hat / (_jnp.sqrt(v_hat) + ADAM_EPS) + ADAM_WD * w)
    return delta, m, v


def reference(x, positions, ffn1_norm, ffn1_w_gate, ffn1_w_up, ffn1_w_down, mix_norm, w_in, mla_q_norm, mla_w_q_b, mla_kv_norm, mla_w_kv_b, swa_sinks, fox_forget_bias, w_out, ffn2_norm, ffn2_w_gate, ffn2_w_up, ffn2_w_down, final_norm, loss_target, m_ffn1_norm, m_ffn1_w_gate, m_ffn1_w_up, m_ffn1_w_down, m_mix_norm, m_w_in, m_mla_q_norm, m_mla_w_q_b, m_mla_kv_norm, m_mla_w_kv_b, m_swa_sinks, m_fox_forget_bias, m_w_out, m_ffn2_norm, m_ffn2_w_gate, m_ffn2_w_up, m_ffn2_w_down, m_final_norm, v_ffn1_norm, v_ffn1_w_gate, v_ffn1_w_up, v_ffn1_w_down, v_mix_norm, v_w_in, v_mla_q_norm, v_mla_w_q_b, v_mla_kv_norm, v_mla_w_kv_b, v_swa_sinks, v_fox_forget_bias, v_w_out, v_ffn2_norm, v_ffn2_w_gate, v_ffn2_w_up, v_ffn2_w_down, v_final_norm):
    given = dict(x=x, positions=positions, ffn1_norm=ffn1_norm, ffn1_w_gate=ffn1_w_gate, ffn1_w_up=ffn1_w_up, ffn1_w_down=ffn1_w_down, mix_norm=mix_norm, w_in=w_in, mla_q_norm=mla_q_norm, mla_w_q_b=mla_w_q_b, mla_kv_norm=mla_kv_norm, mla_w_kv_b=mla_w_kv_b, swa_sinks=swa_sinks, fox_forget_bias=fox_forget_bias, w_out=w_out, ffn2_norm=ffn2_norm, ffn2_w_gate=ffn2_w_gate, ffn2_w_up=ffn2_w_up, ffn2_w_down=ffn2_w_down, final_norm=final_norm, loss_target=loss_target, m_ffn1_norm=m_ffn1_norm, m_ffn1_w_gate=m_ffn1_w_gate, m_ffn1_w_up=m_ffn1_w_up, m_ffn1_w_down=m_ffn1_w_down, m_mix_norm=m_mix_norm, m_w_in=m_w_in, m_mla_q_norm=m_mla_q_norm, m_mla_w_q_b=m_mla_w_q_b, m_mla_kv_norm=m_mla_kv_norm, m_mla_w_kv_b=m_mla_w_kv_b, m_swa_sinks=m_swa_sinks, m_fox_forget_bias=m_fox_forget_bias, m_w_out=m_w_out, m_ffn2_norm=m_ffn2_norm, m_ffn2_w_gate=m_ffn2_w_gate, m_ffn2_w_up=m_ffn2_w_up, m_ffn2_w_down=m_ffn2_w_down, m_final_norm=m_final_norm, v_ffn1_norm=v_ffn1_norm, v_ffn1_w_gate=v_ffn1_w_gate, v_ffn1_w_up=v_ffn1_w_up, v_ffn1_w_down=v_ffn1_w_down, v_mix_norm=v_mix_norm, v_w_in=v_w_in, v_mla_q_norm=v_mla_q_norm, v_mla_w_q_b=v_mla_w_q_b, v_mla_kv_norm=v_mla_kv_norm, v_mla_w_kv_b=v_mla_w_kv_b, v_swa_sinks=v_swa_sinks, v_fox_forget_bias=v_fox_forget_bias, v_w_out=v_w_out, v_ffn2_norm=v_ffn2_norm, v_ffn2_w_gate=v_ffn2_w_gate, v_ffn2_w_up=v_ffn2_w_up, v_ffn2_w_down=v_ffn2_w_down, v_final_norm=v_final_norm)
    weights = {n: given[n] for n in TWIN_WEIGHTS}
    shared = {n: given[n] for n in SHARED_INPUTS}
    per_example = {n: given[n] for n in ['x', 'positions']}
    grad_fn = _jax.value_and_grad(_loss, argnums=(0, 1))

    def one_microbatch(ex, loss_target):
        ex = dict(ex)
        diff = ex.pop(TWIN_DIFF_INPUT)
        return grad_fn(weights, diff, {**shared, **ex}, loss_target)

    if N_MICROBATCH == 1:
        loss, (grad_w, grad_x) = one_microbatch(per_example, given["loss_target"])
    else:
        def body(carry, xs):
            loss_sum, grad_sum = carry
            l_k, (gw_k, gx_k) = one_microbatch(xs[0], xs[1])
            with _jax.named_scope("update"):
                return (loss_sum + l_k, _jax.tree.map(_jnp.add, grad_sum, gw_k)), gx_k

        init = (_jnp.zeros((), _jnp.float32), _jax.tree.map(_jnp.zeros_like, weights))
        (loss, grad_w), grad_x = _jax.lax.scan(body, init, (per_example, given["loss_target"]))
    with _jax.named_scope("update"):
        delta_w, new_m, new_v = {}, {}, {}
        for n in TWIN_WEIGHTS:
            delta_w[n], new_m[n], new_v[n] = _adamw(weights[n], grad_w[n], given["m_" + n], given["v_" + n])
    return (loss, grad_x, *[grad_w[n] for n in TWIN_WEIGHTS], *[delta_w[n] for n in TWIN_WEIGHTS],
            *[new_m[n] for n in TWIN_WEIGHTS], *[new_v[n] for n in TWIN_WEIGHTS])
```

```python
import functools

import jax
import jax.numpy as jnp
from jax import lax
from jax.experimental import pallas as pl
from jax.experimental.pallas import tpu as pltpu

F32 = jnp.float32
BF16 = jnp.bfloat16

N_DEV = 8
DEPTH = 2
RMS_EPS = 1e-6
ROPE_THETA = 10000.0
HEADS = 8
MLA_Q_LORA = 512
MLA_KV_LORA = 256
MLA_NOPE = 128
MLA_ROPE = 64
MLA_V = 128
SWA_KV_HEADS = 2
HEAD_DIM = 64
WINDOW = 128
IN_COLS = 3144
IN_COLS_PAD = 3200

ADAM_LR = 0.001
ADAM_B1 = 0.9
ADAM_B2 = 0.999
ADAM_EPS = 1e-08
ADAM_WD = 0.01
ADAM_STEP = 10

LANES = 128
FLAT_QUANTUM = 1024 * LANES
NEG = -1e30
VMEM_LIMIT_BYTES = 48 * 1024 * 1024

EW_ROWS = 256
MM_TM = 512
MM_TN = 512
MM_TK = 512
ATT_T = 512

BIG = ("ffn1_w_gate", "ffn1_w_up", "ffn1_w_down", "w_in", "mla_w_q_b", "mla_w_kv_b", "w_out",
       "ffn2_w_gate", "ffn2_w_up", "ffn2_w_down")
ROW_SHARDED = ("ffn1_w_down", "w_out", "ffn2_w_down")
SMALL = ("ffn1_norm", "mix_norm", "mla_q_norm", "mla_kv_norm", "swa_sinks", "fox_forget_bias", "ffn2_norm",
         "final_norm")
WEIGHTS = ("ffn1_norm", "ffn1_w_gate", "ffn1_w_up", "ffn1_w_down", "mix_norm", "w_in", "mla_q_norm", "mla_w_q_b",
           "mla_kv_norm", "mla_w_kv_b", "swa_sinks", "fox_forget_bias", "w_out", "ffn2_norm", "ffn2_w_gate",
           "ffn2_w_up", "ffn2_w_down", "final_norm")


def _params(**kw):
    return pltpu.CompilerParams(vmem_limit_bytes=VMEM_LIMIT_BYTES, **kw)


def _tile(n, want):
    if n <= want:
        return n
    t = (want // LANES) * LANES
    while n % t:
        t -= LANES
    return t


def _rows(n, want):
    if n <= want:
        return n
    t = (want // 8) * 8
    while n % t:
        t -= 8
    return t


def _ew(name, fn, ins, outs, tm=EW_ROWS):
    tok = None
    for a, kind in ins:
        if kind == "row":
            tok = a.shape[-2]
    tm = _rows(tok, tm)
    steps = tok // tm

    def spec(shape, kind):
        if kind == "row":
            ax = len(shape) - 2
            blk = tuple(tm if d == ax else s for d, s in enumerate(shape))
            return pl.BlockSpec(blk, lambda i, ax=ax, n=len(shape): tuple(i if d == ax else 0 for d in range(n)))
        return pl.BlockSpec(tuple(shape), lambda i, n=len(shape): (0,) * n)

    n_in = len(ins)
    kinds = [k for _, _, k in outs]

    def body(*refs):
        vals = fn(*[r[...] for r in refs[:n_in]])
        for r, v, kind in zip(refs[n_in:], vals, kinds):
            if kind == "row":
                r[...] = v.astype(r.dtype)
            else:
                @pl.when(pl.program_id(0) == 0)
                def _(r=r):
                    r[...] = jnp.zeros(r.shape, r.dtype)
                r[...] += v.astype(r.dtype)

    res = pl.pallas_call(
        body, name=name, grid=(steps,),
        in_specs=[spec(a.shape, k) for a, k in ins],
        out_specs=[spec(s, k) for s, _, k in outs],
        out_shape=[jax.ShapeDtypeStruct(tuple(s), d) for s, d, _ in outs],
        compiler_params=_params(dimension_semantics=("arbitrary",)),
    )(*[a for a, _ in ins])
    return res


def _rms_fwd(name, x, g):
    def fn(x, g):
        r = lax.rsqrt(jnp.mean(x * x, axis=-1, keepdims=True) + RMS_EPS)
        return [x * r * g]
    return _ew(name, fn, [(x, "row"), (g, "full")], [(x.shape, BF16, "row")])[0]


def _rms_bwd(name, dh, x, g, res=None):
    def fn(dh, x, g, *rest):
        dh = dh.astype(F32)
        r = lax.rsqrt(jnp.mean(x * x, axis=-1, keepdims=True) + RMS_EPS)
        xh = x * r
        dxh = dh * g
        dx = r * (dxh - xh * jnp.mean(dxh * xh, axis=-1, keepdims=True))
        if rest:
            dx = dx + rest[0]
        return [dx, jnp.sum(dh * xh, axis=0, keepdims=True)]
    ins = [(dh, "row"), (x, "row"), (g, "full")] + ([(res, "row")] if res is not None else [])
    return _ew(name, fn, ins, [(x.shape, F32, "row"), (g.shape, F32, "acc")])


def _rope(name, x, xs, cos, sin, sign=1.0):
    def fn(x, xs, c, s):
        return [x * c + sign * (xs * s)]
    return _ew(name, fn, [(x, "row"), (xs, "row"), (cos, "row"), (sin, "row")], [(x.shape, F32, "row")])[0]


def _rope_tables(positions, inv_freq2):
    def fn(pos, f):
        ang = pos.astype(F32) * f
        return [jnp.cos(ang), jnp.sin(ang)]
    t = positions.shape[0]
    return _ew("rope_tables", fn, [(positions, "row"), (inv_freq2, "full")],
               [((t, 2 * 32), F32, "row"), ((t, 2 * 32), F32, "row")])


def _mm(name, lhs, rhs, terms, epi, out_dtypes, extras=(), ta=False, tb=False, tm=MM_TM, tn=MM_TN, tk=MM_TK):
    if ta:
        kdim, m = lhs[0].shape
    else:
        m, kdim = lhs[0].shape
    n = rhs[0].shape[0] if tb else rhs[0].shape[1]
    tm, tn, tk = _tile(m, tm), _tile(n, tn), _tile(kdim, tk)
    nk = kdim // tk
    n_acc = 1 + max(a for _, _, a in terms)
    nl, nr, ne = len(lhs), len(rhs), len(extras)
    dims = (((0 if ta else 1,), (1 if tb else 0,)), ((), ()))

    def body(*refs):
        l_refs, r_refs = refs[:nl], refs[nl:nl + nr]
        e_refs = refs[nl + nr:nl + nr + ne]
        o_refs = refs[nl + nr + ne:len(refs) - n_acc]
        accs = refs[len(refs) - n_acc:]
        k = pl.program_id(2)

        @pl.when(k == 0)
        def _():
            for acc in accs:
                acc[...] = jnp.zeros(acc.shape, F32)

        lv, rv = {}, {}
        for li, ri, ai in terms:
            if li not in lv:
                lv[li] = l_refs[li][...].astype(BF16)
            if ri not in rv:
                rv[ri] = r_refs[ri][...].astype(BF16)
            accs[ai][...] += lax.dot_general(lv[li], rv[ri], dims, preferred_element_type=F32)

        @pl.when(k == nk - 1)
        def _():
            outs = epi([acc[...] for acc in accs], [e[...] for e in e_refs])
            for o, v in zip(o_refs, outs):
                o[...] = v.astype(o.dtype)

    l_spec = pl.BlockSpec((tk, tm), lambda i, j, k: (k, i)) if ta else pl.BlockSpec((tm, tk), lambda i, j, k: (i, k))
    r_spec = pl.BlockSpec((tn, tk), lambda i, j, k: (j, k)) if tb else pl.BlockSpec((tk, tn), lambda i, j, k: (k, j))
    o_spec = pl.BlockSpec((tm, tn), lambda i, j, k: (i, j))
    return pl.pallas_call(
        body, name=name, grid=(m // tm, n // tn, nk),
        in_specs=[l_spec] * nl + [r_spec] * nr + [o_spec] * ne,
        out_specs=[o_spec] * len(out_dtypes),
        out_shape=[jax.ShapeDtypeStruct((m, n), d) for d in out_dtypes],
        scratch_shapes=[pltpu.VMEM((tm, tn), F32)] * n_acc,
        compiler_params=_params(dimension_semantics=("parallel", "parallel", "arbitrary")),
    )(*lhs, *rhs, *extras)


def _mm1(name, a, b, out_dtype=F32, scale=None, add=None, **kw):
    def epi(accs, ex):
        v = accs[0] if scale is None else accs[0] * scale
        return [v + ex[0] if ex else v]
    return _mm(name, [a], [b], [(0, 0, 0)], epi, [out_dtype], extras=[] if add is None else [add], **kw)[0]


def _scores(q, k, scale, cq, ck, q0, k0, window):
    s = lax.dot_general(q, k, (((1,), (1,)), ((), ())), preferred_element_type=F32) * scale
    if cq is not None:
        s = s + (cq - ck)
    qpos = q0 + lax.broadcasted_iota(jnp.int32, s.shape, 0)
    kpos = k0 + lax.broadcasted_iota(jnp.int32, s.shape, 1)
    mask = kpos <= qpos
    if window:
        mask = mask & (kpos > qpos - WINDOW)
    return s, mask


def _flash_fwd(name, q, k, v, scale, group=1, cq=None, ck=None, sink=None, window=False, t=ATT_T):
    h_n, tok, dq = q.shape
    dv = v.shape[-1]
    t = WINDOW if window else min(t, tok)
    nq = tok // t
    nj = 2 if window else nq
    bias = cq is not None
    has_sink = sink is not None

    def kv_blk(qi, j):
        return jnp.maximum(qi - 1 + j, 0) if window else jnp.minimum(j, qi)

    def body(*refs):
        q_ref, k_ref, v_ref = refs[:3]
        pos = 3
        cq_ref = ck_ref = sink_ref = None
        if bias:
            cq_ref, ck_ref = refs[pos], refs[pos + 1]
            pos += 2
        if has_sink:
            sink_ref = refs[pos]
            pos += 1
        o_ref, lse_ref, m_s, l_s, acc_s = refs[pos:]
        h, qi, j = pl.program_id(0), pl.program_id(1), pl.program_id(2)

        @pl.when(j == 0)
        def _():
            m_s[...] = jnp.full(m_s.shape, sink_ref[h] if has_sink else NEG, F32)
            l_s[...] = jnp.full(l_s.shape, 1.0 if has_sink else 0.0, F32)
            acc_s[...] = jnp.zeros(acc_s.shape, F32)

        kb = qi - 1 + j if window else j
        valid = (kb >= 0) if window else (j <= qi)

        @pl.when(valid)
        def _():
            s, mask = _scores(q_ref[...], k_ref[...], scale, cq_ref[...] if bias else None,
                              ck_ref[...] if bias else None, qi * t, kb * t, window)
            s = jnp.where(mask, s, NEG)
            m_prev = m_s[...]
            m_new = jnp.maximum(m_prev, jnp.max(s, axis=1, keepdims=True))
            alpha = jnp.exp(m_prev - m_new)
            p = jnp.exp(s - m_new)
            l_s[...] = alpha * l_s[...] + jnp.sum(p, axis=1, keepdims=True)
            acc_s[...] = alpha * acc_s[...] + jnp.dot(p.astype(BF16), v_ref[...], preferred_element_type=F32)
            m_s[...] = m_new

        @pl.when(j == nj - 1)
        def _():
            o_ref[...] = (acc_s[...] / l_s[...]).astype(o_ref.dtype)
            lse_ref[...] = m_s[...] + jnp.log(l_s[...])

    in_specs = [
        pl.BlockSpec((None, t, dq), lambda h, qi, j: (h, qi, 0)),
        pl.BlockSpec((None, t, dq), lambda h, qi, j: (h // group, kv_blk(qi, j), 0)),
        pl.BlockSpec((None, t, dv), lambda h, qi, j: (h // group, kv_blk(qi, j), 0)),
    ]
    args = [q, k, v]
    if bias:
        in_specs += [pl.BlockSpec((None, t, 1), lambda h, qi, j: (h, qi, 0)),
                     pl.BlockSpec((None, 1, t), lambda h, qi, j: (h, 0, kv_blk(qi, j)))]
        args += [cq, ck]
    if has_sink:
        in_specs.append(pl.BlockSpec(memory_space=pltpu.SMEM))
        args.append(sink)
    return pl.pallas_call(
        body, name=name, grid=(h_n, nq, nj),
        in_specs=in_specs,
        out_specs=[pl.BlockSpec((None, t, dv), lambda h, qi, j: (h, qi, 0)),
                   pl.BlockSpec((None, t, 1), lambda h, qi, j: (h, qi, 0))],
        out_shape=[jax.ShapeDtypeStruct((h_n, tok, dv), BF16), jax.ShapeDtypeStruct((h_n, tok, 1), F32)],
        scratch_shapes=[pltpu.VMEM((t, 1), F32), pltpu.VMEM((t, 1), F32), pltpu.VMEM((t, dv), F32)],
        compiler_params=_params(dimension_semantics=("parallel", "parallel", "arbitrary")),
    )(*args)


def _flash_bwd(name, q, k, v, do, lse, delta, scale, group=1, cq=None, ck=None, window=False, t=ATT_T):
    h_n, tok, dq = q.shape
    dv = v.shape[-1]
    t = WINDOW if window else min(t, tok)
    nq = tok // t
    nj = 2 if window else nq
    bias = cq is not None

    def q_blk(ki, j):
        return jnp.minimum(ki + j, nq - 1) if window else jnp.maximum(j, ki)

    def body(*refs):
        q_ref, k_ref, v_ref, do_ref, lse_ref, dl_ref = refs[:6]
        pos = 6
        cq_ref = ck_ref = None
        if bias:
            cq_ref, ck_ref = refs[pos], refs[pos + 1]
            pos += 2
        dq_ref, dk_ref, dv_ref = refs[pos:pos + 3]
        pos += 3
        dc_ref = dr_ref = None
        if bias:
            dc_ref, dr_ref = refs[pos], refs[pos + 1]
            pos += 2
        dk_s, dv_s = refs[pos], refs[pos + 1]
        dc_s = refs[pos + 2] if bias else None
        ki, j = pl.program_id(1), pl.program_id(2)

        @pl.when((ki == 0) & (j == 0))
        def _():
            dq_ref[...] = jnp.zeros(dq_ref.shape, F32)
            if bias:
                dr_ref[...] = jnp.zeros(dr_ref.shape, F32)

        @pl.when(j == 0)
        def _():
            dk_s[...] = jnp.zeros(dk_s.shape, F32)
            dv_s[...] = jnp.zeros(dv_s.shape, F32)
            if bias:
                dc_s[...] = jnp.zeros(dc_s.shape, F32)

        qb = ki + j if window else j
        valid = (qb < nq) if window else (j >= ki)

        @pl.when(valid)
        def _():
            qv, kv, dov = q_ref[...], k_ref[...], do_ref[...]
            s, mask = _scores(qv, kv, scale, cq_ref[...] if bias else None, ck_ref[...] if bias else None,
                              qb * t, ki * t, window)
            p = jnp.where(mask, jnp.exp(s - lse_ref[...]), 0.0)
            pb = p.astype(BF16)
            dv_s[...] += lax.dot_general(pb, dov, (((0,), (0,)), ((), ())), preferred_element_type=F32)
            dp = lax.dot_general(dov, v_ref[...], (((1,), (1,)), ((), ())), preferred_element_type=F32)
            ds = p * (dp - dl_ref[...])
            dsb = ds.astype(BF16)
            rows = pl.ds(pl.multiple_of(qb * t, t), t)
            dq_ref[rows, :] += scale * jnp.dot(dsb, kv, preferred_element_type=F32)
            dk_s[...] += scale * lax.dot_general(dsb, qv, (((0,), (0,)), ((), ())), preferred_element_type=F32)
            if bias:
                dc_s[...] += jnp.sum(ds, axis=0, keepdims=True)
                dr_ref[rows, :] += jnp.sum(ds, axis=1, keepdims=True)

        @pl.when(j == nj - 1)
        def _():
            dk_ref[...] = dk_s[...]
            dv_ref[...] = dv_s[...]
            if bias:
                dc_ref[...] = dc_s[...]

    def qmap(h, ki, j):
        return (h, q_blk(ki, j), 0)

    in_specs = [
        pl.BlockSpec((None, t, dq), qmap),
        pl.BlockSpec((None, t, dq), lambda h, ki, j: (h // group, ki, 0)),
        pl.BlockSpec((None, t, dv), lambda h, ki, j: (h // group, ki, 0)),
        pl.BlockSpec((None, t, dv), qmap),
        pl.BlockSpec((None, t, 1), qmap),
        pl.BlockSpec((None, t, 1), qmap),
    ]
    args = [q, k, v, do, lse, delta]
    out_specs = [pl.BlockSpec((None, tok, dq), lambda h, ki, j: (h, 0, 0)),
                 pl.BlockSpec((None, t, dq), lambda h, ki, j: (h, ki, 0)),
                 pl.BlockSpec((None, t, dv), lambda h, ki, j: (h, ki, 0))]
    out_shape = [jax.ShapeDtypeStruct((h_n, tok, dq), F32), jax.ShapeDtypeStruct((h_n, tok, dq), F32),
                 jax.ShapeDtypeStruct((h_n, tok, dv), F32)]
    scratch = [pltpu.VMEM((t, dq), F32), pltpu.VMEM((t, dv), F32)]
    if bias:
        in_specs += [pl.BlockSpec((None, t, 1), qmap), pl.BlockSpec((None, 1, t), lambda h, ki, j: (h, 0, ki))]
        args += [cq, ck]
        out_specs += [pl.BlockSpec((None, 1, t), lambda h, ki, j: (h, 0, ki)),
                      pl.BlockSpec((None, tok, 1), lambda h, ki, j: (h, 0, 0))]
        out_shape += [jax.ShapeDtypeStruct((h_n, 1, tok), F32), jax.ShapeDtypeStruct((h_n, tok, 1), F32)]
        scratch.append(pltpu.VMEM((1, t), F32))
    return pl.pallas_call(
        body, name=name, grid=(h_n, nq, nj),
        in_specs=in_specs, out_specs=out_specs, out_shape=out_shape, scratch_shapes=scratch,
        compiler_params=_params(dimension_semantics=("parallel", "arbitrary", "arbitrary")),
    )(*args)


def _delta(name, do, o):
    def fn(do, o):
        return [jnp.sum(do.astype(F32) * o.astype(F32), axis=-1, keepdims=True)]
    return _ew(name, fn, [(do, "row"), (o, "row")], [(do.shape[:2] + (1,), F32, "row")], tm=512)[0]


def _sink_grad(name, sink3, lse, delta):
    def fn(sk, lse, dl):
        return [-jnp.sum(jnp.exp(sk - lse) * dl, axis=1, keepdims=True)]
    return _ew(name, fn, [(sink3, "full"), (lse, "row"), (delta, "row")], [(sink3.shape, F32, "acc")], tm=512)[0]


def _log_sigmoid(z):
    return jnp.minimum(z, 0.0) - jnp.log(1.0 + jnp.exp(-jnp.abs(z)))


def _gate_fwd(zt, bias):
    tok = zt.shape[1]

    def body(z_ref, b_ref, c_ref):
        x = _log_sigmoid(z_ref[...] + b_ref[...])
        lane = lax.broadcasted_iota(jnp.int32, x.shape, 1)
        k = 1
        while k < tok:
            x = x + jnp.where(lane >= k, pltpu.roll(x, k, axis=1), 0.0)
            k *= 2
        c_ref[...] = x

    return pl.pallas_call(body, name="fox_gate_fwd", out_shape=jax.ShapeDtypeStruct(zt.shape, F32),
                          compiler_params=_params())(zt, bias)


def _gate_bwd(d_rows, d_cols, zt, bias):
    tok = zt.shape[1]

    def body(dr_ref, dc_ref, z_ref, b_ref, dz_ref, db_ref):
        x = dr_ref[...] - dc_ref[...]
        lane = lax.broadcasted_iota(jnp.int32, x.shape, 1)
        k = 1
        while k < tok:
            x = x + jnp.where(lane < tok - k, pltpu.roll(x, tok - k, axis=1), 0.0)
            k *= 2
        dz = x / (1.0 + jnp.exp(z_ref[...] + b_ref[...]))
        dz_ref[...] = dz
        db_ref[...] = jnp.sum(dz, axis=1, keepdims=True)

    return pl.pallas_call(body, name="fox_gate_bwd",
                          out_shape=[jax.ShapeDtypeStruct(zt.shape, F32), jax.ShapeDtypeStruct(bias.shape, F32)],
                          compiler_params=_params())(d_rows, d_cols, zt, bias)


def _ffn_fwd(tag, x, g, wg, wu, wd):
    h = _rms_fwd(tag + "_norm", x, g)

    def epi(accs, ex):
        u, v = accs
        return [u, v, u * jax.nn.sigmoid(u) * v]
    u, v, a = _mm(tag + "_gate_up", [h], [wg, wu], [(0, 0, 0), (0, 1, 1)], epi, [BF16, BF16, BF16])
    y = _mm1(tag + "_down", a, wd, scale=0.5, add=x)
    return y, (x, h, u, v, a)


def _ffn_bwd(tag, dy, saved, g, wg, wu, wd):
    x, h, u, v, a = saved

    def epi(accs, ex):
        da = 0.5 * accs[0]
        u, v = ex[0].astype(F32), ex[1].astype(F32)
        sg = jax.nn.sigmoid(u)
        return [da * v * (sg * (1.0 + u * (1.0 - sg))), da * (u * sg)]
    du, dv = _mm(tag + "_d_act", [dy], [wd], [(0, 0, 0)], epi, [BF16, BF16], extras=[u, v], tb=True)
    d_wd = _mm1(tag + "_d_wd", a, dy, scale=0.5, ta=True)
    d_wg, d_wu = _mm(tag + "_d_wgu", [h], [du, dv], [(0, 0, 0), (0, 1, 1)], lambda accs, ex: accs, [F32, F32],
                     ta=True)
    dh = _mm(tag + "_d_h", [du, dv], [wg, wu], [(0, 0, 0), (1, 1, 0)], lambda accs, ex: accs, [F32], tb=True)[0]
    dx, dg = _rms_bwd(tag + "_d_norm", dh, x, g, res=dy)
    return dx, dg, d_wg, d_wu, d_wd


def _swap_halves(x):
    t, w = x.shape
    return x.reshape(t, w // HEAD_DIM, 2, HEAD_DIM // 2)[:, :, ::-1, :].reshape(t, w)


def _heads(x, n):
    t, w = x.shape
    return x.reshape(t, n, w // n).transpose(1, 0, 2).astype(BF16)


def _unheads(x):
    n, t, d = x.shape
    return x.transpose(1, 0, 2).reshape(t, n * d)


def _mixer_fwd(tag, x, w, cos, sin):
    tok = x.shape[0]
    h2 = _rms_fwd(tag + "_norm", x, w["mix_norm"])
    p = _mm1(tag + "_in", h2, w["w_in"], tn=640)
    c_q, c_kv = p[:, :512], p[:, 512:768]
    q_s, k_s, v_s = p[:, 768:1280], p[:, 1280:1408], p[:, 1408:1536]
    q_f, k_f, v_f = p[:, 1536:2048], p[:, 2048:2560], p[:, 2560:3072]
    k_rope, f_logit = p[:, 3072:3136], p[:, 3136:3144]

    qn = _rms_fwd(tag + "_q_norm", c_q, w["mla_q_norm"])
    qm = _mm1(tag + "_q_b", qn, w["mla_w_q_b"])
    kvn = _rms_fwd(tag + "_kv_norm", c_kv, w["mla_kv_norm"])
    kvm = _mm1(tag + "_kv_b", kvn, w["mla_w_kv_b"])

    rin = jnp.concatenate([qm[:, 1024:], q_s, k_s, k_rope, jnp.zeros((tok, 64), F32)], axis=1)
    rout = _rope(tag + "_rope", rin, _swap_halves(rin), cos, sin)
    q_pe, q_sr, k_sr, k_pe = rout[:, :512], rout[:, 512:1024], rout[:, 1024:1152], rout[:, 1152:1216]

    q_m = jnp.concatenate([qm[:, :1024].reshape(tok, HEADS, MLA_NOPE), q_pe.reshape(tok, HEADS, MLA_ROPE)], axis=-1)
    q_m = q_m.transpose(1, 0, 2).astype(BF16)
    k_m = jnp.concatenate([kvm[:, :1024].reshape(tok, HEADS, MLA_NOPE),
                           jnp.broadcast_to(k_pe[:, None, :], (tok, HEADS, MLA_ROPE))], axis=-1)
    k_m = k_m.transpose(1, 0, 2).astype(BF16)
    v_m = _heads(kvm[:, 1024:], HEADS)
    o_mla, lse_mla = _flash_fwd(tag + "_mla_fwd", q_m, k_m, v_m, (MLA_NOPE + MLA_ROPE) ** -0.5)

    q_sh, k_sh, v_sh = _heads(q_sr, HEADS), _heads(k_sr, SWA_KV_HEADS), _heads(v_s, SWA_KV_HEADS)
    o_swa, lse_swa = _flash_fwd(tag + "_swa_fwd", q_sh, k_sh, v_sh, HEAD_DIM ** -0.5,
                                group=HEADS // SWA_KV_HEADS, sink=w["swa_sinks"], window=True)

    zt = f_logit.T
    c = _gate_fwd(zt, w["fox_forget_bias"].reshape(HEADS, 1))
    cq, ck = c[:, :, None], c[:, None, :]
    q_fh, k_fh, v_fh = _heads(q_f, HEADS), _heads(k_f, HEADS), _heads(v_f, HEADS)
    o_fox, lse_fox = _flash_fwd(tag + "_fox_fwd", q_fh, k_fh, v_fh, HEAD_DIM ** -0.5, cq=cq, ck=ck)

    mixed = jnp.concatenate([_unheads(o_mla), _unheads(o_swa), _unheads(o_fox)], axis=1)
    y = _mm1(tag + "_out", mixed, w["w_out"], add=x)
    saved = dict(x=x, h2=h2, c_q=c_q, c_kv=c_kv, qn=qn, kvn=kvn, zt=zt, cq=cq, ck=ck, mixed=mixed,
                 mla=(q_m, k_m, v_m, o_mla, lse_mla), swa=(q_sh, k_sh, v_sh, o_swa, lse_swa),
                 fox=(q_fh, k_fh, v_fh, o_fox, lse_fox))
    return y, saved


def _mixer_bwd(tag, dy, s, w, cos, sin):
    tok = dy.shape[0]
    g = {}
    dmixed = _mm1(tag + "_d_mixed", dy, w["w_out"], out_dtype=BF16, tb=True)
    g["w_out"] = _mm1(tag + "_d_wout", s["mixed"], dy, ta=True)
    do_mla = _heads(dmixed[:, :1024], HEADS)
    do_swa = _heads(dmixed[:, 1024:1536], HEADS)
    do_fox = _heads(dmixed[:, 1536:], HEADS)

    q_m, k_m, v_m, o_mla, lse_mla = s["mla"]
    dl = _delta(tag + "_mla_delta", do_mla, o_mla)
    dq_m, dk_m, dv_m = _flash_bwd(tag + "_mla_bwd", q_m, k_m, v_m, do_mla, lse_mla, dl,
                                  (MLA_NOPE + MLA_ROPE) ** -0.5)

    q_sh, k_sh, v_sh, o_swa, lse_swa = s["swa"]
    dl = _delta(tag + "_swa_delta", do_swa, o_swa)
    g["swa_sinks"] = _sink_grad(tag + "_d_sink", w["swa_sinks"].reshape(HEADS, 1, 1), lse_swa, dl).reshape(HEADS)
    dq_sh, dk_sh, dv_sh = _flash_bwd(tag + "_swa_bwd", q_sh, k_sh, v_sh, do_swa, lse_swa, dl, HEAD_DIM ** -0.5,
                                     group=HEADS // SWA_KV_HEADS, window=True)

    q_fh, k_fh, v_fh, o_fox, lse_fox = s["fox"]
    dl = _delta(tag + "_fox_delta", do_fox, o_fox)
    dq_fh, dk_fh, dv_fh, d_cols, d_rows = _flash_bwd(tag + "_fox_bwd", q_fh, k_fh, v_fh, do_fox, lse_fox, dl,
                                                     HEAD_DIM ** -0.5, cq=s["cq"], ck=s["ck"])
    dzt, dbias = _gate_bwd(d_rows[:, :, 0], d_cols[:, 0, :], s["zt"], w["fox_forget_bias"].reshape(HEADS, 1))
    g["fox_forget_bias"] = dbias.reshape(HEADS)

    grp = HEADS // SWA_KV_HEADS
    dq_mt = dq_m.transpose(1, 0, 2)
    dk_mt = dk_m.transpose(1, 0, 2)
    d_qpe = dq_mt[:, :, MLA_NOPE:].reshape(tok, HEADS * MLA_ROPE)
    d_kpe_heads = dk_mt[:, :, MLA_NOPE:].reshape(tok, HEADS * MLA_ROPE)
    d_qs = _unheads(dq_sh)
    d_ks_heads = dk_sh.reshape(SWA_KV_HEADS, grp, tok, HEAD_DIM).transpose(2, 1, 0, 3).reshape(tok, grp * 128)
    d_vs_heads = dv_sh.reshape(SWA_KV_HEADS, grp, tok, HEAD_DIM).transpose(2, 1, 0, 3).reshape(tok, grp * 128)

    def fold(d_kpe_h, d_ks_h, d_vs_h):
        kpe = d_kpe_h[:, 0:64]
        for i in range(1, HEADS):
            kpe = kpe + d_kpe_h[:, 64 * i:64 * (i + 1)]
        ks, vs = d_ks_h[:, 0:128], d_vs_h[:, 0:128]
        for i in range(1, grp):
            ks = ks + d_ks_h[:, 128 * i:128 * (i + 1)]
            vs = vs + d_vs_h[:, 128 * i:128 * (i + 1)]
        return [jnp.concatenate([kpe, jnp.zeros_like(kpe)], axis=1), ks, vs]
    d_kpe2, d_ksr, d_vs = _ew(tag + "_fold_heads", fold,
                              [(d_kpe_heads, "row"), (d_ks_heads, "row"), (d_vs_heads, "row")],
                              [((tok, 128), F32, "row"), ((tok, 128), F32, "row"), ((tok, 128), F32, "row")])

    rin = jnp.concatenate([d_qpe, d_qs, d_ksr, d_kpe2], axis=1)
    rout = _rope(tag + "_d_rope", rin, _swap_halves(rin), cos, sin, sign=-1.0)
    d_qpe_pre, d_qs_pre, d_ks_pre, d_krope = rout[:, :512], rout[:, 512:1024], rout[:, 1024:1152], rout[:, 1152:1216]

    d_qm = jnp.concatenate([dq_mt[:, :, :MLA_NOPE].reshape(tok, HEADS * MLA_NOPE), d_qpe_pre], axis=1)
    d_kvm = jnp.concatenate([dk_mt[:, :, :MLA_NOPE].reshape(tok, HEADS * MLA_NOPE), _unheads(dv_m)], axis=1)
    g["mla_w_q_b"] = _mm1(tag + "_d_wqb", s["qn"], d_qm, ta=True)
    d_qn = _mm1(tag + "_d_qn", d_qm, w["mla_w_q_b"], tb=True)
    d_cq, g["mla_q_norm"] = _rms_bwd(tag + "_d_q_norm", d_qn, s["c_q"], w["mla_q_norm"])
    g["mla_w_kv_b"] = _mm1(tag + "_d_wkvb", s["kvn"], d_kvm, ta=True)
    d_kvn = _mm1(tag + "_d_kvn", d_kvm, w["mla_w_kv_b"], tb=True)
    d_ckv, g["mla_kv_norm"] = _rms_bwd(tag + "_d_kv_norm", d_kvn, s["c_kv"], w["mla_kv_norm"])

    dp = jnp.concatenate([d_cq, d_ckv, d_qs_pre, d_ks_pre, d_vs, _unheads(dq_fh), _unheads(dk_fh), _unheads(dv_fh),
                          d_krope, dzt.T, jnp.zeros((tok, IN_COLS_PAD - IN_COLS), F32)], axis=1).astype(BF16)
    g["w_in"] = _mm1(tag + "_d_win", s["h2"], dp, ta=True, tn=640)
    dh2 = _mm1(tag + "_d_h2", dp, w["w_in"], tb=True, tk=640)
    dx, g["mix_norm"] = _rms_bwd(tag + "_d_norm", dh2, s["x"], w["mix_norm"], res=dy)
    return dx, g


def _loss_head(x, g, target):
    d = x.shape[1]

    def fn(x, g, tgt):
        r = lax.rsqrt(jnp.mean(x * x, axis=-1, keepdims=True) + RMS_EPS)
        xh = x * r
        err = xh * g - tgt
        loss = 0.5 * jnp.sum(jnp.sum(err * err, axis=-1, keepdims=True), axis=0, keepdims=True) / d
        dy = err / d
        dxh = dy * g
        dx = r * (dxh - xh * jnp.mean(dxh * xh, axis=-1, keepdims=True))
        return [dx, jnp.sum(dy * xh, axis=0, keepdims=True), loss]
    return _ew("loss_head", fn, [(x, "row"), (g, "full"), (target, "row")],
               [(x.shape, F32, "row"), (g.shape, F32, "acc"), ((1, 1), F32, "acc")])


def _adamw(name, w, g, m, v):
    def fn(w, g, m, v):
        m = ADAM_B1 * m + (1.0 - ADAM_B1) * g
        v = ADAM_B2 * v + (1.0 - ADAM_B2) * (g * g)
        m_hat = m / (1.0 - ADAM_B1 ** ADAM_STEP)
        v_hat = v / (1.0 - ADAM_B2 ** ADAM_STEP)
        return [-ADAM_LR * (m_hat / (jnp.sqrt(v_hat) + ADAM_EPS) + ADAM_WD * w), m, v]
    return _ew(name, fn, [(w, "row"), (g, "row"), (m, "row"), (v, "row")], [(w.shape, F32, "row")] * 3)


def _sum_slots(name, r):
    def fn(r):
        acc = r[0].astype(F32)
        for i in range(1, N_DEV):
            acc = acc + r[i].astype(F32)
        return [acc]
    return _ew(name, fn, [(r, "row")], [(r.shape[1:], F32, "row")], tm=1024)[0]


def _coords(dev):
    return (dev // 4, (dev // 2) % 2, dev % 2)


def _all_gather(shard):
    rows, lanes = shard.shape

    def body(x_ref, out_ref, send_sems, recv_sems, local_sem):
        x, y, c = lax.axis_index("x"), lax.axis_index("y"), lax.axis_index("c")
        me, sibling = (x, y, c), (x, y, 1 - c)
        chips = [(1 - x, y), (x, 1 - y), (1 - x, 1 - y)]

        def slot(px, py, pc):
            return out_ref.at[4 * px + 2 * py + pc]

        def copy(k, block, to, src=None):
            return pltpu.make_async_remote_copy(
                src_ref=slot(*block) if src is None else src, dst_ref=slot(*block),
                send_sem=send_sems.at[k], recv_sem=recv_sems.at[k],
                device_id=to, device_id_type=pl.DeviceIdType.MESH)

        mine = pltpu.make_async_copy(x_ref, slot(*me), local_sem)
        mine.start()
        first = [copy(0, me, sibling, src=x_ref)]
        first += [copy(1 + j, me, (*chip, c), src=x_ref) for j, chip in enumerate(chips)]
        for cp in first:
            cp.start()
        passed = [copy(4 + j, (*chip, c), sibling) for j, chip in enumerate(chips)]
        for j, chip in enumerate(chips):
            copy(1 + j, (*chip, c), me).wait_recv()
            passed[j].start()
        copy(0, sibling, me).wait_recv()
        for j, chip in enumerate(chips):
            copy(4 + j, (*chip, 1 - c), me).wait_recv()
        for cp in first + passed:
            cp.wait_send()
        mine.wait()

    return pl.pallas_call(
        body, name="all_gather_weights",
        out_shape=jax.ShapeDtypeStruct((N_DEV, rows, lanes), shard.dtype),
        in_specs=[pl.BlockSpec(memory_space=pl.ANY)],
        out_specs=pl.BlockSpec(memory_space=pl.ANY),
        scratch_shapes=[pltpu.SemaphoreType.DMA((7,)), pltpu.SemaphoreType.DMA((7,)), pltpu.SemaphoreType.DMA],
    )(shard)


def _all_to_all(blocks):
    def body(g_ref, out_ref, send_sems, recv_sems, local_sem):
        me = 4 * lax.axis_index("x") + 2 * lax.axis_index("y") + lax.axis_index("c")

        def copy(peer):
            return pltpu.make_async_remote_copy(
                src_ref=g_ref.at[peer], dst_ref=out_ref.at[me],
                send_sem=send_sems.at[peer], recv_sem=recv_sems.at[me],
                device_id=_coords(peer), device_id_type=pl.DeviceIdType.MESH)

        def arrival(peer):
            return pltpu.make_async_remote_copy(
                src_ref=g_ref.at[peer], dst_ref=out_ref.at[peer],
                send_sem=send_sems.at[peer], recv_sem=recv_sems.at[peer],
                device_id=_coords(peer), device_id_type=pl.DeviceIdType.MESH)

        mine = pltpu.make_async_copy(g_ref.at[me], out_ref.at[me], local_sem)
        mine.start()
        for peer in range(N_DEV):
            @pl.when(peer != me)
            def _(peer=peer):
                copy(peer).start()
        for peer in range(N_DEV):
            @pl.when(peer != me)
            def _(peer=peer):
                arrival(peer).wait_recv()
        for peer in range(N_DEV):
            @pl.when(peer != me)
            def _(peer=peer):
                copy(peer).wait_send()
        mine.wait()

    return pl.pallas_call(
        body, name="all_to_all_grads",
        out_shape=jax.ShapeDtypeStruct(blocks.shape, blocks.dtype),
        in_specs=[pl.BlockSpec(memory_space=pl.ANY)],
        out_specs=pl.BlockSpec(memory_space=pl.ANY),
        scratch_shapes=[pltpu.SemaphoreType.DMA((N_DEV,)), pltpu.SemaphoreType.DMA((N_DEV,)),
                        pltpu.SemaphoreType.DMA],
    )(blocks)


def _all_reduce_small(buf):
    def body(x_ref, out_ref, slots, send_sems, recv_sems):
        me = 4 * lax.axis_index("x") + 2 * lax.axis_index("y") + lax.axis_index("c")

        def copy(peer):
            return pltpu.make_async_remote_copy(
                src_ref=x_ref, dst_ref=slots.at[me],
                send_sem=send_sems.at[peer], recv_sem=recv_sems.at[me],
                device_id=_coords(peer), device_id_type=pl.DeviceIdType.MESH)

        def arrival(peer):
            return pltpu.make_async_remote_copy(
                src_ref=x_ref, dst_ref=slots.at[peer],
                send_sem=send_sems.at[peer], recv_sem=recv_sems.at[peer],
                device_id=_coords(peer), device_id_type=pl.DeviceIdType.MESH)

        slots[pl.ds(me, 1)] = x_ref[...][None]
        for peer in range(N_DEV):
            @pl.when(peer != me)
            def _(peer=peer):
                copy(peer).start()
        for peer in range(N_DEV):
            @pl.when(peer != me)
            def _(peer=peer):
                arrival(peer).wait_recv()
        for peer in range(N_DEV):
            @pl.when(peer != me)
            def _(peer=peer):
                copy(peer).wait_send()
        acc = slots[0]
        for peer in range(1, N_DEV):
            acc = acc + slots[peer]
        out_ref[...] = acc

    return pl.pallas_call(
        body, name="all_reduce_small",
        out_shape=jax.ShapeDtypeStruct(buf.shape, F32),
        in_specs=[pl.BlockSpec(memory_space=pltpu.VMEM)],
        out_specs=pl.BlockSpec(memory_space=pltpu.VMEM),
        scratch_shapes=[pltpu.VMEM((N_DEV,) + buf.shape, F32), pltpu.SemaphoreType.DMA((N_DEV,)),
                        pltpu.SemaphoreType.DMA((N_DEV,))],
        compiler_params=_params(),
    )(buf)


def _in_to_kernel(w):
    pad = jnp.zeros(w.shape[:-1] + (IN_COLS_PAD - IN_COLS,), w.dtype)
    return jnp.concatenate([w[..., :768], w[..., 832:3136], w[..., 768:832], w[..., 3136:], pad], axis=-1)


def _in_from_kernel(w):
    return jnp.concatenate([w[..., :768], w[..., 3072:3136], w[..., 768:3072], w[..., 3136:3144]], axis=-1)


def _split_to_kernel(w, a, b):
    r = w.shape[0]
    w3 = w.reshape(r, HEADS, a + b)
    return jnp.concatenate([w3[:, :, :a].reshape(r, HEADS * a), w3[:, :, a:].reshape(r, HEADS * b)], axis=1)


def _split_from_kernel(w, a, b):
    r = w.shape[0]
    return jnp.concatenate([w[:, :HEADS * a].reshape(r, HEADS, a), w[:, HEADS * a:].reshape(r, HEADS, b)],
                           axis=-1).reshape(r, HEADS * (a + b))


def _local_step(x, positions, target, full, small):
    tok = x.shape[0]
    inv_freq = ROPE_THETA ** (-jnp.arange(0, HEAD_DIM, 2, dtype=F32) / HEAD_DIM)
    cos64, sin64 = _rope_tables(positions, jnp.concatenate([-inv_freq, inv_freq])[None, :])
    cos = jnp.tile(cos64, (1, 20))
    sin = jnp.tile(sin64, (1, 20))

    layers = []
    for l in range(DEPTH):
        w = {k: full[k][l] for k in BIG}
        w["w_in"] = _in_to_kernel(w["w_in"])
        w["mla_w_q_b"] = _split_to_kernel(w["mla_w_q_b"], MLA_NOPE, MLA_ROPE)
        w["mla_w_kv_b"] = _split_to_kernel(w["mla_w_kv_b"], MLA_NOPE, MLA_V)
        for k in SMALL:
            if k != "final_norm":
                w[k] = small[k][l][None, :] if "norm" in k else small[k][l]
        layers.append(w)

    saved = []
    for l, w in enumerate(layers):
        t = "l%d" % l
        x, s1 = _ffn_fwd(t + "_ffn1", x, w["ffn1_norm"], w["ffn1_w_gate"], w["ffn1_w_up"], w["ffn1_w_down"])
        x, s2 = _mixer_fwd(t + "_mix", x, w, cos, sin)
        x, s3 = _ffn_fwd(t + "_ffn2", x, w["ffn2_norm"], w["ffn2_w_gate"], w["ffn2_w_up"], w["ffn2_w_down"])
        saved.append((s1, s2, s3))

    dx, d_final, loss = _loss_head(x, small["final_norm"][None, :], target)

    grads = [None] * DEPTH
    for l in reversed(range(DEPTH)):
        w, (s1, s2, s3) = layers[l], saved[l]
        t = "l%d" % l
        g = {}
        dx, g["ffn2_norm"], g["ffn2_w_gate"], g["ffn2_w_up"], g["ffn2_w_down"] = _ffn_bwd(
            t + "_ffn2", dx, s3, w["ffn2_norm"], w["ffn2_w_gate"], w["ffn2_w_up"], w["ffn2_w_down"])
        dx, gm = _mixer_bwd(t + "_mix", dx, s2, w, cos, sin)
        g.update(gm)
        dx, g["ffn1_norm"], g["ffn1_w_gate"], g["ffn1_w_up"], g["ffn1_w_down"] = _ffn_bwd(
            t + "_ffn1", dx, s1, w["ffn1_norm"], w["ffn1_w_gate"], w["ffn1_w_up"], w["ffn1_w_down"])
        g["w_in"] = _in_from_kernel(g["w_in"])
        g["mla_w_q_b"] = _split_from_kernel(g["mla_w_q_b"], MLA_NOPE, MLA_ROPE)
        g["mla_w_kv_b"] = _split_from_kernel(g["mla_w_kv_b"], MLA_NOPE, MLA_V)
        for k in SMALL:
            if k != "final_norm":
                g[k] = g[k].reshape(-1)
        grads[l] = g
    out = {k: jnp.stack([grads[l][k] for l in range(DEPTH)]) for k in grads[0]}
    out["final_norm"] = d_final.reshape(-1)
    return loss, dx, out


def _flat_rows(a):
    return a.reshape(-1, LANES)


def kernel(x, positions, ffn1_norm, ffn1_w_gate, ffn1_w_up, ffn1_w_down, mix_norm, w_in, mla_q_norm, mla_w_q_b, mla_kv_norm, mla_w_kv_b, swa_sinks, fox_forget_bias, w_out, ffn2_norm, ffn2_w_gate, ffn2_w_up, ffn2_w_down, final_norm, loss_target, m_ffn1_norm, m_ffn1_w_gate, m_ffn1_w_up, m_ffn1_w_down, m_mix_norm, m_w_in, m_mla_q_norm, m_mla_w_q_b, m_mla_kv_norm, m_mla_w_kv_b, m_swa_sinks, m_fox_forget_bias, m_w_out, m_ffn2_norm, m_ffn2_w_gate, m_ffn2_w_up, m_ffn2_w_down, m_final_norm, v_ffn1_norm, v_ffn1_w_gate, v_ffn1_w_up, v_ffn1_w_down, v_mix_norm, v_w_in, v_mla_q_norm, v_mla_w_q_b, v_mla_kv_norm, v_mla_w_kv_b, v_swa_sinks, v_fox_forget_bias, v_w_out, v_ffn2_norm, v_ffn2_w_gate, v_ffn2_w_up, v_ffn2_w_down, v_final_norm):
    given = dict(locals())
    weights = {k: given[k] for k in WEIGHTS}
    mom_m = {k: given["m_" + k] for k in WEIGHTS}
    mom_v = {k: given["v_" + k] for k in WEIGHTS}

    sizes = [weights[k].size for k in BIG]
    total = sum(sizes)
    padded = -(-total // FLAT_QUANTUM) * FLAT_QUANTUM
    flat = jnp.concatenate([weights[k].astype(BF16).reshape(-1) for k in BIG]
                           + [jnp.zeros((padded - total,), BF16)])
    gathered = _all_gather(_flat_rows(flat))
    gathered = gathered.reshape(N_DEV, -1)
    full, off = {}, 0
    for k, n in zip(BIG, sizes):
        blk = gathered[:, off:off + n].reshape((N_DEV,) + weights[k].shape)
        off += n
        if k in ROW_SHARDED:
            d, r, c = weights[k].shape
            full[k] = blk.transpose(1, 0, 2, 3).reshape(d, N_DEV * r, c)
        else:
            d, r, c = weights[k].shape
            full[k] = blk.transpose(1, 2, 0, 3).reshape(d, r, N_DEV * c)

    small = {k: weights[k] for k in SMALL}
    loss, grad_x, grads = _local_step(x[0], positions[0][:, None], loss_target[0], full, small)

    parts = []
    for k in BIG:
        d, r, c = weights[k].shape
        g = grads[k]
        if k in ROW_SHARDED:
            blk = g.reshape(d, N_DEV, r, c).transpose(1, 0, 2, 3)
        else:
            blk = g.reshape(d, r, N_DEV, c).transpose(2, 0, 1, 3)
        parts.append(blk.astype(BF16).reshape(N_DEV, -1))
    send = jnp.concatenate(parts + [jnp.zeros((N_DEV, padded - total), BF16)], axis=1)
    received = _all_to_all(send.reshape(N_DEV, -1, LANES))
    g_flat = _sum_slots("sum_grad_blocks", received).reshape(-1)

    small_sizes = [weights[k].size for k in SMALL]
    sbuf = jnp.concatenate([grads[k].reshape(-1) for k in SMALL] + [loss.reshape(-1)])
    pad = (-sbuf.size) % (8 * LANES)
    sbuf = jnp.concatenate([sbuf, jnp.zeros((pad,), F32)])
    stot = _all_reduce_small(_flat_rows(sbuf)).reshape(-1)

    grad_w = {}
    off = 0
    for k, n in zip(BIG, sizes):
        grad_w[k] = g_flat[off:off + n].reshape(weights[k].shape)
        off += n
    off = 0
    for k, n in zip(SMALL, small_sizes):
        grad_w[k] = stot[off:off + n].reshape(weights[k].shape)
        off += n
    loss_total = stot[off]

    delta, new_m, new_v = {}, {}, {}
    for k in WEIGHTS:
        shape = weights[k].shape
        two_d = (-1, shape[-1]) if len(shape) > 1 else (1, -1)
        d, m, v = _adamw("adamw_" + k, weights[k].reshape(two_d), grad_w[k].reshape(two_d),
                         mom_m[k].reshape(two_d), mom_v[k].reshape(two_d))
        delta[k], new_m[k], new_v[k] = d.reshape(shape), m.reshape(shape), v.reshape(shape)

    return (loss_total, grad_x[None], *[grad_w[k] for k in WEIGHTS], *[delta[k] for k in WEIGHTS],
            *[new_m[k] for k in WEIGHTS], *[new_v[k] for k in WEIGHTS])
```

```python
import functools

import jax
import jax.numpy as jnp
from jax import lax
from jax.experimental import pallas as pl
from jax.experimental.pallas import tpu as pltpu

F32 = jnp.float32
BF16 = jnp.bfloat16

N_DEV = 8
DEPTH = 2
RMS_EPS = 1e-6
ROPE_THETA = 10000.0
HEADS = 8
MLA_Q_LORA = 512
MLA_KV_LORA = 256
MLA_NOPE = 128
MLA_ROPE = 64
MLA_V = 128
SWA_KV_HEADS = 2
HEAD_DIM = 64
WINDOW = 128
IN_COLS = 3144
IN_COLS_PAD = 3200

ADAM_LR = 0.001
ADAM_B1 = 0.9
ADAM_B2 = 0.999
ADAM_EPS = 1e-08
ADAM_WD = 0.01
ADAM_STEP = 10

LANES = 128
NEG = -1e30
LOG2E = 1.4426950408889634
LN2 = 0.6931471805599453
VMEM_LIMIT_BYTES = 48 * 1024 * 1024

EW_ROWS = 256
MM_TM = 512
MM_TN = 512
MM_TK = 512
ATT_T = 512

BIG = ("ffn1_w_gate", "ffn1_w_up", "ffn1_w_down", "w_in", "mla_w_q_b", "mla_w_kv_b", "w_out",
       "ffn2_w_gate", "ffn2_w_up", "ffn2_w_down")
ROW_SHARDED = ("ffn1_w_down", "w_out", "ffn2_w_down")
SMALL = ("ffn1_norm", "mix_norm", "mla_q_norm", "mla_kv_norm", "swa_sinks", "fox_forget_bias", "ffn2_norm",
         "final_norm")
WEIGHTS = ("ffn1_norm", "ffn1_w_gate", "ffn1_w_up", "ffn1_w_down", "mix_norm", "w_in", "mla_q_norm", "mla_w_q_b",
           "mla_kv_norm", "mla_w_kv_b", "swa_sinks", "fox_forget_bias", "w_out", "ffn2_norm", "ffn2_w_gate",
           "ffn2_w_up", "ffn2_w_down", "final_norm")


def _params(**kw):
    return pltpu.CompilerParams(vmem_limit_bytes=VMEM_LIMIT_BYTES, **kw)


def _tile(n, want):
    if n <= want:
        return n
    t = (want // LANES) * LANES
    while n % t:
        t -= LANES
    return t


def _rows(n, want):
    if n <= want:
        return n
    t = (want // 8) * 8
    while n % t:
        t -= 8
    return t


def _ew(name, fn, ins, outs, tm=EW_ROWS):
    tok = None
    for a, kind in ins:
        if kind == "row":
            tok = a.shape[-2]
    tm = _rows(tok, tm)
    steps = tok // tm

    def spec(shape, kind):
        if kind == "row":
            ax = len(shape) - 2
            blk = tuple(tm if d == ax else s for d, s in enumerate(shape))
            return pl.BlockSpec(blk, lambda i, ax=ax, n=len(shape): tuple(i if d == ax else 0 for d in range(n)))
        return pl.BlockSpec(tuple(shape), lambda i, n=len(shape): (0,) * n)

    n_in = len(ins)
    kinds = [k for _, _, k in outs]

    def body(*refs):
        vals = fn(*[r[...] for r in refs[:n_in]])
        for r, v, kind in zip(refs[n_in:], vals, kinds):
            if kind == "row":
                r[...] = v.astype(r.dtype)
            else:
                @pl.when(pl.program_id(0) == 0)
                def _(r=r):
                    r[...] = jnp.zeros(r.shape, r.dtype)
                r[...] += v.astype(r.dtype)

    res = pl.pallas_call(
        body, name=name, grid=(steps,),
        in_specs=[spec(a.shape, k) for a, k in ins],
        out_specs=[spec(s, k) for s, _, k in outs],
        out_shape=[jax.ShapeDtypeStruct(tuple(s), d) for s, d, _ in outs],
        compiler_params=_params(dimension_semantics=("arbitrary",)),
    )(*[a for a, _ in ins])
    return res


def _rms_fwd(name, x, g):
    def fn(x, g):
        r = lax.rsqrt(jnp.mean(x * x, axis=-1, keepdims=True) + RMS_EPS)
        return [x * r * g]
    return _ew(name, fn, [(x, "row"), (g, "full")], [(x.shape, BF16, "row")])[0]


def _rms_bwd(name, dh, x, g, res=None, also_bf16=False):
    def fn(dh, x, g, *rest):
        dh = dh.astype(F32)
        r = lax.rsqrt(jnp.mean(x * x, axis=-1, keepdims=True) + RMS_EPS)
        xh = x * r
        dxh = dh * g
        dx = r * (dxh - xh * jnp.mean(dxh * xh, axis=-1, keepdims=True))
        if rest:
            dx = dx + rest[0]
        return [dx, jnp.sum(dh * xh, axis=0, keepdims=True)] + ([dx] if also_bf16 else [])
    ins = [(dh, "row"), (x, "row"), (g, "full")] + ([(res, "row")] if res is not None else [])
    outs = [(x.shape, F32, "row"), (g.shape, F32, "acc")] + ([(x.shape, BF16, "row")] if also_bf16 else [])
    return _ew(name, fn, ins, outs)


def _rope(name, x, xs, cos, sin, sign=1.0):
    def fn(x, xs, c, s):
        return [x * c + sign * (xs * s)]
    return _ew(name, fn, [(x, "row"), (xs, "row"), (cos, "row"), (sin, "row")], [(x.shape, F32, "row")])[0]


def _rope_tables(positions, inv_freq2):
    def fn(pos, f):
        ang = pos.astype(F32) * f
        return [jnp.cos(ang), jnp.sin(ang)]
    t = positions.shape[0]
    return _ew("rope_tables", fn, [(positions, "row"), (inv_freq2, "full")],
               [((t, 2 * 32), F32, "row"), ((t, 2 * 32), F32, "row")])


def _mm(name, lhs, rhs, terms, epi, out_dtypes, extras=(), ta=False, tb=False, tm=MM_TM, tn=MM_TN, tk=MM_TK):
    if ta:
        kdim, m = lhs[0].shape
    else:
        m, kdim = lhs[0].shape
    n = rhs[0].shape[0] if tb else rhs[0].shape[1]
    tm, tn, tk = _tile(m, tm), _tile(n, tn), _tile(kdim, tk)
    nk = kdim // tk
    n_acc = 1 + max(a for _, _, a in terms)
    nl, nr, ne = len(lhs), len(rhs), len(extras)
    dims = (((0 if ta else 1,), (1 if tb else 0,)), ((), ()))

    def body(*refs):
        l_refs, r_refs = refs[:nl], refs[nl:nl + nr]
        e_refs = refs[nl + nr:nl + nr + ne]
        o_refs = refs[nl + nr + ne:len(refs) - n_acc]
        accs = refs[len(refs) - n_acc:]
        k = pl.program_id(2)

        @pl.when(k == 0)
        def _():
            for acc in accs:
                acc[...] = jnp.zeros(acc.shape, F32)

        lv, rv = {}, {}
        for li, ri, ai in terms:
            if li not in lv:
                lv[li] = l_refs[li][...].astype(BF16)
            if ri not in rv:
                rv[ri] = r_refs[ri][...].astype(BF16)
            accs[ai][...] += lax.dot_general(lv[li], rv[ri], dims, preferred_element_type=F32)

        @pl.when(k == nk - 1)
        def _():
            outs = epi([acc[...] for acc in accs], [e[...] for e in e_refs])
            for o, v in zip(o_refs, outs):
                o[...] = v.astype(o.dtype)

    l_spec = pl.BlockSpec((tk, tm), lambda i, j, k: (k, i)) if ta else pl.BlockSpec((tm, tk), lambda i, j, k: (i, k))
    r_spec = pl.BlockSpec((tn, tk), lambda i, j, k: (j, k)) if tb else pl.BlockSpec((tk, tn), lambda i, j, k: (k, j))
    o_spec = pl.BlockSpec((tm, tn), lambda i, j, k: (i, j))
    return pl.pallas_call(
        body, name=name, grid=(m // tm, n // tn, nk),
        in_specs=[l_spec] * nl + [r_spec] * nr + [o_spec] * ne,
        out_specs=[o_spec] * len(out_dtypes),
        out_shape=[jax.ShapeDtypeStruct((m, n), d) for d in out_dtypes],
        scratch_shapes=[pltpu.VMEM((tm, tn), F32)] * n_acc,
        compiler_params=_params(dimension_semantics=("parallel", "parallel", "arbitrary")),
    )(*lhs, *rhs, *extras)


def _mm1(name, a, b, out_dtype=F32, scale=None, add=None, **kw):
    def epi(accs, ex):
        v = accs[0] if scale is None else accs[0] * scale
        return [v + ex[0] if ex else v]
    return _mm(name, [a], [b], [(0, 0, 0)], epi, [out_dtype], extras=[] if add is None else [add], **kw)[0]


def _scores(q, k, cq, ck):
    s = lax.dot_general(q, k, (((1,), (1,)), ((), ())), preferred_element_type=F32)
    if cq is not None:
        s = s + (cq - ck)
    return s


def _mask(shape, q0, k0, window):
    qpos = q0 + lax.broadcasted_iota(jnp.int32, shape, 0)
    kpos = k0 + lax.broadcasted_iota(jnp.int32, shape, 1)
    mask = kpos <= qpos
    if window:
        mask = mask & (kpos > qpos - WINDOW)
    return mask


def _flash_fwd(name, q, k, v, scale, group=1, cq=None, ck=None, sink=None, window=False, t=ATT_T):
    h_n, tok, dq = q.shape
    dv = v.shape[-1]
    t = WINDOW if window else min(t, tok)
    nq = tok // t
    nj = 2 if window else nq
    bias = cq is not None
    has_sink = sink is not None

    def kv_blk(qi, j):
        return jnp.maximum(qi - 1 + j, 0) if window else jnp.minimum(j, qi)

    def body(*refs):
        q_ref, k_ref, v_ref = refs[:3]
        pos = 3
        cq_ref = ck_ref = sink_ref = None
        if bias:
            cq_ref, ck_ref = refs[pos], refs[pos + 1]
            pos += 2
        if has_sink:
            sink_ref = refs[pos]
            pos += 1
        o_ref, lse_ref, m_s, l_s, acc_s, qs_s = refs[pos:]
        h, qi, j = pl.program_id(0), pl.program_id(1), pl.program_id(2)

        @pl.when(j == 0)
        def _():
            qs_s[...] = (q_ref[...].astype(F32) * (scale * LOG2E)).astype(BF16)
            m_s[...] = jnp.full(m_s.shape, sink_ref[h] * LOG2E if has_sink else NEG, F32)
            l_s[...] = jnp.full(l_s.shape, 1.0 if has_sink else 0.0, F32)
            acc_s[...] = jnp.zeros(acc_s.shape, F32)

        kb = qi - 1 + j if window else j

        def step(masked):
            s = _scores(qs_s[...], k_ref[...], cq_ref[...] if bias else None, ck_ref[...] if bias else None)
            if masked:
                s = jnp.where(_mask(s.shape, qi * t, kb * t, window), s, NEG)
            m_prev = m_s[...]
            m_new = jnp.maximum(m_prev, jnp.max(s, axis=1, keepdims=True))
            alpha = jnp.exp2(m_prev - m_new)
            p = jnp.exp2(s - m_new)
            l_s[...] = alpha * l_s[...] + jnp.sum(p, axis=1, keepdims=True)
            acc_s[...] = alpha * acc_s[...] + jnp.dot(p.astype(BF16), v_ref[...], preferred_element_type=F32)
            m_s[...] = m_new

        if window:
            pl.when(kb >= 0)(lambda: step(True))
        else:
            pl.when(j < qi)(lambda: step(False))
            pl.when(j == qi)(lambda: step(True))

        @pl.when(j == nj - 1)
        def _():
            o_ref[...] = (acc_s[...] / l_s[...]).astype(o_ref.dtype)
            lse_ref[...] = m_s[...] + jnp.log(l_s[...]) * LOG2E

    in_specs = [
        pl.BlockSpec((None, t, dq), lambda h, qi, j: (h, qi, 0)),
        pl.BlockSpec((None, t, dq), lambda h, qi, j: (h // group, kv_blk(qi, j), 0)),
        pl.BlockSpec((None, t, dv), lambda h, qi, j: (h // group, kv_blk(qi, j), 0)),
    ]
    args = [q, k, v]
    if bias:
        in_specs += [pl.BlockSpec((None, t, 1), lambda h, qi, j: (h, qi, 0)),
                     pl.BlockSpec((None, 1, t), lambda h, qi, j: (h, 0, kv_blk(qi, j)))]
        args += [cq, ck]
    if has_sink:
        in_specs.append(pl.BlockSpec(memory_space=pltpu.SMEM))
        args.append(sink)
    return pl.pallas_call(
        body, name=name, grid=(h_n, nq, nj),
        in_specs=in_specs,
        out_specs=[pl.BlockSpec((None, t, dv), lambda h, qi, j: (h, qi, 0)),
                   pl.BlockSpec((None, t, 1), lambda h, qi, j: (h, qi, 0))],
        out_shape=[jax.ShapeDtypeStruct((h_n, tok, dv), BF16), jax.ShapeDtypeStruct((h_n, tok, 1), F32)],
        scratch_shapes=[pltpu.VMEM((t, 1), F32), pltpu.VMEM((t, 1), F32), pltpu.VMEM((t, dv), F32),
                        pltpu.VMEM((t, dq), BF16)],
        compiler_params=_params(dimension_semantics=("parallel", "parallel", "arbitrary")),
    )(*args)


def _flash_bwd(name, q, k, v, do, lse, delta, scale, group=1, cq=None, ck=None, window=False, t=ATT_T):
    h_n, tok, dq = q.shape
    dv = v.shape[-1]
    t = WINDOW if window else min(t, tok)
    nq = tok // t
    nj = 2 if window else nq
    bias = cq is not None

    def q_blk(ki, j):
        return jnp.minimum(ki + j, nq - 1) if window else jnp.maximum(j, ki)

    def body(*refs):
        q_ref, k_ref, v_ref, do_ref, lse_ref, dl_ref = refs[:6]
        pos = 6
        cq_ref = ck_ref = None
        if bias:
            cq_ref, ck_ref = refs[pos], refs[pos + 1]
            pos += 2
        dq_ref, dk_ref, dv_ref = refs[pos:pos + 3]
        pos += 3
        dc_ref = dr_ref = None
        if bias:
            dc_ref, dr_ref = refs[pos], refs[pos + 1]
            pos += 2
        dk_s, dv_s = refs[pos], refs[pos + 1]
        dc_s = refs[pos + 2] if bias else None
        ki, j = pl.program_id(1), pl.program_id(2)

        @pl.when((ki == 0) & (j == 0))
        def _():
            dq_ref[...] = jnp.zeros(dq_ref.shape, F32)
            if bias:
                dr_ref[...] = jnp.zeros(dr_ref.shape, F32)

        @pl.when(j == 0)
        def _():
            dk_s[...] = jnp.zeros(dk_s.shape, F32)
            dv_s[...] = jnp.zeros(dv_s.shape, F32)
            if bias:
                dc_s[...] = jnp.zeros(dc_s.shape, F32)

        qb = ki + j if window else j

        def step(masked):
            qv, kv, dov = q_ref[...], k_ref[...], do_ref[...]
            qs = (qv.astype(F32) * (scale * LOG2E)).astype(BF16)
            s = _scores(qs, kv, cq_ref[...] if bias else None, ck_ref[...] if bias else None)
            p = jnp.exp2(s - lse_ref[...])
            if masked:
                p = jnp.where(_mask(s.shape, qb * t, ki * t, window), p, 0.0)
            pb = p.astype(BF16)
            dv_s[...] += lax.dot_general(pb, dov, (((0,), (0,)), ((), ())), preferred_element_type=F32)
            dp = lax.dot_general(dov, v_ref[...], (((1,), (1,)), ((), ())), preferred_element_type=F32)
            ds = p * (dp - dl_ref[...])
            dsb = ds.astype(BF16)
            rows = pl.ds(pl.multiple_of(qb * t, t), t)
            dq_ref[rows, :] += scale * jnp.dot(dsb, kv, preferred_element_type=F32)
            dk_s[...] += scale * lax.dot_general(dsb, qv, (((0,), (0,)), ((), ())), preferred_element_type=F32)
            if bias:
                dc_s[...] += jnp.sum(ds, axis=0, keepdims=True)
                dr_ref[rows, :] += jnp.sum(ds, axis=1, keepdims=True)

        if window:
            pl.when(qb < nq)(lambda: step(True))
        else:
            pl.when(j > ki)(lambda: step(False))
            pl.when(j == ki)(lambda: step(True))

        @pl.when(j == nj - 1)
        def _():
            dk_ref[...] = dk_s[...]
            dv_ref[...] = dv_s[...]
            if bias:
                dc_ref[...] = dc_s[...]

    def qmap(h, ki, j):
        return (h, q_blk(ki, j), 0)

    in_specs = [
        pl.BlockSpec((None, t, dq), qmap),
        pl.BlockSpec((None, t, dq), lambda h, ki, j: (h // group, ki, 0)),
        pl.BlockSpec((None, t, dv), lambda h, ki, j: (h // group, ki, 0)),
        pl.BlockSpec((None, t, dv), qmap),
        pl.BlockSpec((None, t, 1), qmap),
        pl.BlockSpec((None, t, 1), qmap),
    ]
    args = [q, k, v, do, lse, delta]
    out_specs = [pl.BlockSpec((None, tok, dq), lambda h, ki, j: (h, 0, 0)),
                 pl.BlockSpec((None, t, dq), lambda h, ki, j: (h, ki, 0)),
                 pl.BlockSpec((None, t, dv), lambda h, ki, j: (h, ki, 0))]
    out_shape = [jax.ShapeDtypeStruct((h_n, tok, dq), F32), jax.ShapeDtypeStruct((h_n, tok, dq), F32),
                 jax.ShapeDtypeStruct((h_n, tok, dv), F32)]
    scratch = [pltpu.VMEM((t, dq), F32), pltpu.VMEM((t, dv), F32)]
    if bias:
        in_specs += [pl.BlockSpec((None, t, 1), qmap), pl.BlockSpec((None, 1, t), lambda h, ki, j: (h, 0, ki))]
        args += [cq, ck]
        out_specs += [pl.BlockSpec((None, 1, t), lambda h, ki, j: (h, 0, ki)),
                      pl.BlockSpec((None, tok, 1), lambda h, ki, j: (h, 0, 0))]
        out_shape += [jax.ShapeDtypeStruct((h_n, 1, tok), F32), jax.ShapeDtypeStruct((h_n, tok, 1), F32)]
        scratch.append(pltpu.VMEM((1, t), F32))
    return pl.pallas_call(
        body, name=name, grid=(h_n, nq, nj),
        in_specs=in_specs, out_specs=out_specs, out_shape=out_shape, scratch_shapes=scratch,
        compiler_params=_params(dimension_semantics=("parallel", "arbitrary", "arbitrary")),
    )(*args)


def _delta(name, do, o):
    def fn(do, o):
        return [jnp.sum(do.astype(F32) * o.astype(F32), axis=-1, keepdims=True)]
    return _ew(name, fn, [(do, "row"), (o, "row")], [(do.shape[:2] + (1,), F32, "row")], tm=512)[0]


def _sink_grad(name, sink3, lse, delta):
    def fn(sk, lse, dl):
        return [-jnp.sum(jnp.exp2(sk * LOG2E - lse) * dl, axis=1, keepdims=True)]
    return _ew(name, fn, [(sink3, "full"), (lse, "row"), (delta, "row")], [(sink3.shape, F32, "acc")], tm=512)[0]


def _log_sigmoid(z):
    return jnp.minimum(z, 0.0) - jnp.log(1.0 + jnp.exp(-jnp.abs(z)))


def _gate_fwd(zt, bias):
    tok = zt.shape[1]

    def body(z_ref, b_ref, c_ref):
        x = _log_sigmoid(z_ref[...] + b_ref[...])
        lane = lax.broadcasted_iota(jnp.int32, x.shape, 1)
        k = 1
        while k < tok:
            x = x + jnp.where(lane >= k, pltpu.roll(x, k, axis=1), 0.0)
            k *= 2
        c_ref[...] = x * LOG2E

    return pl.pallas_call(body, name="fox_gate_fwd", out_shape=jax.ShapeDtypeStruct(zt.shape, F32),
                          compiler_params=_params())(zt, bias)


def _gate_bwd(d_rows, d_cols, zt, bias):
    tok = zt.shape[1]

    def body(dr_ref, dc_ref, z_ref, b_ref, dz_ref, db_ref):
        x = dr_ref[...] - dc_ref[...]
        lane = lax.broadcasted_iota(jnp.int32, x.shape, 1)
        k = 1
        while k < tok:
            x = x + jnp.where(lane < tok - k, pltpu.roll(x, tok - k, axis=1), 0.0)
            k *= 2
        dz = x / (1.0 + jnp.exp(z_ref[...] + b_ref[...]))
        dz_ref[...] = dz
        db_ref[...] = jnp.sum(dz, axis=1, keepdims=True)

    return pl.pallas_call(body, name="fox_gate_bwd",
                          out_shape=[jax.ShapeDtypeStruct(zt.shape, F32), jax.ShapeDtypeStruct(bias.shape, F32)],
                          compiler_params=_params())(d_rows, d_cols, zt, bias)


_NT = (((1,), (1,)), ((), ()))
_TN = (((0,), (0,)), ((), ()))


def _ffn_gate_up(name, h, wg, wu, l, tm=512):
    tok, d = h.shape
    nb, n = wg.shape[0], wg.shape[3]
    tm = min(tm, tok)

    def body(h_ref, wg_ref, wu_ref, u_ref, v_ref, a_ref):
        hv = h_ref[...]
        u = jnp.dot(hv, wg_ref[...], preferred_element_type=F32)
        v = jnp.dot(hv, wu_ref[...], preferred_element_type=F32)
        u_ref[...] = u.astype(BF16)
        v_ref[...] = v.astype(BF16)
        a_ref[...] = (u * jax.nn.sigmoid(u) * v).astype(BF16)

    w_spec = pl.BlockSpec((None, None, d, n), lambda j, i: (j, l, 0, 0))
    o_spec = pl.BlockSpec((None, tm, n), lambda j, i: (j, i, 0))
    return pl.pallas_call(
        body, name=name, grid=(nb, tok // tm),
        in_specs=[pl.BlockSpec((tm, d), lambda j, i: (i, 0)), w_spec, w_spec],
        out_specs=[o_spec] * 3, out_shape=[jax.ShapeDtypeStruct((nb, tok, n), BF16)] * 3,
        compiler_params=_params(dimension_semantics=("parallel", "parallel")),
    )(h, wg, wu)


def _ffn_down(name, a, wd, x, l, tm=1024, tn=1024):
    nb, tok, n = a.shape
    d = wd.shape[3]
    tm, tn = min(tm, tok), min(tn, d)

    def body(a_ref, wd_ref, x_ref, y_ref, acc):
        j = pl.program_id(2)

        @pl.when(j == 0)
        def _():
            acc[...] = jnp.zeros(acc.shape, F32)
        acc[...] += jnp.dot(a_ref[...], wd_ref[...], preferred_element_type=F32)

        @pl.when(j == nb - 1)
        def _():
            y_ref[...] = x_ref[...] + 0.5 * acc[...]

    return pl.pallas_call(
        body, name=name, grid=(tok // tm, d // tn, nb),
        in_specs=[pl.BlockSpec((None, tm, n), lambda i, c, j: (j, i, 0)),
                  pl.BlockSpec((None, None, n, tn), lambda i, c, j: (j, l, 0, c)),
                  pl.BlockSpec((tm, tn), lambda i, c, j: (i, c))],
        out_specs=pl.BlockSpec((tm, tn), lambda i, c, j: (i, c)),
        out_shape=jax.ShapeDtypeStruct((tok, d), F32),
        scratch_shapes=[pltpu.VMEM((tm, tn), F32)],
        compiler_params=_params(dimension_semantics=("parallel", "parallel", "arbitrary")),
    )(a, wd, x)


def _ffn_d_act(name, dy, u, v, wd, l, tm=512):
    nb, tok, n = u.shape
    d = dy.shape[1]
    tm = min(tm, tok)

    def body(dy_ref, u_ref, v_ref, wd_ref, du_ref, dv_ref):
        da = 0.5 * lax.dot_general(dy_ref[...], wd_ref[...], _NT, preferred_element_type=F32)
        uv, vv = u_ref[...].astype(F32), v_ref[...].astype(F32)
        sg = jax.nn.sigmoid(uv)
        du_ref[...] = (da * vv * (sg * (1.0 + uv * (1.0 - sg)))).astype(BF16)
        dv_ref[...] = (da * (uv * sg)).astype(BF16)

    t_spec = pl.BlockSpec((None, tm, n), lambda j, i: (j, i, 0))
    return pl.pallas_call(
        body, name=name, grid=(nb, tok // tm),
        in_specs=[pl.BlockSpec((tm, d), lambda j, i: (i, 0)), t_spec, t_spec,
                  pl.BlockSpec((None, None, n, d), lambda j, i: (j, l, 0, 0))],
        out_specs=[t_spec] * 2, out_shape=[jax.ShapeDtypeStruct((nb, tok, n), BF16)] * 2,
        compiler_params=_params(dimension_semantics=("parallel", "parallel")),
    )(dy, u, v, wd)


def _ffn_d_h(name, du, dv, wg, wu, l, tm=1024):
    nb, tok, n = du.shape
    d = wg.shape[2]
    tm = min(tm, tok)

    def body(du_ref, dv_ref, wg_ref, wu_ref, dh_ref, acc):
        j = pl.program_id(1)

        @pl.when(j == 0)
        def _():
            acc[...] = jnp.zeros(acc.shape, F32)
        acc[...] += (lax.dot_general(du_ref[...], wg_ref[...], _NT, preferred_element_type=F32)
                     + lax.dot_general(dv_ref[...], wu_ref[...], _NT, preferred_element_type=F32))

        @pl.when(j == nb - 1)
        def _():
            dh_ref[...] = acc[...].astype(BF16)

    t_spec = pl.BlockSpec((None, tm, n), lambda i, j: (j, i, 0))
    w_spec = pl.BlockSpec((None, None, d, n), lambda i, j: (j, l, 0, 0))
    return pl.pallas_call(
        body, name=name, grid=(tok // tm, nb),
        in_specs=[t_spec, t_spec, w_spec, w_spec],
        out_specs=pl.BlockSpec((tm, d), lambda i, j: (i, 0)),
        out_shape=jax.ShapeDtypeStruct((tok, d), BF16),
        scratch_shapes=[pltpu.VMEM((tm, d), F32)],
        compiler_params=_params(dimension_semantics=("parallel", "arbitrary")),
    )(du, dv, wg, wu)


def _ffn_wgrad_in(name, h, du, dv, l, like, prev=None, tk=512, td=1024):
    tok, d = h.shape
    nb, _, n = du.shape
    tk, td = min(tk, tok), min(td, d)
    nk = tok // tk
    n_in = 3

    def body(*refs):
        h_ref, du_ref, dv_ref = refs[:3]
        og_ref, ou_ref, accg, accu = refs[len(refs) - 4:]
        k = pl.program_id(2)

        @pl.when(k == 0)
        def _():
            accg[...] = jnp.zeros(accg.shape, F32)
            accu[...] = jnp.zeros(accu.shape, F32)
        hv = h_ref[...]
        accg[...] += lax.dot_general(hv, du_ref[...], _TN, preferred_element_type=F32)
        accu[...] += lax.dot_general(hv, dv_ref[...], _TN, preferred_element_type=F32)

        @pl.when(k == nk - 1)
        def _():
            og_ref[...] = accg[...].astype(BF16)
            ou_ref[...] = accu[...].astype(BF16)

    t_spec = pl.BlockSpec((None, tk, n), lambda j, c, k: (j, k, 0))
    o_spec = pl.BlockSpec((None, None, td, n), lambda j, c, k: (j, l, c, 0))
    in_specs = [pl.BlockSpec((tk, td), lambda j, c, k: (k, c)), t_spec, t_spec]
    args = [h, du, dv]
    aliases = {}
    if prev is not None:
        in_specs += [pl.BlockSpec(memory_space=pl.ANY)] * 2
        args += list(prev)
        aliases = {n_in: 0, n_in + 1: 1}
    return pl.pallas_call(
        body, name=name, grid=(nb, d // td, nk),
        in_specs=in_specs, out_specs=[o_spec] * 2,
        out_shape=[jax.ShapeDtypeStruct(like.shape, BF16)] * 2,
        scratch_shapes=[pltpu.VMEM((td, n), F32)] * 2,
        input_output_aliases=aliases,
        compiler_params=_params(dimension_semantics=("parallel", "parallel", "arbitrary")),
    )(*args)


def _ffn_wgrad_out(name, a, dy, l, like, prev=None, tk=512):
    nb, tok, n = a.shape
    d = dy.shape[1]
    tk = min(tk, tok)
    nk = tok // tk

    def body(*refs):
        a_ref, dy_ref = refs[:2]
        o_ref, acc = refs[len(refs) - 2:]
        k = pl.program_id(1)

        @pl.when(k == 0)
        def _():
            acc[...] = jnp.zeros(acc.shape, F32)
        acc[...] += lax.dot_general(a_ref[...], dy_ref[...], _TN, preferred_element_type=F32)

        @pl.when(k == nk - 1)
        def _():
            o_ref[...] = (0.5 * acc[...]).astype(BF16)

    in_specs = [pl.BlockSpec((None, tk, n), lambda j, k: (j, k, 0)), pl.BlockSpec((tk, d), lambda j, k: (k, 0))]
    args = [a, dy]
    aliases = {}
    if prev is not None:
        in_specs.append(pl.BlockSpec(memory_space=pl.ANY))
        args.append(prev)
        aliases = {2: 0}
    return pl.pallas_call(
        body, name=name, grid=(nb, nk),
        in_specs=in_specs, out_specs=pl.BlockSpec((None, None, n, d), lambda j, k: (j, l, 0, 0)),
        out_shape=jax.ShapeDtypeStruct(like.shape, BF16),
        scratch_shapes=[pltpu.VMEM((n, d), F32)],
        input_output_aliases=aliases,
        compiler_params=_params(dimension_semantics=("parallel", "arbitrary")),
    )(*args)


def _ffn_fwd(tag, x, g, wg, wu, wd, l):
    h = _rms_fwd(tag + "_norm", x, g)
    u, v, a = _ffn_gate_up(tag + "_gate_up", h, wg, wu, l)
    y = _ffn_down(tag + "_down", a, wd, x, l)
    return y, (x, h, u, v, a)


def _ffn_bwd(tag, dy, dyb, saved, g, wg, wu, wd, l, prev):
    x, h, u, v, a = saved
    du, dv = _ffn_d_act(tag + "_d_act", dyb, u, v, wd, l)
    d_wd = _ffn_wgrad_out(tag + "_d_wd", a, dyb, l, wd, None if prev is None else prev[2])
    d_wg, d_wu = _ffn_wgrad_in(tag + "_d_wgu", h, du, dv, l, wg, None if prev is None else prev[:2])
    dh = _ffn_d_h(tag + "_d_h", du, dv, wg, wu, l)
    dx, dg, dxb = _rms_bwd(tag + "_d_norm", dh, x, g, res=dy, also_bf16=True)
    return dx, dxb, dg, (d_wg, d_wu, d_wd)


def _swap_halves(x):
    t, w = x.shape
    return x.reshape(t, w // HEAD_DIM, 2, HEAD_DIM // 2)[:, :, ::-1, :].reshape(t, w)


def _heads(x, n):
    t, w = x.shape
    return x.reshape(t, n, w // n).transpose(1, 0, 2).astype(BF16)


def _unheads(x):
    n, t, d = x.shape
    return x.transpose(1, 0, 2).reshape(t, n * d)


def _mixer_fwd(tag, x, w, cos, sin):
    tok = x.shape[0]
    h2 = _rms_fwd(tag + "_norm", x, w["mix_norm"])
    p = _mm1(tag + "_in", h2, w["w_in"], tn=640)
    c_q, c_kv = p[:, :512], p[:, 512:768]
    q_s, k_s, v_s = p[:, 768:1280], p[:, 1280:1408], p[:, 1408:1536]
    q_f, k_f, v_f = p[:, 1536:2048], p[:, 2048:2560], p[:, 2560:3072]
    k_rope, f_logit = p[:, 3072:3136], p[:, 3136:3144]

    qn = _rms_fwd(tag + "_q_norm", c_q, w["mla_q_norm"])
    qm = _mm1(tag + "_q_b", qn, w["mla_w_q_b"])
    kvn = _rms_fwd(tag + "_kv_norm", c_kv, w["mla_kv_norm"])
    kvm = _mm1(tag + "_kv_b", kvn, w["mla_w_kv_b"])

    rin = jnp.concatenate([qm[:, 1024:], q_s, k_s, k_rope, jnp.zeros((tok, 64), F32)], axis=1)
    rout = _rope(tag + "_rope", rin, _swap_halves(rin), cos, sin)
    q_pe, q_sr, k_sr, k_pe = rout[:, :512], rout[:, 512:1024], rout[:, 1024:1152], rout[:, 1152:1216]

    q_m = jnp.concatenate([qm[:, :1024].reshape(tok, HEADS, MLA_NOPE), q_pe.reshape(tok, HEADS, MLA_ROPE)], axis=-1)
    q_m = q_m.transpose(1, 0, 2).astype(BF16)
    k_m = jnp.concatenate([kvm[:, :1024].reshape(tok, HEADS, MLA_NOPE),
                           jnp.broadcast_to(k_pe[:, None, :], (tok, HEADS, MLA_ROPE))], axis=-1)
    k_m = k_m.transpose(1, 0, 2).astype(BF16)
    v_m = _heads(kvm[:, 1024:], HEADS)
    o_mla, lse_mla = _flash_fwd(tag + "_mla_fwd", q_m, k_m, v_m, (MLA_NOPE + MLA_ROPE) ** -0.5)

    q_sh, k_sh, v_sh = _heads(q_sr, HEADS), _heads(k_sr, SWA_KV_HEADS), _heads(v_s, SWA_KV_HEADS)
    o_swa, lse_swa = _flash_fwd(tag + "_swa_fwd", q_sh, k_sh, v_sh, HEAD_DIM ** -0.5,
                                group=HEADS // SWA_KV_HEADS, sink=w["swa_sinks"], window=True)

    zt = f_logit.T
    c = _gate_fwd(zt, w["fox_forget_bias"].reshape(HEADS, 1))
    cq, ck = c[:, :, None], c[:, None, :]
    q_fh, k_fh, v_fh = _heads(q_f, HEADS), _heads(k_f, HEADS), _heads(v_f, HEADS)
    o_fox, lse_fox = _flash_fwd(tag + "_fox_fwd", q_fh, k_fh, v_fh, HEAD_DIM ** -0.5, cq=cq, ck=ck)

    mixed = jnp.concatenate([_unheads(o_mla), _unheads(o_swa), _unheads(o_fox)], axis=1)
    y = _mm1(tag + "_out", mixed, w["w_out"], add=x)
    saved = dict(x=x, h2=h2, c_q=c_q, c_kv=c_kv, qn=qn, kvn=kvn, zt=zt, cq=cq, ck=ck, mixed=mixed,
                 mla=(q_m, k_m, v_m, o_mla, lse_mla), swa=(q_sh, k_sh, v_sh, o_swa, lse_swa),
                 fox=(q_fh, k_fh, v_fh, o_fox, lse_fox))
    return y, saved


def _mixer_bwd(tag, dy, s, w, cos, sin):
    tok = dy.shape[0]
    g = {}
    dmixed = _mm1(tag + "_d_mixed", dy, w["w_out"], out_dtype=BF16, tb=True)
    g["w_out"] = _mm1(tag + "_d_wout", s["mixed"], dy, ta=True)
    do_mla = _heads(dmixed[:, :1024], HEADS)
    do_swa = _heads(dmixed[:, 1024:1536], HEADS)
    do_fox = _heads(dmixed[:, 1536:], HEADS)

    q_m, k_m, v_m, o_mla, lse_mla = s["mla"]
    dl = _delta(tag + "_mla_delta", do_mla, o_mla)
    dq_m, dk_m, dv_m = _flash_bwd(tag + "_mla_bwd", q_m, k_m, v_m, do_mla, lse_mla, dl,
                                  (MLA_NOPE + MLA_ROPE) ** -0.5)

    q_sh, k_sh, v_sh, o_swa, lse_swa = s["swa"]
    dl = _delta(tag + "_swa_delta", do_swa, o_swa)
    g["swa_sinks"] = _sink_grad(tag + "_d_sink", w["swa_sinks"].reshape(HEADS, 1, 1), lse_swa, dl).reshape(HEADS)
    dq_sh, dk_sh, dv_sh = _flash_bwd(tag + "_swa_bwd", q_sh, k_sh, v_sh, do_swa, lse_swa, dl, HEAD_DIM ** -0.5,
                                     group=HEADS // SWA_KV_HEADS, window=True)

    q_fh, k_fh, v_fh, o_fox, lse_fox = s["fox"]
    dl = _delta(tag + "_fox_delta", do_fox, o_fox)
    dq_fh, dk_fh, dv_fh, d_cols, d_rows = _flash_bwd(tag + "_fox_bwd", q_fh, k_fh, v_fh, do_fox, lse_fox, dl,
                                                     HEAD_DIM ** -0.5, cq=s["cq"], ck=s["ck"])
    dzt, dbias = _gate_bwd(d_rows[:, :, 0], d_cols[:, 0, :], s["zt"], w["fox_forget_bias"].reshape(HEADS, 1))
    g["fox_forget_bias"] = dbias.reshape(HEADS)

    grp = HEADS // SWA_KV_HEADS
    dq_mt = dq_m.transpose(1, 0, 2)
    dk_mt = dk_m.transpose(1, 0, 2)
    d_qpe = dq_mt[:, :, MLA_NOPE:].reshape(tok, HEADS * MLA_ROPE)
    d_kpe_heads = dk_mt[:, :, MLA_NOPE:].reshape(tok, HEADS * MLA_ROPE)
    d_qs = _unheads(dq_sh)
    d_ks_heads = dk_sh.reshape(SWA_KV_HEADS, grp, tok, HEAD_DIM).transpose(2, 1, 0, 3).reshape(tok, grp * 128)
    d_vs_heads = dv_sh.reshape(SWA_KV_HEADS, grp, tok, HEAD_DIM).transpose(2, 1, 0, 3).reshape(tok, grp * 128)

    def fold(d_kpe_h, d_ks_h, d_vs_h):
        kpe = d_kpe_h[:, 0:64]
        for i in range(1, HEADS):
            kpe = kpe + d_kpe_h[:, 64 * i:64 * (i + 1)]
        ks, vs = d_ks_h[:, 0:128], d_vs_h[:, 0:128]
        for i in range(1, grp):
            ks = ks + d_ks_h[:, 128 * i:128 * (i + 1)]
            vs = vs + d_vs_h[:, 128 * i:128 * (i + 1)]
        return [jnp.concatenate([kpe, jnp.zeros_like(kpe)], axis=1), ks, vs]
    d_kpe2, d_ksr, d_vs = _ew(tag + "_fold_heads", fold,
                              [(d_kpe_heads, "row"), (d_ks_heads, "row"), (d_vs_heads, "row")],
                              [((tok, 128), F32, "row"), ((tok, 128), F32, "row"), ((tok, 128), F32, "row")])

    rin = jnp.concatenate([d_qpe, d_qs, d_ksr, d_kpe2], axis=1)
    rout = _rope(tag + "_d_rope", rin, _swap_halves(rin), cos, sin, sign=-1.0)
    d_qpe_pre, d_qs_pre, d_ks_pre, d_krope = rout[:, :512], rout[:, 512:1024], rout[:, 1024:1152], rout[:, 1152:1216]

    d_qm = jnp.concatenate([dq_mt[:, :, :MLA_NOPE].reshape(tok, HEADS * MLA_NOPE), d_qpe_pre], axis=1)
    d_kvm = jnp.concatenate([dk_mt[:, :, :MLA_NOPE].reshape(tok, HEADS * MLA_NOPE), _unheads(dv_m)], axis=1)
    g["mla_w_q_b"] = _mm1(tag + "_d_wqb", s["qn"], d_qm, ta=True)
    d_qn = _mm1(tag + "_d_qn", d_qm, w["mla_w_q_b"], tb=True)
    d_cq, g["mla_q_norm"] = _rms_bwd(tag + "_d_q_norm", d_qn, s["c_q"], w["mla_q_norm"])
    g["mla_w_kv_b"] = _mm1(tag + "_d_wkvb", s["kvn"], d_kvm, ta=True)
    d_kvn = _mm1(tag + "_d_kvn", d_kvm, w["mla_w_kv_b"], tb=True)
    d_ckv, g["mla_kv_norm"] = _rms_bwd(tag + "_d_kv_norm", d_kvn, s["c_kv"], w["mla_kv_norm"])

    dp = jnp.concatenate([d_cq, d_ckv, d_qs_pre, d_ks_pre, d_vs, _unheads(dq_fh), _unheads(dk_fh), _unheads(dv_fh),
                          d_krope, dzt.T, jnp.zeros((tok, IN_COLS_PAD - IN_COLS), F32)], axis=1).astype(BF16)
    g["w_in"] = _mm1(tag + "_d_win", s["h2"], dp, ta=True, tn=640)
    dh2 = _mm1(tag + "_d_h2", dp, w["w_in"], tb=True, tk=640)
    dx, g["mix_norm"], dxb = _rms_bwd(tag + "_d_norm", dh2, s["x"], w["mix_norm"], res=dy, also_bf16=True)
    return dx, dxb, g


def _loss_head(x, g, target):
    d = x.shape[1]

    def fn(x, g, tgt):
        r = lax.rsqrt(jnp.mean(x * x, axis=-1, keepdims=True) + RMS_EPS)
        xh = x * r
        err = xh * g - tgt
        loss = 0.5 * jnp.sum(jnp.sum(err * err, axis=-1, keepdims=True), axis=0, keepdims=True) / d
        dy = err / d
        dxh = dy * g
        dx = r * (dxh - xh * jnp.mean(dxh * xh, axis=-1, keepdims=True))
        return [dx, jnp.sum(dy * xh, axis=0, keepdims=True), loss, dx]
    return _ew("loss_head", fn, [(x, "row"), (g, "full"), (target, "row")],
               [(x.shape, F32, "row"), (g.shape, F32, "acc"), ((1, 1), F32, "acc"), (x.shape, BF16, "row")])


def _adam_update(w, g, m, v):
    m = ADAM_B1 * m + (1.0 - ADAM_B1) * g
    v = ADAM_B2 * v + (1.0 - ADAM_B2) * (g * g)
    m_hat = m / (1.0 - ADAM_B1 ** ADAM_STEP)
    v_hat = v / (1.0 - ADAM_B2 ** ADAM_STEP)
    return [-ADAM_LR * (m_hat / (jnp.sqrt(v_hat) + ADAM_EPS) + ADAM_WD * w), m, v]


def _adamw(name, w, g, m, v):
    return _ew(name, _adam_update, [(w, "row"), (g, "row"), (m, "row"), (v, "row")], [(w.shape, F32, "row")] * 3)


def _adamw_slots(name, r, w, m, v):
    def fn(r, w, m, v):
        g = r[0].astype(F32)
        for i in range(1, N_DEV):
            g = g + r[i].astype(F32)
        return [g] + _adam_update(w, g, m, v)
    return _ew(name, fn, [(r, "row"), (w, "row"), (m, "row"), (v, "row")], [(w.shape, F32, "row")] * 4, tm=128)


def _coords(dev):
    return (dev // 4, (dev // 2) % 2, dev % 2)


def _all_gather(shards):
    n = len(shards)

    def body(*refs):
        x_refs, out_refs = refs[:n], refs[n:2 * n]
        send_sems, recv_sems, local_sems = refs[2 * n:]
        x, y, c = lax.axis_index("x"), lax.axis_index("y"), lax.axis_index("c")
        me, sibling = (x, y, c), (x, y, 1 - c)
        chips = [(1 - x, y), (x, 1 - y), (1 - x, 1 - y)]

        def slot(a, px, py, pc):
            return out_refs[a].at[4 * px + 2 * py + pc]

        def copy(a, k, block, to, src=None):
            return pltpu.make_async_remote_copy(
                src_ref=slot(a, *block) if src is None else src, dst_ref=slot(a, *block),
                send_sem=send_sems.at[a, k], recv_sem=recv_sems.at[a, k],
                device_id=to, device_id_type=pl.DeviceIdType.MESH)

        mine = [pltpu.make_async_copy(x_refs[a], slot(a, *me), local_sems.at[a]) for a in range(n)]
        first, passed = [], []
        for a in range(n):
            mine[a].start()
            first.append(copy(a, 0, me, sibling, src=x_refs[a]))
            first += [copy(a, 1 + j, me, (*chip, c), src=x_refs[a]) for j, chip in enumerate(chips)]
        for cp in first:
            cp.start()
        for a in range(n):
            for j, chip in enumerate(chips):
                copy(a, 1 + j, (*chip, c), me).wait_recv()
                passed.append(copy(a, 4 + j, (*chip, c), sibling))
                passed[-1].start()
        for a in range(n):
            copy(a, 0, sibling, me).wait_recv()
            for j, chip in enumerate(chips):
                copy(a, 4 + j, (*chip, 1 - c), me).wait_recv()
        for cp in first + passed:
            cp.wait_send()
        for a in range(n):
            mine[a].wait()

    return pl.pallas_call(
        body, name="all_gather_weights",
        out_shape=[jax.ShapeDtypeStruct((N_DEV,) + s.shape, s.dtype) for s in shards],
        in_specs=[pl.BlockSpec(memory_space=pl.ANY)] * n,
        out_specs=[pl.BlockSpec(memory_space=pl.ANY)] * n,
        scratch_shapes=[pltpu.SemaphoreType.DMA((n, 7)), pltpu.SemaphoreType.DMA((n, 7)),
                        pltpu.SemaphoreType.DMA((n,))],
    )(*shards)


def _all_to_all(blocks):
    n = len(blocks)

    def body(*refs):
        g_refs, out_refs = refs[:n], refs[n:2 * n]
        send_sems, recv_sems, local_sems = refs[2 * n:]
        me = 4 * lax.axis_index("x") + 2 * lax.axis_index("y") + lax.axis_index("c")

        def copy(a, peer):
            return pltpu.make_async_remote_copy(
                src_ref=g_refs[a].at[peer], dst_ref=out_refs[a].at[me],
                send_sem=send_sems.at[a, peer], recv_sem=recv_sems.at[a, me],
                device_id=_coords(peer), device_id_type=pl.DeviceIdType.MESH)

        def arrival(a, peer):
            return pltpu.make_async_remote_copy(
                src_ref=g_refs[a].at[peer], dst_ref=out_refs[a].at[peer],
                send_sem=send_sems.at[a, peer], recv_sem=recv_sems.at[a, peer],
                device_id=_coords(peer), device_id_type=pl.DeviceIdType.MESH)

        mine = [pltpu.make_async_copy(g_refs[a].at[me], out_refs[a].at[me], local_sems.at[a]) for a in range(n)]
        for a in range(n):
            mine[a].start()
        for step in ("start", "wait_recv", "wait_send"):
            for peer in range(N_DEV):
                @pl.when(peer != me)
                def _(peer=peer, step=step):
                    for a in range(n):
                        if step == "start":
                            copy(a, peer).start()
                        elif step == "wait_recv":
                            arrival(a, peer).wait_recv()
                        else:
                            copy(a, peer).wait_send()
        for a in range(n):
            mine[a].wait()

    return pl.pallas_call(
        body, name="all_to_all_grads",
        out_shape=[jax.ShapeDtypeStruct(b.shape, b.dtype) for b in blocks],
        in_specs=[pl.BlockSpec(memory_space=pl.ANY)] * n,
        out_specs=[pl.BlockSpec(memory_space=pl.ANY)] * n,
        scratch_shapes=[pltpu.SemaphoreType.DMA((n, N_DEV)), pltpu.SemaphoreType.DMA((n, N_DEV)),
                        pltpu.SemaphoreType.DMA((n,))],
    )(*blocks)


def _all_reduce_small(buf):
    def body(x_ref, out_ref, slots, send_sems, recv_sems):
        me = 4 * lax.axis_index("x") + 2 * lax.axis_index("y") + lax.axis_index("c")

        def copy(peer):
            return pltpu.make_async_remote_copy(
                src_ref=x_ref, dst_ref=slots.at[me],
                send_sem=send_sems.at[peer], recv_sem=recv_sems.at[me],
                device_id=_coords(peer), device_id_type=pl.DeviceIdType.MESH)

        def arrival(peer):
            return pltpu.make_async_remote_copy(
                src_ref=x_ref, dst_ref=slots.at[peer],
                send_sem=send_sems.at[peer], recv_sem=recv_sems.at[peer],
                device_id=_coords(peer), device_id_type=pl.DeviceIdType.MESH)

        slots[pl.ds(me, 1)] = x_ref[...][None]
        for peer in range(N_DEV):
            @pl.when(peer != me)
            def _(peer=peer):
                copy(peer).start()
        for peer in range(N_DEV):
            @pl.when(peer != me)
            def _(peer=peer):
                arrival(peer).wait_recv()
        for peer in range(N_DEV):
            @pl.when(peer != me)
            def _(peer=peer):
                copy(peer).wait_send()
        acc = slots[0]
        for peer in range(1, N_DEV):
            acc = acc + slots[peer]
        out_ref[...] = acc

    return pl.pallas_call(
        body, name="all_reduce_small",
        out_shape=jax.ShapeDtypeStruct(buf.shape, F32),
        in_specs=[pl.BlockSpec(memory_space=pltpu.VMEM)],
        out_specs=pl.BlockSpec(memory_space=pltpu.VMEM),
        scratch_shapes=[pltpu.VMEM((N_DEV,) + buf.shape, F32), pltpu.SemaphoreType.DMA((N_DEV,)),
                        pltpu.SemaphoreType.DMA((N_DEV,))],
        compiler_params=_params(),
    )(buf)


def _in_to_kernel(w):
    pad = jnp.zeros(w.shape[:-1] + (IN_COLS_PAD - IN_COLS,), w.dtype)
    return jnp.concatenate([w[..., :768], w[..., 832:3136], w[..., 768:832], w[..., 3136:], pad], axis=-1)


def _in_from_kernel(w):
    return jnp.concatenate([w[..., :768], w[..., 3072:3136], w[..., 768:3072], w[..., 3136:3144]], axis=-1)


def _split_to_kernel(w, a, b):
    r = w.shape[0]
    w3 = w.reshape(r, HEADS, a + b)
    return jnp.concatenate([w3[:, :, :a].reshape(r, HEADS * a), w3[:, :, a:].reshape(r, HEADS * b)], axis=1)


def _split_from_kernel(w, a, b):
    r = w.shape[0]
    return jnp.concatenate([w[:, :HEADS * a].reshape(r, HEADS, a), w[:, HEADS * a:].reshape(r, HEADS, b)],
                           axis=-1).reshape(r, HEADS * (a + b))


MIXER_BIG = ("w_in", "mla_w_q_b", "mla_w_kv_b", "w_out")


def _col_blocks_to_full(blk):
    nb, k, n = blk.shape
    return blk.transpose(1, 0, 2).reshape(k, nb * n)


def _full_to_col_blocks(g):
    k, c = g.shape
    return g.reshape(k, N_DEV, c // N_DEV).transpose(1, 0, 2).astype(BF16)


def _local_step(x, positions, target, gathered, small):
    inv_freq = ROPE_THETA ** (-jnp.arange(0, HEAD_DIM, 2, dtype=F32) / HEAD_DIM)
    cos64, sin64 = _rope_tables(positions, jnp.concatenate([-inv_freq, inv_freq])[None, :])
    cos = jnp.tile(cos64, (1, 20))
    sin = jnp.tile(sin64, (1, 20))

    layers = []
    for l in range(DEPTH):
        w = {}
        w["w_in"] = _in_to_kernel(_col_blocks_to_full(gathered["w_in"][:, l]))
        w["mla_w_q_b"] = _split_to_kernel(_col_blocks_to_full(gathered["mla_w_q_b"][:, l]), MLA_NOPE, MLA_ROPE)
        w["mla_w_kv_b"] = _split_to_kernel(_col_blocks_to_full(gathered["mla_w_kv_b"][:, l]), MLA_NOPE, MLA_V)
        blk = gathered["w_out"][:, l]
        w["w_out"] = blk.reshape(blk.shape[0] * blk.shape[1], blk.shape[2])
        for k in SMALL:
            if k != "final_norm":
                w[k] = small[k][l][None, :] if "norm" in k else small[k][l]
        layers.append(w)

    def ffn_w(f):
        return gathered[f + "_w_gate"], gathered[f + "_w_up"], gathered[f + "_w_down"]

    saved = []
    for l, w in enumerate(layers):
        t = "l%d" % l
        x, s1 = _ffn_fwd(t + "_ffn1", x, w["ffn1_norm"], *ffn_w("ffn1"), l)
        x, s2 = _mixer_fwd(t + "_mix", x, w, cos, sin)
        x, s3 = _ffn_fwd(t + "_ffn2", x, w["ffn2_norm"], *ffn_w("ffn2"), l)
        saved.append((s1, s2, s3))

    dx, d_final, loss, dxb = _loss_head(x, small["final_norm"][None, :], target)

    mixer_g = [None] * DEPTH
    small_g = [None] * DEPTH
    ffn_g = {"ffn1": None, "ffn2": None}
    for l in reversed(range(DEPTH)):
        w, (s1, s2, s3) = layers[l], saved[l]
        t = "l%d" % l
        sg = {}
        dx, dxb, sg["ffn2_norm"], ffn_g["ffn2"] = _ffn_bwd(t + "_ffn2", dx, dxb, s3, w["ffn2_norm"], *ffn_w("ffn2"), l,
                                                           ffn_g["ffn2"])
        dx, dxb, g = _mixer_bwd(t + "_mix", dx, s2, w, cos, sin)
        dx, dxb, sg["ffn1_norm"], ffn_g["ffn1"] = _ffn_bwd(t + "_ffn1", dx, dxb, s1, w["ffn1_norm"], *ffn_w("ffn1"), l,
                                                           ffn_g["ffn1"])
        mixer_g[l] = {
            "w_in": _full_to_col_blocks(_in_from_kernel(g["w_in"])),
            "mla_w_q_b": _full_to_col_blocks(_split_from_kernel(g["mla_w_q_b"], MLA_NOPE, MLA_ROPE)),
            "mla_w_kv_b": _full_to_col_blocks(_split_from_kernel(g["mla_w_kv_b"], MLA_NOPE, MLA_V)),
            "w_out": g["w_out"].astype(BF16).reshape(N_DEV, -1, g["w_out"].shape[1]),
        }
        for k in SMALL:
            if k in g:
                sg[k] = g[k]
        small_g[l] = {k: v.reshape(-1) for k, v in sg.items()}

    big = {k: jnp.stack([mixer_g[l][k] for l in range(DEPTH)], axis=1) for k in MIXER_BIG}
    for f in ("ffn1", "ffn2"):
        big[f + "_w_gate"], big[f + "_w_up"], big[f + "_w_down"] = ffn_g[f]
    sm = {k: jnp.stack([small_g[l][k] for l in range(DEPTH)]) for k in small_g[0]}
    sm["final_norm"] = d_final.reshape(-1)
    return loss, dx, big, sm


def _flat_rows(a):
    return a.reshape(-1, LANES)


def kernel(x, positions, ffn1_norm, ffn1_w_gate, ffn1_w_up, ffn1_w_down, mix_norm, w_in, mla_q_norm, mla_w_q_b, mla_kv_norm, mla_w_kv_b, swa_sinks, fox_forget_bias, w_out, ffn2_norm, ffn2_w_gate, ffn2_w_up, ffn2_w_down, final_norm, loss_target, m_ffn1_norm, m_ffn1_w_gate, m_ffn1_w_up, m_ffn1_w_down, m_mix_norm, m_w_in, m_mla_q_norm, m_mla_w_q_b, m_mla_kv_norm, m_mla_w_kv_b, m_swa_sinks, m_fox_forget_bias, m_w_out, m_ffn2_norm, m_ffn2_w_gate, m_ffn2_w_up, m_ffn2_w_down, m_final_norm, v_ffn1_norm, v_ffn1_w_gate, v_ffn1_w_up, v_ffn1_w_down, v_mix_norm, v_w_in, v_mla_q_norm, v_mla_w_q_b, v_mla_kv_norm, v_mla_w_kv_b, v_swa_sinks, v_fox_forget_bias, v_w_out, v_ffn2_norm, v_ffn2_w_gate, v_ffn2_w_up, v_ffn2_w_down, v_final_norm):
    given = dict(locals())
    weights = {k: given[k] for k in WEIGHTS}
    mom_m = {k: given["m_" + k] for k in WEIGHTS}
    mom_v = {k: given["v_" + k] for k in WEIGHTS}

    gathered = dict(zip(BIG, _all_gather([weights[k].astype(BF16) for k in BIG])))

    small = {k: weights[k] for k in SMALL}
    loss, grad_x, big_g, small_g = _local_step(x[0], positions[0][:, None], loss_target[0], gathered, small)

    received = dict(zip(BIG, _all_to_all([big_g[k] for k in BIG])))

    small_sizes = [weights[k].size for k in SMALL]
    sbuf = jnp.concatenate([small_g[k].reshape(-1) for k in SMALL] + [loss.reshape(-1)])
    pad = (-sbuf.size) % (8 * LANES)
    sbuf = jnp.concatenate([sbuf, jnp.zeros((pad,), F32)])
    stot = _all_reduce_small(_flat_rows(sbuf)).reshape(-1)

    grad_w, delta, new_m, new_v = {}, {}, {}, {}
    for k in BIG:
        shape = weights[k].shape
        two_d = (-1, shape[-1])
        g, d, m, v = _adamw_slots("adamw_" + k, received[k].reshape((N_DEV,) + (shape[0] * shape[1], shape[2])),
                                  weights[k].reshape(two_d), mom_m[k].reshape(two_d), mom_v[k].reshape(two_d))
        grad_w[k], delta[k], new_m[k], new_v[k] = (t.reshape(shape) for t in (g, d, m, v))
    off = 0
    for k, n in zip(SMALL, small_sizes):
        shape = weights[k].shape
        two_d = (-1, shape[-1]) if len(shape) > 1 else (1, -1)
        grad_w[k] = stot[off:off + n].reshape(shape)
        off += n
        d, m, v = _adamw("adamw_" + k, weights[k].reshape(two_d), grad_w[k].reshape(two_d),
                         mom_m[k].reshape(two_d), mom_v[k].reshape(two_d))
        delta[k], new_m[k], new_v[k] = d.reshape(shape), m.reshape(shape), v.reshape(shape)
    loss_total = stot[off]

    return (loss_total, grad_x[None], *[grad_w[k] for k in WEIGHTS], *[delta[k] for k in WEIGHTS],
            *[new_m[k] for k in WEIGHTS], *[new_v[k] for k in WEIGHTS])
```

```python
import functools

import jax
import jax.numpy as jnp
from jax import lax
from jax.experimental import pallas as pl
from jax.experimental.pallas import tpu as pltpu

F32 = jnp.float32
BF16 = jnp.bfloat16

N_DEV = 8
DEPTH = 2
RMS_EPS = 1e-6
ROPE_THETA = 10000.0
HEADS = 8
MLA_Q_LORA = 512
MLA_KV_LORA = 256
MLA_NOPE = 128
MLA_ROPE = 64
MLA_V = 128
SWA_KV_HEADS = 2
HEAD_DIM = 64
WINDOW = 128
IN_COLS = 3144
IN_COLS_PAD = 3200

ADAM_LR = 0.001
ADAM_B1 = 0.9
ADAM_B2 = 0.999
ADAM_EPS = 1e-08
ADAM_WD = 0.01
ADAM_STEP = 10

LANES = 128
NEG = -1e30
LOG2E = 1.4426950408889634
LN2 = 0.6931471805599453
VMEM_LIMIT_BYTES = 48 * 1024 * 1024

EW_ROWS = 256
MM_TM = 1024
MM_TN = 1024
MM_TK = 1024
ATT_T = 512

BIG = ("ffn1_w_gate", "ffn1_w_up", "ffn1_w_down", "w_in", "mla_w_q_b", "mla_w_kv_b", "w_out",
       "ffn2_w_gate", "ffn2_w_up", "ffn2_w_down")
ROW_SHARDED = ("ffn1_w_down", "w_out", "ffn2_w_down")
SMALL = ("ffn1_norm", "mix_norm", "mla_q_norm", "mla_kv_norm", "swa_sinks", "fox_forget_bias", "ffn2_norm",
         "final_norm")
WEIGHTS = ("ffn1_norm", "ffn1_w_gate", "ffn1_w_up", "ffn1_w_down", "mix_norm", "w_in", "mla_q_norm", "mla_w_q_b",
           "mla_kv_norm", "mla_w_kv_b", "swa_sinks", "fox_forget_bias", "w_out", "ffn2_norm", "ffn2_w_gate",
           "ffn2_w_up", "ffn2_w_down", "final_norm")


def _params(**kw):
    return pltpu.CompilerParams(vmem_limit_bytes=VMEM_LIMIT_BYTES, **kw)


def _tile(n, want):
    if n <= want:
        return n
    t = (want // LANES) * LANES
    while n % t:
        t -= LANES
    return t


def _rows(n, want):
    if n <= want:
        return n
    t = (want // 8) * 8
    while n % t:
        t -= 8
    return t


def _ew(name, fn, ins, outs, tm=EW_ROWS):
    tok = None
    for a, kind in ins:
        if kind == "row":
            tok = a.shape[-2]
    tm = _rows(tok, tm)
    steps = tok // tm

    def spec(shape, kind):
        if kind == "row":
            ax = len(shape) - 2
            blk = tuple(tm if d == ax else s for d, s in enumerate(shape))
            return pl.BlockSpec(blk, lambda i, ax=ax, n=len(shape): tuple(i if d == ax else 0 for d in range(n)))
        return pl.BlockSpec(tuple(shape), lambda i, n=len(shape): (0,) * n)

    n_in = len(ins)
    kinds = [k for _, _, k in outs]

    def body(*refs):
        vals = fn(*[r[...] for r in refs[:n_in]])
        for r, v, kind in zip(refs[n_in:], vals, kinds):
            if kind == "row":
                r[...] = v.astype(r.dtype)
            else:
                @pl.when(pl.program_id(0) == 0)
                def _(r=r):
                    r[...] = jnp.zeros(r.shape, r.dtype)
                r[...] += v.astype(r.dtype)

    res = pl.pallas_call(
        body, name=name, grid=(steps,),
        in_specs=[spec(a.shape, k) for a, k in ins],
        out_specs=[spec(s, k) for s, _, k in outs],
        out_shape=[jax.ShapeDtypeStruct(tuple(s), d) for s, d, _ in outs],
        compiler_params=_params(dimension_semantics=("arbitrary",)),
    )(*[a for a, _ in ins])
    return res


def _rms_fwd(name, x, g):
    def fn(x, g):
        r = lax.rsqrt(jnp.mean(x * x, axis=-1, keepdims=True) + RMS_EPS)
        return [x * r * g]
    return _ew(name, fn, [(x, "row"), (g, "full")], [(x.shape, BF16, "row")])[0]


def _rms_bwd(name, dh, x, g, res=None, also_bf16=False):
    def fn(dh, x, g, *rest):
        dh = dh.astype(F32)
        r = lax.rsqrt(jnp.mean(x * x, axis=-1, keepdims=True) + RMS_EPS)
        xh = x * r
        dxh = dh * g
        dx = r * (dxh - xh * jnp.mean(dxh * xh, axis=-1, keepdims=True))
        if rest:
            dx = dx + rest[0]
        return [dx, jnp.sum(dh * xh, axis=0, keepdims=True)] + ([dx] if also_bf16 else [])
    ins = [(dh, "row"), (x, "row"), (g, "full")] + ([(res, "row")] if res is not None else [])
    outs = [(x.shape, F32, "row"), (g.shape, F32, "acc")] + ([(x.shape, BF16, "row")] if also_bf16 else [])
    return _ew(name, fn, ins, outs)


def _rope(name, x, xs, cos, sin, sign=1.0):
    def fn(x, xs, c, s):
        return [x * c + sign * (xs * s)]
    return _ew(name, fn, [(x, "row"), (xs, "row"), (cos, "row"), (sin, "row")], [(x.shape, F32, "row")])[0]


def _rope_tables(positions, inv_freq2):
    def fn(pos, f):
        ang = pos.astype(F32) * f
        return [jnp.cos(ang), jnp.sin(ang)]
    t = positions.shape[0]
    return _ew("rope_tables", fn, [(positions, "row"), (inv_freq2, "full")],
               [((t, 2 * 32), F32, "row"), ((t, 2 * 32), F32, "row")])


def _mm(name, lhs, rhs, terms, epi, out_dtypes, extras=(), ta=False, tb=False, tm=MM_TM, tn=MM_TN, tk=MM_TK):
    if ta:
        kdim, m = lhs[0].shape
    else:
        m, kdim = lhs[0].shape
    n = rhs[0].shape[0] if tb else rhs[0].shape[1]
    tm, tn, tk = _tile(m, tm), _tile(n, tn), _tile(kdim, tk)
    nk = kdim // tk
    n_acc = 1 + max(a for _, _, a in terms)
    nl, nr, ne = len(lhs), len(rhs), len(extras)
    dims = (((0 if ta else 1,), (1 if tb else 0,)), ((), ()))

    def body(*refs):
        l_refs, r_refs = refs[:nl], refs[nl:nl + nr]
        e_refs = refs[nl + nr:nl + nr + ne]
        o_refs = refs[nl + nr + ne:len(refs) - n_acc]
        accs = refs[len(refs) - n_acc:]
        k = pl.program_id(2)

        @pl.when(k == 0)
        def _():
            for acc in accs:
                acc[...] = jnp.zeros(acc.shape, F32)

        lv, rv = {}, {}
        for li, ri, ai in terms:
            if li not in lv:
                lv[li] = l_refs[li][...].astype(BF16)
            if ri not in rv:
                rv[ri] = r_refs[ri][...].astype(BF16)
            accs[ai][...] += lax.dot_general(lv[li], rv[ri], dims, preferred_element_type=F32)

        @pl.when(k == nk - 1)
        def _():
            outs = epi([acc[...] for acc in accs], [e[...] for e in e_refs])
            for o, v in zip(o_refs, outs):
                o[...] = v.astype(o.dtype)

    l_spec = pl.BlockSpec((tk, tm), lambda i, j, k: (k, i)) if ta else pl.BlockSpec((tm, tk), lambda i, j, k: (i, k))
    r_spec = pl.BlockSpec((tn, tk), lambda i, j, k: (j, k)) if tb else pl.BlockSpec((tk, tn), lambda i, j, k: (k, j))
    o_spec = pl.BlockSpec((tm, tn), lambda i, j, k: (i, j))
    return pl.pallas_call(
        body, name=name, grid=(m // tm, n // tn, nk),
        in_specs=[l_spec] * nl + [r_spec] * nr + [o_spec] * ne,
        out_specs=[o_spec] * len(out_dtypes),
        out_shape=[jax.ShapeDtypeStruct((m, n), d) for d in out_dtypes],
        scratch_shapes=[pltpu.VMEM((tm, tn), F32)] * n_acc,
        compiler_params=_params(dimension_semantics=("parallel", "parallel", "arbitrary")),
    )(*lhs, *rhs, *extras)


def _mm1(name, a, b, out_dtype=F32, scale=None, add=None, **kw):
    def epi(accs, ex):
        v = accs[0] if scale is None else accs[0] * scale
        return [v + ex[0] if ex else v]
    return _mm(name, [a], [b], [(0, 0, 0)], epi, [out_dtype], extras=[] if add is None else [add], **kw)[0]


def _scores(q, k, cq, ck):
    s = lax.dot_general(q, k, (((1,), (1,)), ((), ())), preferred_element_type=F32)
    if cq is not None:
        s = s + (cq - ck)
    return s


def _mask(shape, q0, k0, window):
    qpos = q0 + lax.broadcasted_iota(jnp.int32, shape, 0)
    kpos = k0 + lax.broadcasted_iota(jnp.int32, shape, 1)
    mask = kpos <= qpos
    if window:
        mask = mask & (kpos > qpos - WINDOW)
    return mask


def _flash_fwd(name, q, k, v, scale, group=1, cq=None, ck=None, sink=None, window=False, t=ATT_T):
    h_n, tok, dq = q.shape
    dv = v.shape[-1]
    t = WINDOW if window else min(t, tok)
    nq = tok // t
    nj = 2 if window else nq
    bias = cq is not None
    has_sink = sink is not None

    def kv_blk(qi, j):
        return jnp.maximum(qi - 1 + j, 0) if window else jnp.minimum(j, qi)

    def body(*refs):
        q_ref, k_ref, v_ref = refs[:3]
        pos = 3
        cq_ref = ck_ref = sink_ref = None
        if bias:
            cq_ref, ck_ref = refs[pos], refs[pos + 1]
            pos += 2
        if has_sink:
            sink_ref = refs[pos]
            pos += 1
        o_ref, lse_ref, m_s, l_s, acc_s, qs_s = refs[pos:]
        h, qi, j = pl.program_id(0), pl.program_id(1), pl.program_id(2)

        @pl.when(j == 0)
        def _():
            qs_s[...] = (q_ref[...].astype(F32) * (scale * LOG2E)).astype(BF16)
            m_s[...] = jnp.full(m_s.shape, sink_ref[h] * LOG2E if has_sink else NEG, F32)
            l_s[...] = jnp.full(l_s.shape, 1.0 if has_sink else 0.0, F32)
            acc_s[...] = jnp.zeros(acc_s.shape, F32)

        kb = qi - 1 + j if window else j

        def step(masked):
            s = _scores(qs_s[...], k_ref[...], cq_ref[...] if bias else None, ck_ref[...] if bias else None)
            if masked:
                s = jnp.where(_mask(s.shape, qi * t, kb * t, window), s, NEG)
            m_prev = m_s[...]
            m_new = jnp.maximum(m_prev, jnp.max(s, axis=1, keepdims=True))
            alpha = jnp.exp2(m_prev - m_new)
            p = jnp.exp2(s - m_new)
            l_s[...] = alpha * l_s[...] + jnp.sum(p, axis=1, keepdims=True)
            acc_s[...] = alpha * acc_s[...] + jnp.dot(p.astype(BF16), v_ref[...], preferred_element_type=F32)
            m_s[...] = m_new

        if window:
            pl.when(kb >= 0)(lambda: step(True))
        else:
            pl.when(j < qi)(lambda: step(False))
            pl.when(j == qi)(lambda: step(True))

        @pl.when(j == nj - 1)
        def _():
            o_ref[...] = (acc_s[...] / l_s[...]).astype(o_ref.dtype)
            lse_ref[...] = m_s[...] + jnp.log(l_s[...]) * LOG2E

    in_specs = [
        pl.BlockSpec((None, t, dq), lambda h, qi, j: (h, qi, 0)),
        pl.BlockSpec((None, t, dq), lambda h, qi, j: (h // group, kv_blk(qi, j), 0)),
        pl.BlockSpec((None, t, dv), lambda h, qi, j: (h // group, kv_blk(qi, j), 0)),
    ]
    args = [q, k, v]
    if bias:
        in_specs += [pl.BlockSpec((None, t, 1), lambda h, qi, j: (h, qi, 0)),
                     pl.BlockSpec((None, 1, t), lambda h, qi, j: (h, 0, kv_blk(qi, j)))]
        args += [cq, ck]
    if has_sink:
        in_specs.append(pl.BlockSpec(memory_space=pltpu.SMEM))
        args.append(sink)
    return pl.pallas_call(
        body, name=name, grid=(h_n, nq, nj),
        in_specs=in_specs,
        out_specs=[pl.BlockSpec((None, t, dv), lambda h, qi, j: (h, qi, 0)),
                   pl.BlockSpec((None, t, 1), lambda h, qi, j: (h, qi, 0))],
        out_shape=[jax.ShapeDtypeStruct((h_n, tok, dv), BF16), jax.ShapeDtypeStruct((h_n, tok, 1), F32)],
        scratch_shapes=[pltpu.VMEM((t, 1), F32), pltpu.VMEM((t, 1), F32), pltpu.VMEM((t, dv), F32),
                        pltpu.VMEM((t, dq), BF16)],
        compiler_params=_params(dimension_semantics=("parallel", "parallel", "arbitrary")),
    )(*args)


def _flash_bwd(name, q, k, v, do, lse, delta, scale, group=1, cq=None, ck=None, window=False, t=ATT_T):
    h_n, tok, dq = q.shape
    dv = v.shape[-1]
    t = WINDOW if window else min(t, tok)
    nq = tok // t
    nj = 2 if window else nq
    bias = cq is not None

    def q_blk(ki, j):
        return jnp.minimum(ki + j, nq - 1) if window else jnp.maximum(j, ki)

    def body(*refs):
        q_ref, k_ref, v_ref, do_ref, lse_ref, dl_ref = refs[:6]
        pos = 6
        cq_ref = ck_ref = None
        if bias:
            cq_ref, ck_ref = refs[pos], refs[pos + 1]
            pos += 2
        dq_ref, dk_ref, dv_ref = refs[pos:pos + 3]
        pos += 3
        dc_ref = dr_ref = None
        if bias:
            dc_ref, dr_ref = refs[pos], refs[pos + 1]
            pos += 2
        dk_s, dv_s = refs[pos], refs[pos + 1]
        dc_s = refs[pos + 2] if bias else None
        ki, j = pl.program_id(1), pl.program_id(2)

        @pl.when((ki == 0) & (j == 0))
        def _():
            dq_ref[...] = jnp.zeros(dq_ref.shape, F32)
            if bias:
                dr_ref[...] = jnp.zeros(dr_ref.shape, F32)

        @pl.when(j == 0)
        def _():
            dk_s[...] = jnp.zeros(dk_s.shape, F32)
            dv_s[...] = jnp.zeros(dv_s.shape, F32)
            if bias:
                dc_s[...] = jnp.zeros(dc_s.shape, F32)

        qb = ki + j if window else j

        def step(masked):
            qv, kv, dov = q_ref[...], k_ref[...], do_ref[...]
            qs = (qv.astype(F32) * (scale * LOG2E)).astype(BF16)
            s = _scores(qs, kv, cq_ref[...] if bias else None, ck_ref[...] if bias else None)
            p = jnp.exp2(s - lse_ref[...])
            if masked:
                p = jnp.where(_mask(s.shape, qb * t, ki * t, window), p, 0.0)
            pb = p.astype(BF16)
            dv_s[...] += lax.dot_general(pb, dov, (((0,), (0,)), ((), ())), preferred_element_type=F32)
            dp = lax.dot_general(dov, v_ref[...], (((1,), (1,)), ((), ())), preferred_element_type=F32)
            ds = p * (dp - dl_ref[...])
            dsb = ds.astype(BF16)
            rows = pl.ds(pl.multiple_of(qb * t, t), t)
            dq_ref[rows, :] += scale * jnp.dot(dsb, kv, preferred_element_type=F32)
            dk_s[...] += scale * lax.dot_general(dsb, qv, (((0,), (0,)), ((), ())), preferred_element_type=F32)
            if bias:
                dc_s[...] += jnp.sum(ds, axis=0, keepdims=True)
                dr_ref[rows, :] += jnp.sum(ds, axis=1, keepdims=True)

        if window:
            pl.when(qb < nq)(lambda: step(True))
        else:
            pl.when(j > ki)(lambda: step(False))
            pl.when(j == ki)(lambda: step(True))

        @pl.when(j == nj - 1)
        def _():
            dk_ref[...] = dk_s[...]
            dv_ref[...] = dv_s[...]
            if bias:
                dc_ref[...] = dc_s[...]

    def qmap(h, ki, j):
        return (h, q_blk(ki, j), 0)

    in_specs = [
        pl.BlockSpec((None, t, dq), qmap),
        pl.BlockSpec((None, t, dq), lambda h, ki, j: (h // group, ki, 0)),
        pl.BlockSpec((None, t, dv), lambda h, ki, j: (h // group, ki, 0)),
        pl.BlockSpec((None, t, dv), qmap),
        pl.BlockSpec((None, t, 1), qmap),
        pl.BlockSpec((None, t, 1), qmap),
    ]
    args = [q, k, v, do, lse, delta]
    out_specs = [pl.BlockSpec((None, tok, dq), lambda h, ki, j: (h, 0, 0)),
                 pl.BlockSpec((None, t, dq), lambda h, ki, j: (h, ki, 0)),
                 pl.BlockSpec((None, t, dv), lambda h, ki, j: (h, ki, 0))]
    out_shape = [jax.ShapeDtypeStruct((h_n, tok, dq), F32), jax.ShapeDtypeStruct((h_n, tok, dq), F32),
                 jax.ShapeDtypeStruct((h_n, tok, dv), F32)]
    scratch = [pltpu.VMEM((t, dq), F32), pltpu.VMEM((t, dv), F32)]
    if bias:
        in_specs += [pl.BlockSpec((None, t, 1), qmap), pl.BlockSpec((None, 1, t), lambda h, ki, j: (h, 0, ki))]
        args += [cq, ck]
        out_specs += [pl.BlockSpec((None, 1, t), lambda h, ki, j: (h, 0, ki)),
                      pl.BlockSpec((None, tok, 1), lambda h, ki, j: (h, 0, 0))]
        out_shape += [jax.ShapeDtypeStruct((h_n, 1, tok), F32), jax.ShapeDtypeStruct((h_n, tok, 1), F32)]
        scratch.append(pltpu.VMEM((1, t), F32))
    return pl.pallas_call(
        body, name=name, grid=(h_n, nq, nj),
        in_specs=in_specs, out_specs=out_specs, out_shape=out_shape, scratch_shapes=scratch,
        compiler_params=_params(dimension_semantics=("parallel", "arbitrary", "arbitrary")),
    )(*args)


def _delta(name, do, o):
    def fn(do, o):
        return [jnp.sum(do.astype(F32) * o.astype(F32), axis=-1, keepdims=True)]
    return _ew(name, fn, [(do, "row"), (o, "row")], [(do.shape[:2] + (1,), F32, "row")], tm=512)[0]


def _sink_grad(name, sink3, lse, delta):
    def fn(sk, lse, dl):
        return [-jnp.sum(jnp.exp2(sk * LOG2E - lse) * dl, axis=1, keepdims=True)]
    return _ew(name, fn, [(sink3, "full"), (lse, "row"), (delta, "row")], [(sink3.shape, F32, "acc")], tm=512)[0]


def _resident(block, index_map):
    return pl.BlockSpec(block, index_map, pipeline_mode=pl.Buffered(1))


def _attn_fwd(name, q_t, k, v_t4, scale, group=1, cq_row=None, ck_col=None, sink=None, window=False, t=ATT_T):
    h_n, dq, tok = q_t.shape
    dv = v_t4.shape[2]
    t = min(t, tok)
    nq = tok // t
    bias = cq_row is not None
    has_sink = sink is not None

    def body(*refs):
        q_ref, k_ref, v_ref = refs[:3]
        pos = 3
        cq_ref = ck_ref = sink_ref = None
        if bias:
            cq_ref, ck_ref = refs[pos], refs[pos + 1]
            pos += 2
        if has_sink:
            sink_ref = refs[pos]
            pos += 1
        o_ref, lse_ref, m_s, l_s, acc_s = refs[pos:]
        h, qi = pl.program_id(0), pl.program_id(1)
        qs = (q_ref[...].astype(F32) * (scale * LOG2E)).astype(BF16)
        m_s[...] = jnp.full(m_s.shape, sink_ref[h] * LOG2E if has_sink else NEG, F32)
        l_s[...] = jnp.full(l_s.shape, 1.0 if has_sink else 0.0, F32)
        acc_s[...] = jnp.zeros(acc_s.shape, F32)
        c_ref = cq_ref[:, 0:1] if bias else None

        def step(j, masked):
            rows = pl.ds(pl.multiple_of(j * t, t), t)
            s = jnp.dot(k_ref[rows, :], qs, preferred_element_type=F32)
            if bias:
                s = s - (ck_ref[rows, :] - c_ref)
            if masked:
                kpos = j * t + lax.broadcasted_iota(jnp.int32, s.shape, 0)
                qpos = qi * t + lax.broadcasted_iota(jnp.int32, s.shape, 1)
                mask = kpos <= qpos
                if window:
                    mask = mask & (kpos > qpos - WINDOW)
                s = jnp.where(mask, s, NEG)
            m_prev = m_s[...]
            m_new = jnp.maximum(m_prev, jnp.max(s, axis=0, keepdims=True))
            alpha = jnp.exp2(m_prev - m_new)
            p = jnp.exp2(s - m_new)
            l_s[...] = alpha * l_s[...] + jnp.sum(p, axis=0, keepdims=True)
            acc_s[...] = alpha * acc_s[...] + jnp.dot(v_ref[j], p.astype(BF16), preferred_element_type=F32)
            m_s[...] = m_new

        if window:
            pl.when(qi > 0)(lambda: step(qi - 1, True))
        else:
            def below(j, carry):
                step(j, False)
                return carry
            lax.fori_loop(0, qi, below, 0)
        step(qi, True)
        o_ref[...] = (acc_s[...] / l_s[...]).astype(o_ref.dtype)
        lse_ref[...] = m_s[...] + jnp.log(l_s[...]) * LOG2E

    nk = tok // t
    in_specs = [
        pl.BlockSpec((None, dq, t), lambda h, qi: (h, 0, qi)),
        _resident((None, tok, dq), lambda h, qi: (h // group, 0, 0)),
        _resident((None, nk, dv, t), lambda h, qi: (h // group, 0, 0, 0)),
    ]
    args = [q_t, k, v_t4]
    if bias:
        in_specs += [pl.BlockSpec((None, 1, t), lambda h, qi: (h, 0, qi)),
                     _resident((None, tok, 1), lambda h, qi: (h, 0, 0))]
        args += [cq_row, ck_col]
    if has_sink:
        in_specs.append(pl.BlockSpec(memory_space=pltpu.SMEM))
        args.append(sink)
    return pl.pallas_call(
        body, name=name, grid=(h_n, nq),
        in_specs=in_specs,
        out_specs=[pl.BlockSpec((None, dv, t), lambda h, qi: (h, 0, qi)),
                   pl.BlockSpec((None, 1, t), lambda h, qi: (h, 0, qi))],
        out_shape=[jax.ShapeDtypeStruct((h_n, dv, tok), BF16), jax.ShapeDtypeStruct((h_n, 1, tok), F32)],
        scratch_shapes=[pltpu.VMEM((1, t), F32), pltpu.VMEM((1, t), F32), pltpu.VMEM((dv, t), F32)],
        compiler_params=_params(dimension_semantics=("parallel", "parallel")),
    )(*args)


def _attn_bwd(name, q, q_t, k, k_t4, v_t4, do, do_t, lse, delta, scale, group=1, cq_col=None, ck_row4=None,
              window=False, t=ATT_T):
    h_n, tok, dq = q.shape
    dv = do.shape[2]
    t = min(t, tok)
    nq = tok // t
    bias = cq_col is not None

    def body(*refs):
        q_ref, qt_ref, k_ref, kt_ref, vt_ref, do_ref, dot_ref, lse_ref, dl_ref = refs[:9]
        pos = 9
        cq_ref = ck_ref = dc_ref = dr_ref = None
        if bias:
            cq_ref, ck_ref = refs[pos], refs[pos + 1]
            pos += 2
        dq_ref, dk_ref, dv_ref = refs[pos:pos + 3]
        if bias:
            dc_ref, dr_ref = refs[pos + 3], refs[pos + 4]
        qi = pl.program_id(1)

        @pl.when(qi == 0)
        def _():
            dk_ref[...] = jnp.zeros(dk_ref.shape, F32)
            dv_ref[...] = jnp.zeros(dv_ref.shape, F32)
            if bias:
                dc_ref[...] = jnp.zeros(dc_ref.shape, F32)

        qs = (q_ref[...].astype(F32) * (scale * LOG2E)).astype(BF16)
        dq_ref[...] = jnp.zeros(dq_ref.shape, F32)
        if bias:
            dr_ref[...] = jnp.zeros(dr_ref.shape, F32)
        c_ref = cq_ref[0:1, :] if bias else None

        def step(j, masked):
            s = jnp.dot(qs, kt_ref[j], preferred_element_type=F32)
            if bias:
                s = s - (ck_ref[j] - c_ref)
            p = jnp.exp2(s - lse_ref[...])
            if masked:
                qpos = qi * t + lax.broadcasted_iota(jnp.int32, s.shape, 0)
                kpos = j * t + lax.broadcasted_iota(jnp.int32, s.shape, 1)
                mask = kpos <= qpos
                if window:
                    mask = mask & (kpos > qpos - WINDOW)
                p = jnp.where(mask, p, 0.0)
            pb = p.astype(BF16)
            dv_ref[j] += jnp.dot(dot_ref[...], pb, preferred_element_type=F32)
            dp = jnp.dot(do_ref[...], vt_ref[j], preferred_element_type=F32)
            ds = p * (dp - dl_ref[...])
            dsb = ds.astype(BF16)
            rows = pl.ds(pl.multiple_of(j * t, t), t)
            dq_ref[...] += jnp.dot(dsb, k_ref[rows, :], preferred_element_type=F32)
            dk_ref[j] += scale * jnp.dot(qt_ref[...], dsb, preferred_element_type=F32)
            if bias:
                dc_ref[j] += jnp.sum(ds, axis=0, keepdims=True)
                dr_ref[...] += jnp.sum(ds, axis=1, keepdims=True)

        if window:
            pl.when(qi > 0)(lambda: step(qi - 1, True))
        else:
            def below(j, carry):
                step(j, False)
                return carry
            lax.fori_loop(0, qi, below, 0)
        step(qi, True)
        dq_ref[...] = dq_ref[...] * scale

    nk = nq

    def q_tile(shape_tail):
        return pl.BlockSpec((None, t) + shape_tail, lambda h, qi: (h, qi, 0))

    in_specs = [
        q_tile((dq,)),
        pl.BlockSpec((None, dq, t), lambda h, qi: (h, 0, qi)),
        _resident((None, tok, dq), lambda h, qi: (h // group, 0, 0)),
        _resident((None, nk, dq, t), lambda h, qi: (h // group, 0, 0, 0)),
        _resident((None, nk, dv, t), lambda h, qi: (h // group, 0, 0, 0)),
        q_tile((dv,)),
        pl.BlockSpec((None, dv, t), lambda h, qi: (h, 0, qi)),
        q_tile((1,)),
        q_tile((1,)),
    ]
    args = [q, q_t, k, k_t4, v_t4, do, do_t, lse, delta]
    out_specs = [q_tile((dq,)),
                 pl.BlockSpec((None, nk, dq, t), lambda h, qi: (h, 0, 0, 0)),
                 pl.BlockSpec((None, nk, dv, t), lambda h, qi: (h, 0, 0, 0))]
    out_shape = [jax.ShapeDtypeStruct((h_n, tok, dq), F32), jax.ShapeDtypeStruct((h_n, nk, dq, t), F32),
                 jax.ShapeDtypeStruct((h_n, nk, dv, t), F32)]
    if bias:
        in_specs += [q_tile((1,)), _resident((None, nk, 1, t), lambda h, qi: (h, 0, 0, 0))]
        args += [cq_col, ck_row4]
        out_specs += [pl.BlockSpec((None, nk, 1, t), lambda h, qi: (h, 0, 0, 0)), q_tile((1,))]
        out_shape += [jax.ShapeDtypeStruct((h_n, nk, 1, t), F32), jax.ShapeDtypeStruct((h_n, tok, 1), F32)]
    return pl.pallas_call(
        body, name=name, grid=(h_n, nq),
        in_specs=in_specs, out_specs=out_specs, out_shape=out_shape,
        compiler_params=_params(dimension_semantics=("parallel", "arbitrary")),
    )(*args)


def _delta_t(name, do_t, o_t, tl=1024):
    h_n, dv, tok = do_t.shape
    tl = min(tl, tok)

    def body(do_ref, o_ref, out_ref):
        out_ref[...] = jnp.sum(do_ref[...].astype(F32) * o_ref[...].astype(F32), axis=1, keepdims=True)

    spec = pl.BlockSpec((h_n, dv, tl), lambda i: (0, 0, i))
    return pl.pallas_call(
        body, name=name, grid=(tok // tl,), in_specs=[spec, spec],
        out_specs=pl.BlockSpec((h_n, 1, tl), lambda i: (0, 0, i)),
        out_shape=jax.ShapeDtypeStruct((h_n, 1, tok), F32),
        compiler_params=_params(dimension_semantics=("parallel",)),
    )(do_t, o_t)


def _log_sigmoid(z):
    return jnp.minimum(z, 0.0) - jnp.log(1.0 + jnp.exp(-jnp.abs(z)))


def _gate_fwd(zt, bias):
    tok = zt.shape[1]

    def body(z_ref, b_ref, c_ref):
        x = _log_sigmoid(z_ref[...] + b_ref[...])
        lane = lax.broadcasted_iota(jnp.int32, x.shape, 1)
        k = 1
        while k < tok:
            x = x + jnp.where(lane >= k, pltpu.roll(x, k, axis=1), 0.0)
            k *= 2
        c_ref[...] = x * LOG2E

    return pl.pallas_call(body, name="fox_gate_fwd", out_shape=jax.ShapeDtypeStruct(zt.shape, F32),
                          compiler_params=_params())(zt, bias)


def _gate_bwd(d_rows, d_cols, zt, bias):
    tok = zt.shape[1]

    def body(dr_ref, dc_ref, z_ref, b_ref, dz_ref, db_ref):
        x = dr_ref[...] - dc_ref[...]
        lane = lax.broadcasted_iota(jnp.int32, x.shape, 1)
        k = 1
        while k < tok:
            x = x + jnp.where(lane < tok - k, pltpu.roll(x, tok - k, axis=1), 0.0)
            k *= 2
        dz = x / (1.0 + jnp.exp(z_ref[...] + b_ref[...]))
        dz_ref[...] = dz
        db_ref[...] = jnp.sum(dz, axis=1, keepdims=True)

    return pl.pallas_call(body, name="fox_gate_bwd",
                          out_shape=[jax.ShapeDtypeStruct(zt.shape, F32), jax.ShapeDtypeStruct(bias.shape, F32)],
                          compiler_params=_params())(d_rows, d_cols, zt, bias)


_NT = (((1,), (1,)), ((), ()))
_TN = (((0,), (0,)), ((), ()))


def _ffn_gate_up(name, h, wg, wu, l, tm=512):
    tok, d = h.shape
    nb, n = wg.shape[0], wg.shape[3]
    tm = min(tm, tok)

    def body(h_ref, wg_ref, wu_ref, u_ref, v_ref, a_ref):
        hv = h_ref[...]
        u = jnp.dot(hv, wg_ref[...], preferred_element_type=F32)
        v = jnp.dot(hv, wu_ref[...], preferred_element_type=F32)
        u_ref[...] = u.astype(BF16)
        v_ref[...] = v.astype(BF16)
        a_ref[...] = (u * jax.nn.sigmoid(u) * v).astype(BF16)

    w_spec = pl.BlockSpec((None, None, d, n), lambda j, i: (j, l, 0, 0))
    o_spec = pl.BlockSpec((None, tm, n), lambda j, i: (j, i, 0))
    return pl.pallas_call(
        body, name=name, grid=(nb, tok // tm),
        in_specs=[pl.BlockSpec((tm, d), lambda j, i: (i, 0)), w_spec, w_spec],
        out_specs=[o_spec] * 3, out_shape=[jax.ShapeDtypeStruct((nb, tok, n), BF16)] * 3,
        compiler_params=_params(dimension_semantics=("parallel", "parallel")),
    )(h, wg, wu)


def _ffn_down(name, a, wd, x, l, tm=1024, tn=1024):
    nb, tok, n = a.shape
    d = wd.shape[3]
    tm, tn = min(tm, tok), min(tn, d)

    def body(a_ref, wd_ref, x_ref, y_ref, acc):
        j = pl.program_id(2)

        @pl.when(j == 0)
        def _():
            acc[...] = jnp.zeros(acc.shape, F32)
        acc[...] += jnp.dot(a_ref[...], wd_ref[...], preferred_element_type=F32)

        @pl.when(j == nb - 1)
        def _():
            y_ref[...] = x_ref[...] + 0.5 * acc[...]

    return pl.pallas_call(
        body, name=name, grid=(tok // tm, d // tn, nb),
        in_specs=[pl.BlockSpec((None, tm, n), lambda i, c, j: (j, i, 0)),
                  pl.BlockSpec((None, None, n, tn), lambda i, c, j: (j, l, 0, c)),
                  pl.BlockSpec((tm, tn), lambda i, c, j: (i, c))],
        out_specs=pl.BlockSpec((tm, tn), lambda i, c, j: (i, c)),
        out_shape=jax.ShapeDtypeStruct((tok, d), F32),
        scratch_shapes=[pltpu.VMEM((tm, tn), F32)],
        compiler_params=_params(dimension_semantics=("parallel", "parallel", "arbitrary")),
    )(a, wd, x)


def _ffn_d_act(name, dy, u, v, wd, l, tm=512):
    nb, tok, n = u.shape
    d = dy.shape[1]
    tm = min(tm, tok)

    def body(dy_ref, u_ref, v_ref, wd_ref, du_ref, dv_ref):
        da = 0.5 * lax.dot_general(dy_ref[...], wd_ref[...], _NT, preferred_element_type=F32)
        uv, vv = u_ref[...].astype(F32), v_ref[...].astype(F32)
        sg = jax.nn.sigmoid(uv)
        du_ref[...] = (da * vv * (sg * (1.0 + uv * (1.0 - sg)))).astype(BF16)
        dv_ref[...] = (da * (uv * sg)).astype(BF16)

    t_spec = pl.BlockSpec((None, tm, n), lambda j, i: (j, i, 0))
    return pl.pallas_call(
        body, name=name, grid=(nb, tok // tm),
        in_specs=[pl.BlockSpec((tm, d), lambda j, i: (i, 0)), t_spec, t_spec,
                  pl.BlockSpec((None, None, n, d), lambda j, i: (j, l, 0, 0))],
        out_specs=[t_spec] * 2, out_shape=[jax.ShapeDtypeStruct((nb, tok, n), BF16)] * 2,
        compiler_params=_params(dimension_semantics=("parallel", "parallel")),
    )(dy, u, v, wd)


def _ffn_d_h(name, du, dv, wg, wu, l, tm=1024):
    nb, tok, n = du.shape
    d = wg.shape[2]
    tm = min(tm, tok)

    def body(du_ref, dv_ref, wg_ref, wu_ref, dh_ref, acc):
        j = pl.program_id(1)

        @pl.when(j == 0)
        def _():
            acc[...] = jnp.zeros(acc.shape, F32)
        acc[...] += (lax.dot_general(du_ref[...], wg_ref[...], _NT, preferred_element_type=F32)
                     + lax.dot_general(dv_ref[...], wu_ref[...], _NT, preferred_element_type=F32))

        @pl.when(j == nb - 1)
        def _():
            dh_ref[...] = acc[...].astype(BF16)

    t_spec = pl.BlockSpec((None, tm, n), lambda i, j: (j, i, 0))
    w_spec = pl.BlockSpec((None, None, d, n), lambda i, j: (j, l, 0, 0))
    return pl.pallas_call(
        body, name=name, grid=(tok // tm, nb),
        in_specs=[t_spec, t_spec, w_spec, w_spec],
        out_specs=pl.BlockSpec((tm, d), lambda i, j: (i, 0)),
        out_shape=jax.ShapeDtypeStruct((tok, d), BF16),
        scratch_shapes=[pltpu.VMEM((tm, d), F32)],
        compiler_params=_params(dimension_semantics=("parallel", "arbitrary")),
    )(du, dv, wg, wu)


def _ffn_wgrad_in(name, h, du, dv, l, like, prev=None, tk=512, td=1024):
    tok, d = h.shape
    nb, _, n = du.shape
    tk, td = min(tk, tok), min(td, d)
    nk = tok // tk
    n_in = 3

    def body(*refs):
        h_ref, du_ref, dv_ref = refs[:3]
        og_ref, ou_ref, accg, accu = refs[len(refs) - 4:]
        k = pl.program_id(2)

        @pl.when(k == 0)
        def _():
            accg[...] = jnp.zeros(accg.shape, F32)
            accu[...] = jnp.zeros(accu.shape, F32)
        hv = h_ref[...]
        accg[...] += lax.dot_general(hv, du_ref[...], _TN, preferred_element_type=F32)
        accu[...] += lax.dot_general(hv, dv_ref[...], _TN, preferred_element_type=F32)

        @pl.when(k == nk - 1)
        def _():
            og_ref[...] = accg[...].astype(BF16)
            ou_ref[...] = accu[...].astype(BF16)

    t_spec = pl.BlockSpec((None, tk, n), lambda j, c, k: (j, k, 0))
    o_spec = pl.BlockSpec((None, None, td, n), lambda j, c, k: (j, l, c, 0))
    in_specs = [pl.BlockSpec((tk, td), lambda j, c, k: (k, c)), t_spec, t_spec]
    args = [h, du, dv]
    aliases = {}
    if prev is not None:
        in_specs += [pl.BlockSpec(memory_space=pl.ANY)] * 2
        args += list(prev)
        aliases = {n_in: 0, n_in + 1: 1}
    return pl.pallas_call(
        body, name=name, grid=(nb, d // td, nk),
        in_specs=in_specs, out_specs=[o_spec] * 2,
        out_shape=[jax.ShapeDtypeStruct(like.shape, BF16)] * 2,
        scratch_shapes=[pltpu.VMEM((td, n), F32)] * 2,
        input_output_aliases=aliases,
        compiler_params=_params(dimension_semantics=("parallel", "parallel", "arbitrary")),
    )(*args)


def _ffn_wgrad_out(name, a, dy, l, like, prev=None, tk=512):
    nb, tok, n = a.shape
    d = dy.shape[1]
    tk = min(tk, tok)
    nk = tok // tk

    def body(*refs):
        a_ref, dy_ref = refs[:2]
        o_ref, acc = refs[len(refs) - 2:]
        k = pl.program_id(1)

        @pl.when(k == 0)
        def _():
            acc[...] = jnp.zeros(acc.shape, F32)
        acc[...] += lax.dot_general(a_ref[...], dy_ref[...], _TN, preferred_element_type=F32)

        @pl.when(k == nk - 1)
        def _():
            o_ref[...] = (0.5 * acc[...]).astype(BF16)

    in_specs = [pl.BlockSpec((None, tk, n), lambda j, k: (j, k, 0)), pl.BlockSpec((tk, d), lambda j, k: (k, 0))]
    args = [a, dy]
    aliases = {}
    if prev is not None:
        in_specs.append(pl.BlockSpec(memory_space=pl.ANY))
        args.append(prev)
        aliases = {2: 0}
    return pl.pallas_call(
        body, name=name, grid=(nb, nk),
        in_specs=in_specs, out_specs=pl.BlockSpec((None, None, n, d), lambda j, k: (j, l, 0, 0)),
        out_shape=jax.ShapeDtypeStruct(like.shape, BF16),
        scratch_shapes=[pltpu.VMEM((n, d), F32)],
        input_output_aliases=aliases,
        compiler_params=_params(dimension_semantics=("parallel", "arbitrary")),
    )(*args)


def _ffn_fwd(tag, x, g, wg, wu, wd, l):
    h = _rms_fwd(tag + "_norm", x, g)
    u, v, a = _ffn_gate_up(tag + "_gate_up", h, wg, wu, l)
    y = _ffn_down(tag + "_down", a, wd, x, l)
    return y, (x, h, u, v, a)


def _ffn_bwd(tag, dy, dyb, saved, g, wg, wu, wd, l, prev):
    x, h, u, v, a = saved
    du, dv = _ffn_d_act(tag + "_d_act", dyb, u, v, wd, l)
    d_wd = _ffn_wgrad_out(tag + "_d_wd", a, dyb, l, wd, None if prev is None else prev[2])
    d_wg, d_wu = _ffn_wgrad_in(tag + "_d_wgu", h, du, dv, l, wg, None if prev is None else prev[:2])
    dh = _ffn_d_h(tag + "_d_h", du, dv, wg, wu, l)
    dx, dg, dxb = _rms_bwd(tag + "_d_norm", dh, x, g, res=dy, also_bf16=True)
    return dx, dxb, dg, (d_wg, d_wu, d_wd)


def _swap_halves(x):
    t, w = x.shape
    return x.reshape(t, w // HEAD_DIM, 2, HEAD_DIM // 2)[:, :, ::-1, :].reshape(t, w)


def _unheads(x):
    n, t, d = x.shape
    return x.transpose(1, 0, 2).reshape(t, n * d)


def _nat(x3):
    return x3.transpose(1, 0, 2).astype(BF16)


def _tr(x3):
    return x3.transpose(1, 2, 0).astype(BF16)


def _tr4(x3, t):
    tok, n, d = x3.shape
    return x3.reshape(tok // t, t, n, d).transpose(2, 0, 3, 1).astype(BF16)


def _from_t(x_t):
    n, d, tok = x_t.shape
    return x_t.transpose(2, 0, 1).reshape(tok, n * d)


def _from_t4(x4):
    n, nk, d, t = x4.shape
    return x4.transpose(1, 3, 0, 2).reshape(nk * t, n, d)


def _col(row):
    return row.transpose(0, 2, 1)


def _mixer_fwd(tag, x, w, cos, sin):
    tok = x.shape[0]
    h2 = _rms_fwd(tag + "_norm", x, w["mix_norm"])
    p = _mm1(tag + "_in", h2, w["w_in"], tn=640)
    c_q, c_kv = p[:, :512], p[:, 512:768]
    q_s, k_s, v_s = p[:, 768:1280], p[:, 1280:1408], p[:, 1408:1536]
    q_f, k_f, v_f = p[:, 1536:2048], p[:, 2048:2560], p[:, 2560:3072]
    k_rope, f_logit = p[:, 3072:3136], p[:, 3136:3144]

    qn = _rms_fwd(tag + "_q_norm", c_q, w["mla_q_norm"])
    qm = _mm1(tag + "_q_b", qn, w["mla_w_q_b"])
    kvn = _rms_fwd(tag + "_kv_norm", c_kv, w["mla_kv_norm"])
    kvm = _mm1(tag + "_kv_b", kvn, w["mla_w_kv_b"])

    rin = jnp.concatenate([qm[:, 1024:], q_s, k_s, k_rope, jnp.zeros((tok, 64), F32)], axis=1)
    rout = _rope(tag + "_rope", rin, _swap_halves(rin), cos, sin)
    q_pe, q_sr, k_sr, k_pe = rout[:, :512], rout[:, 512:1024], rout[:, 1024:1152], rout[:, 1152:1216]

    t = min(ATT_T, tok)
    q_m = jnp.concatenate([qm[:, :1024].reshape(tok, HEADS, MLA_NOPE), q_pe.reshape(tok, HEADS, MLA_ROPE)], axis=-1)
    k_m = jnp.concatenate([kvm[:, :1024].reshape(tok, HEADS, MLA_NOPE),
                           jnp.broadcast_to(k_pe[:, None, :], (tok, HEADS, MLA_ROPE))], axis=-1)
    v_m = kvm[:, 1024:].reshape(tok, HEADS, MLA_V)
    mla = dict(q=_nat(q_m), q_t=_tr(q_m), k=_nat(k_m), k_t4=_tr4(k_m, t), v_t4=_tr4(v_m, t))
    mla["o_t"], mla["lse"] = _attn_fwd(tag + "_mla_fwd", mla["q_t"], mla["k"], mla["v_t4"],
                                       (MLA_NOPE + MLA_ROPE) ** -0.5, t=t)

    q_s3, k_s3 = q_sr.reshape(tok, HEADS, HEAD_DIM), k_sr.reshape(tok, SWA_KV_HEADS, HEAD_DIM)
    v_s3 = v_s.reshape(tok, SWA_KV_HEADS, HEAD_DIM)
    swa = dict(q=_nat(q_s3), q_t=_tr(q_s3), k=_nat(k_s3), k_t4=_tr4(k_s3, t), v_t4=_tr4(v_s3, t))
    swa["o_t"], swa["lse"] = _attn_fwd(tag + "_swa_fwd", swa["q_t"], swa["k"], swa["v_t4"], HEAD_DIM ** -0.5,
                                       group=HEADS // SWA_KV_HEADS, sink=w["swa_sinks"], window=True, t=t)

    zt = f_logit.T
    c = _gate_fwd(zt, w["fox_forget_bias"].reshape(HEADS, 1))
    q_f3, k_f3, v_f3 = (a.reshape(tok, HEADS, HEAD_DIM) for a in (q_f, k_f, v_f))
    fox = dict(q=_nat(q_f3), q_t=_tr(q_f3), k=_nat(k_f3), k_t4=_tr4(k_f3, t), v_t4=_tr4(v_f3, t),
               c_col=c[:, :, None], c_row4=c.reshape(HEADS, tok // t, 1, t))
    fox["o_t"], fox["lse"] = _attn_fwd(tag + "_fox_fwd", fox["q_t"], fox["k"], fox["v_t4"], HEAD_DIM ** -0.5,
                                       cq_row=c[:, None, :], ck_col=fox["c_col"], t=t)

    mixed = jnp.concatenate([_from_t(mla["o_t"]), _from_t(swa["o_t"]), _from_t(fox["o_t"])], axis=1)
    y = _mm1(tag + "_out", mixed, w["w_out"], add=x)
    saved = dict(x=x, h2=h2, c_q=c_q, c_kv=c_kv, qn=qn, kvn=kvn, zt=zt, mixed=mixed, mla=mla, swa=swa, fox=fox)
    return y, saved


def _mixer_bwd(tag, dy, dyb, s, w, cos, sin):
    tok = dy.shape[0]
    g = {}
    dmixed = _mm1(tag + "_d_mixed", dyb, w["w_out"], out_dtype=BF16, tb=True)
    g["w_out"] = _mm1(tag + "_d_wout", s["mixed"], dyb, ta=True)
    t = min(ATT_T, tok)

    def attn_bwd(name, a, d_out, scale, **kw):
        do3 = d_out.reshape(tok, HEADS, -1)
        do_t = _tr(do3)
        dl = _col(_delta_t(name + "_delta", do_t, a["o_t"]))
        return dl, _attn_bwd(name + "_bwd", a["q"], a["q_t"], a["k"], a["k_t4"], a["v_t4"], _nat(do3), do_t,
                             _col(a["lse"]), dl, scale, t=t, **kw)

    _, (dq_m, dk_m4, dv_m4) = attn_bwd(tag + "_mla", s["mla"], dmixed[:, :1024], (MLA_NOPE + MLA_ROPE) ** -0.5)
    dl, (dq_sh, dk_s4, dv_s4) = attn_bwd(tag + "_swa", s["swa"], dmixed[:, 1024:1536], HEAD_DIM ** -0.5,
                                         group=HEADS // SWA_KV_HEADS, window=True)
    g["swa_sinks"] = _sink_grad(tag + "_d_sink", w["swa_sinks"].reshape(HEADS, 1, 1), _col(s["swa"]["lse"]),
                                dl).reshape(HEADS)
    _, (dq_fh, dk_f4, dv_f4, d_cols, d_rows) = attn_bwd(tag + "_fox", s["fox"], dmixed[:, 1536:], HEAD_DIM ** -0.5,
                                                        cq_col=s["fox"]["c_col"], ck_row4=s["fox"]["c_row4"])
    dzt, dbias = _gate_bwd(d_rows[:, :, 0], d_cols.reshape(HEADS, tok), s["zt"],
                           w["fox_forget_bias"].reshape(HEADS, 1))
    g["fox_forget_bias"] = dbias.reshape(HEADS)

    grp = HEADS // SWA_KV_HEADS
    dq_mt = dq_m.transpose(1, 0, 2)
    dk_mt = _from_t4(dk_m4)
    d_qpe = dq_mt[:, :, MLA_NOPE:].reshape(tok, HEADS * MLA_ROPE)
    d_kpe_heads = dk_mt[:, :, MLA_NOPE:].reshape(tok, HEADS * MLA_ROPE)
    d_qs = _unheads(dq_sh)
    d_ks_heads = _from_t4(dk_s4).reshape(tok, SWA_KV_HEADS, grp, HEAD_DIM).transpose(0, 2, 1, 3).reshape(tok, grp * 128)
    d_vs_heads = _from_t4(dv_s4).reshape(tok, SWA_KV_HEADS, grp, HEAD_DIM).transpose(0, 2, 1, 3).reshape(tok, grp * 128)

    def fold(d_kpe_h, d_ks_h, d_vs_h):
        kpe = d_kpe_h[:, 0:64]
        for i in range(1, HEADS):
            kpe = kpe + d_kpe_h[:, 64 * i:64 * (i + 1)]
        ks, vs = d_ks_h[:, 0:128], d_vs_h[:, 0:128]
        for i in range(1, grp):
            ks = ks + d_ks_h[:, 128 * i:128 * (i + 1)]
            vs = vs + d_vs_h[:, 128 * i:128 * (i + 1)]
        return [jnp.concatenate([kpe, jnp.zeros_like(kpe)], axis=1), ks, vs]
    d_kpe2, d_ksr, d_vs = _ew(tag + "_fold_heads", fold,
                              [(d_kpe_heads, "row"), (d_ks_heads, "row"), (d_vs_heads, "row")],
                              [((tok, 128), F32, "row"), ((tok, 128), F32, "row"), ((tok, 128), F32, "row")])

    rin = jnp.concatenate([d_qpe, d_qs, d_ksr, d_kpe2], axis=1)
    rout = _rope(tag + "_d_rope", rin, _swap_halves(rin), cos, sin, sign=-1.0)
    d_qpe_pre, d_qs_pre, d_ks_pre, d_krope = rout[:, :512], rout[:, 512:1024], rout[:, 1024:1152], rout[:, 1152:1216]

    d_qm = jnp.concatenate([dq_mt[:, :, :MLA_NOPE].reshape(tok, HEADS * MLA_NOPE), d_qpe_pre], axis=1)
    d_kvm = jnp.concatenate([dk_mt[:, :, :MLA_NOPE].reshape(tok, HEADS * MLA_NOPE),
                             _from_t4(dv_m4).reshape(tok, HEADS * MLA_V)], axis=1)
    g["mla_w_q_b"] = _mm1(tag + "_d_wqb", s["qn"], d_qm, ta=True)
    d_qn = _mm1(tag + "_d_qn", d_qm, w["mla_w_q_b"], tb=True)
    d_cq, g["mla_q_norm"] = _rms_bwd(tag + "_d_q_norm", d_qn, s["c_q"], w["mla_q_norm"])
    g["mla_w_kv_b"] = _mm1(tag + "_d_wkvb", s["kvn"], d_kvm, ta=True)
    d_kvn = _mm1(tag + "_d_kvn", d_kvm, w["mla_w_kv_b"], tb=True)
    d_ckv, g["mla_kv_norm"] = _rms_bwd(tag + "_d_kv_norm", d_kvn, s["c_kv"], w["mla_kv_norm"])

    dp = jnp.concatenate([d_cq, d_ckv, d_qs_pre, d_ks_pre, d_vs, _unheads(dq_fh),
                          _from_t4(dk_f4).reshape(tok, HEADS * HEAD_DIM), _from_t4(dv_f4).reshape(tok, HEADS * HEAD_DIM),
                          d_krope, dzt.T, jnp.zeros((tok, IN_COLS_PAD - IN_COLS), F32)], axis=1).astype(BF16)
    g["w_in"] = _mm1(tag + "_d_win", s["h2"], dp, ta=True, tn=640)
    dh2 = _mm1(tag + "_d_h2", dp, w["w_in"], tb=True, tk=640)
    dx, g["mix_norm"], dxb = _rms_bwd(tag + "_d_norm", dh2, s["x"], w["mix_norm"], res=dy, also_bf16=True)
    return dx, dxb, g


def _loss_head(x, g, target):
    d = x.shape[1]

    def fn(x, g, tgt):
        r = lax.rsqrt(jnp.mean(x * x, axis=-1, keepdims=True) + RMS_EPS)
        xh = x * r
        err = xh * g - tgt
        loss = 0.5 * jnp.sum(jnp.sum(err * err, axis=-1, keepdims=True), axis=0, keepdims=True) / d
        dy = err / d
        dxh = dy * g
        dx = r * (dxh - xh * jnp.mean(dxh * xh, axis=-1, keepdims=True))
        return [dx, jnp.sum(dy * xh, axis=0, keepdims=True), loss, dx]
    return _ew("loss_head", fn, [(x, "row"), (g, "full"), (target, "row")],
               [(x.shape, F32, "row"), (g.shape, F32, "acc"), ((1, 1), F32, "acc"), (x.shape, BF16, "row")])


def _adam_update(w, g, m, v):
    m = ADAM_B1 * m + (1.0 - ADAM_B1) * g
    v = ADAM_B2 * v + (1.0 - ADAM_B2) * (g * g)
    m_hat = m / (1.0 - ADAM_B1 ** ADAM_STEP)
    v_hat = v / (1.0 - ADAM_B2 ** ADAM_STEP)
    return [-ADAM_LR * (m_hat / (jnp.sqrt(v_hat) + ADAM_EPS) + ADAM_WD * w), m, v]


def _adamw(name, w, g, m, v):
    return _ew(name, _adam_update, [(w, "row"), (g, "row"), (m, "row"), (v, "row")], [(w.shape, F32, "row")] * 3)


def _adamw_slots(name, r, w, m, v):
    def fn(r, w, m, v):
        g = r[0].astype(F32)
        for i in range(1, N_DEV):
            g = g + r[i].astype(F32)
        return [g] + _adam_update(w, g, m, v)
    return _ew(name, fn, [(r, "row"), (w, "row"), (m, "row"), (v, "row")], [(w.shape, F32, "row")] * 4, tm=128)


def _coords(dev):
    return (dev // 4, (dev // 2) % 2, dev % 2)


def _all_gather(shards):
    n = len(shards)

    def body(*refs):
        x_refs, out_refs = refs[:n], refs[n:2 * n]
        send_sems, recv_sems, local_sems = refs[2 * n:]
        x, y, c = lax.axis_index("x"), lax.axis_index("y"), lax.axis_index("c")
        me, sibling = (x, y, c), (x, y, 1 - c)
        chips = [(1 - x, y), (x, 1 - y), (1 - x, 1 - y)]

        def slot(a, px, py, pc):
            return out_refs[a].at[4 * px + 2 * py + pc]

        def copy(a, k, block, to, src=None):
            return pltpu.make_async_remote_copy(
                src_ref=slot(a, *block) if src is None else src, dst_ref=slot(a, *block),
                send_sem=send_sems.at[a, k], recv_sem=recv_sems.at[a, k],
                device_id=to, device_id_type=pl.DeviceIdType.MESH)

        mine = [pltpu.make_async_copy(x_refs[a], slot(a, *me), local_sems.at[a]) for a in range(n)]
        first, passed = [], []
        for a in range(n):
            mine[a].start()
            first.append(copy(a, 0, me, sibling, src=x_refs[a]))
            first += [copy(a, 1 + j, me, (*chip, c), src=x_refs[a]) for j, chip in enumerate(chips)]
        for cp in first:
            cp.start()
        for a in range(n):
            for j, chip in enumerate(chips):
                copy(a, 1 + j, (*chip, c), me).wait_recv()
                passed.append(copy(a, 4 + j, (*chip, c), sibling))
                passed[-1].start()
        for a in range(n):
            copy(a, 0, sibling, me).wait_recv()
            for j, chip in enumerate(chips):
                copy(a, 4 + j, (*chip, 1 - c), me).wait_recv()
        for cp in first + passed:
            cp.wait_send()
        for a in range(n):
            mine[a].wait()

    return pl.pallas_call(
        body, name="all_gather_weights",
        out_shape=[jax.ShapeDtypeStruct((N_DEV,) + s.shape, s.dtype) for s in shards],
        in_specs=[pl.BlockSpec(memory_space=pl.ANY)] * n,
        out_specs=[pl.BlockSpec(memory_space=pl.ANY)] * n,
        scratch_shapes=[pltpu.SemaphoreType.DMA((n, 7)), pltpu.SemaphoreType.DMA((n, 7)),
                        pltpu.SemaphoreType.DMA((n,))],
    )(*shards)


def _all_to_all(blocks):
    n = len(blocks)

    def body(*refs):
        g_refs, out_refs = refs[:n], refs[n:2 * n]
        send_sems, recv_sems, local_sems = refs[2 * n:]
        me = 4 * lax.axis_index("x") + 2 * lax.axis_index("y") + lax.axis_index("c")

        def copy(a, peer):
            return pltpu.make_async_remote_copy(
                src_ref=g_refs[a].at[peer], dst_ref=out_refs[a].at[me],
                send_sem=send_sems.at[a, peer], recv_sem=recv_sems.at[a, me],
                device_id=_coords(peer), device_id_type=pl.DeviceIdType.MESH)

        def arrival(a, peer):
            return pltpu.make_async_remote_copy(
                src_ref=g_refs[a].at[peer], dst_ref=out_refs[a].at[peer],
                send_sem=send_sems.at[a, peer], recv_sem=recv_sems.at[a, peer],
                device_id=_coords(peer), device_id_type=pl.DeviceIdType.MESH)

        mine = [pltpu.make_async_copy(g_refs[a].at[me], out_refs[a].at[me], local_sems.at[a]) for a in range(n)]
        for a in range(n):
            mine[a].start()
        for step in ("start", "wait_recv", "wait_send"):
            for peer in range(N_DEV):
                @pl.when(peer != me)
                def _(peer=peer, step=step):
                    for a in range(n):
                        if step == "start":
                            copy(a, peer).start()
                        elif step == "wait_recv":
                            arrival(a, peer).wait_recv()
                        else:
                            copy(a, peer).wait_send()
        for a in range(n):
            mine[a].wait()

    return pl.pallas_call(
        body, name="all_to_all_grads",
        out_shape=[jax.ShapeDtypeStruct(b.shape, b.dtype) for b in blocks],
        in_specs=[pl.BlockSpec(memory_space=pl.ANY)] * n,
        out_specs=[pl.BlockSpec(memory_space=pl.ANY)] * n,
        scratch_shapes=[pltpu.SemaphoreType.DMA((n, N_DEV)), pltpu.SemaphoreType.DMA((n, N_DEV)),
                        pltpu.SemaphoreType.DMA((n,))],
    )(*blocks)


def _all_reduce_small(buf):
    def body(x_ref, out_ref, slots, send_sems, recv_sems):
        me = 4 * lax.axis_index("x") + 2 * lax.axis_index("y") + lax.axis_index("c")

        def copy(peer):
            return pltpu.make_async_remote_copy(
                src_ref=x_ref, dst_ref=slots.at[me],
                send_sem=send_sems.at[peer], recv_sem=recv_sems.at[me],
                device_id=_coords(peer), device_id_type=pl.DeviceIdType.MESH)

        def arrival(peer):
            return pltpu.make_async_remote_copy(
                src_ref=x_ref, dst_ref=slots.at[peer],
                send_sem=send_sems.at[peer], recv_sem=recv_sems.at[peer],
                device_id=_coords(peer), device_id_type=pl.DeviceIdType.MESH)

        slots[pl.ds(me, 1)] = x_ref[...][None]
        for peer in range(N_DEV):
            @pl.when(peer != me)
            def _(peer=peer):
                copy(peer).start()
        for peer in range(N_DEV):
            @pl.when(peer != me)
            def _(peer=peer):
                arrival(peer).wait_recv()
        for peer in range(N_DEV):
            @pl.when(peer != me)
            def _(peer=peer):
                copy(peer).wait_send()
        acc = slots[0]
        for peer in range(1, N_DEV):
            acc = acc + slots[peer]
        out_ref[...] = acc

    return pl.pallas_call(
        body, name="all_reduce_small",
        out_shape=jax.ShapeDtypeStruct(buf.shape, F32),
        in_specs=[pl.BlockSpec(memory_space=pltpu.VMEM)],
        out_specs=pl.BlockSpec(memory_space=pltpu.VMEM),
        scratch_shapes=[pltpu.VMEM((N_DEV,) + buf.shape, F32), pltpu.SemaphoreType.DMA((N_DEV,)),
                        pltpu.SemaphoreType.DMA((N_DEV,))],
        compiler_params=_params(),
    )(buf)


def _in_to_kernel(w):
    pad = jnp.zeros(w.shape[:-1] + (IN_COLS_PAD - IN_COLS,), w.dtype)
    return jnp.concatenate([w[..., :768], w[..., 832:3136], w[..., 768:832], w[..., 3136:], pad], axis=-1)


def _in_from_kernel(w):
    return jnp.concatenate([w[..., :768], w[..., 3072:3136], w[..., 768:3072], w[..., 3136:3144]], axis=-1)


def _split_to_kernel(w, a, b):
    r = w.shape[0]
    w3 = w.reshape(r, HEADS, a + b)
    return jnp.concatenate([w3[:, :, :a].reshape(r, HEADS * a), w3[:, :, a:].reshape(r, HEADS * b)], axis=1)


def _split_from_kernel(w, a, b):
    r = w.shape[0]
    return jnp.concatenate([w[:, :HEADS * a].reshape(r, HEADS, a), w[:, HEADS * a:].reshape(r, HEADS, b)],
                           axis=-1).reshape(r, HEADS * (a + b))


MIXER_BIG = ("w_in", "mla_w_q_b", "mla_w_kv_b", "w_out")


def _col_blocks_to_full(blk):
    nb, k, n = blk.shape
    return blk.transpose(1, 0, 2).reshape(k, nb * n)


def _full_to_col_blocks(g):
    k, c = g.shape
    return g.reshape(k, N_DEV, c // N_DEV).transpose(1, 0, 2).astype(BF16)


def _local_step(x, positions, target, gathered, small):
    inv_freq = ROPE_THETA ** (-jnp.arange(0, HEAD_DIM, 2, dtype=F32) / HEAD_DIM)
    cos64, sin64 = _rope_tables(positions, jnp.concatenate([-inv_freq, inv_freq])[None, :])
    cos = jnp.tile(cos64, (1, 20))
    sin = jnp.tile(sin64, (1, 20))

    layers = []
    for l in range(DEPTH):
        w = {}
        w["w_in"] = _in_to_kernel(_col_blocks_to_full(gathered["w_in"][:, l]))
        w["mla_w_q_b"] = _split_to_kernel(_col_blocks_to_full(gathered["mla_w_q_b"][:, l]), MLA_NOPE, MLA_ROPE)
        w["mla_w_kv_b"] = _split_to_kernel(_col_blocks_to_full(gathered["mla_w_kv_b"][:, l]), MLA_NOPE, MLA_V)
        blk = gathered["w_out"][:, l]
        w["w_out"] = blk.reshape(blk.shape[0] * blk.shape[1], blk.shape[2])
        for k in SMALL:
            if k != "final_norm":
                w[k] = small[k][l][None, :] if "norm" in k else small[k][l]
        layers.append(w)

    def ffn_w(f):
        return gathered[f + "_w_gate"], gathered[f + "_w_up"], gathered[f + "_w_down"]

    saved = []
    for l, w in enumerate(layers):
        t = "l%d" % l
        x, s1 = _ffn_fwd(t + "_ffn1", x, w["ffn1_norm"], *ffn_w("ffn1"), l)
        x, s2 = _mixer_fwd(t + "_mix", x, w, cos, sin)
        x, s3 = _ffn_fwd(t + "_ffn2", x, w["ffn2_norm"], *ffn_w("ffn2"), l)
        saved.append((s1, s2, s3))

    dx, d_final, loss, dxb = _loss_head(x, small["final_norm"][None, :], target)

    mixer_g = [None] * DEPTH
    small_g = [None] * DEPTH
    ffn_g = {"ffn1": None, "ffn2": None}
    for l in reversed(range(DEPTH)):
        w, (s1, s2, s3) = layers[l], saved[l]
        t = "l%d" % l
        sg = {}
        dx, dxb, sg["ffn2_norm"], ffn_g["ffn2"] = _ffn_bwd(t + "_ffn2", dx, dxb, s3, w["ffn2_norm"], *ffn_w("ffn2"), l,
                                                           ffn_g["ffn2"])
        dx, dxb, g = _mixer_bwd(t + "_mix", dx, dxb, s2, w, cos, sin)
        dx, dxb, sg["ffn1_norm"], ffn_g["ffn1"] = _ffn_bwd(t + "_ffn1", dx, dxb, s1, w["ffn1_norm"], *ffn_w("ffn1"), l,
                                                           ffn_g["ffn1"])
        mixer_g[l] = {
            "w_in": _full_to_col_blocks(_in_from_kernel(g["w_in"])),
            "mla_w_q_b": _full_to_col_blocks(_split_from_kernel(g["mla_w_q_b"], MLA_NOPE, MLA_ROPE)),
            "mla_w_kv_b": _full_to_col_blocks(_split_from_kernel(g["mla_w_kv_b"], MLA_NOPE, MLA_V)),
            "w_out": g["w_out"].astype(BF16).reshape(N_DEV, -1, g["w_out"].shape[1]),
        }
        for k in SMALL:
            if k in g:
                sg[k] = g[k]
        small_g[l] = {k: v.reshape(-1) for k, v in sg.items()}

    big = {k: jnp.stack([mixer_g[l][k] for l in range(DEPTH)], axis=1) for k in MIXER_BIG}
    for f in ("ffn1", "ffn2"):
        big[f + "_w_gate"], big[f + "_w_up"], big[f + "_w_down"] = ffn_g[f]
    sm = {k: jnp.stack([small_g[l][k] for l in range(DEPTH)]) for k in small_g[0]}
    sm["final_norm"] = d_final.reshape(-1)
    return loss, dx, big, sm


def _flat_rows(a):
    return a.reshape(-1, LANES)


def kernel(x, positions, ffn1_norm, ffn1_w_gate, ffn1_w_up, ffn1_w_down, mix_norm, w_in, mla_q_norm, mla_w_q_b, mla_kv_norm, mla_w_kv_b, swa_sinks, fox_forget_bias, w_out, ffn2_norm, ffn2_w_gate, ffn2_w_up, ffn2_w_down, final_norm, loss_target, m_ffn1_norm, m_ffn1_w_gate, m_ffn1_w_up, m_ffn1_w_down, m_mix_norm, m_w_in, m_mla_q_norm, m_mla_w_q_b, m_mla_kv_norm, m_mla_w_kv_b, m_swa_sinks, m_fox_forget_bias, m_w_out, m_ffn2_norm, m_ffn2_w_gate, m_ffn2_w_up, m_ffn2_w_down, m_final_norm, v_ffn1_norm, v_ffn1_w_gate, v_ffn1_w_up, v_ffn1_w_down, v_mix_norm, v_w_in, v_mla_q_norm, v_mla_w_q_b, v_mla_kv_norm, v_mla_w_kv_b, v_swa_sinks, v_fox_forget_bias, v_w_out, v_ffn2_norm, v_ffn2_w_gate, v_ffn2_w_up, v_ffn2_w_down, v_final_norm):
    given = dict(locals())
    weights = {k: given[k] for k in WEIGHTS}
    mom_m = {k: given["m_" + k] for k in WEIGHTS}
    mom_v = {k: given["v_" + k] for k in WEIGHTS}

    gathered = dict(zip(BIG, _all_gather([weights[k].astype(BF16) for k in BIG])))

    small = {k: weights[k] for k in SMALL}
    loss, grad_x, big_g, small_g = _local_step(x[0], positions[0][:, None], loss_target[0], gathered, small)

    received = dict(zip(BIG, _all_to_all([big_g[k] for k in BIG])))

    small_sizes = [weights[k].size for k in SMALL]
    sbuf = jnp.concatenate([small_g[k].reshape(-1) for k in SMALL] + [loss.reshape(-1)])
    pad = (-sbuf.size) % (8 * LANES)
    sbuf = jnp.concatenate([sbuf, jnp.zeros((pad,), F32)])
    stot = _all_reduce_small(_flat_rows(sbuf)).reshape(-1)

    grad_w, delta, new_m, new_v = {}, {}, {}, {}
    for k in BIG:
        shape = weights[k].shape
        two_d = (-1, shape[-1])
        g, d, m, v = _adamw_slots("adamw_" + k, received[k].reshape((N_DEV,) + (shape[0] * shape[1], shape[2])),
                                  weights[k].reshape(two_d), mom_m[k].reshape(two_d), mom_v[k].reshape(two_d))
        grad_w[k], delta[k], new_m[k], new_v[k] = (t.reshape(shape) for t in (g, d, m, v))
    off = 0
    for k, n in zip(SMALL, small_sizes):
        shape = weights[k].shape
        two_d = (-1, shape[-1]) if len(shape) > 1 else (1, -1)
        grad_w[k] = stot[off:off + n].reshape(shape)
        off += n
        d, m, v = _adamw("adamw_" + k, weights[k].reshape(two_d), grad_w[k].reshape(two_d),
                         mom_m[k].reshape(two_d), mom_v[k].reshape(two_d))
        delta[k], new_m[k], new_v[k] = d.reshape(shape), m.reshape(shape), v.reshape(shape)
    loss_total = stot[off]

    return (loss_total, grad_x[None], *[grad_w[k] for k in WEIGHTS], *[delta[k] for k in WEIGHTS],
            *[new_m[k] for k in WEIGHTS], *[new_v[k] for k in WEIGHTS])
```

```python
import jax
import jax.numpy as jnp
from jax import lax
from jax.experimental import pallas as pl
from jax.experimental.pallas import tpu as pltpu

F32 = jnp.float32
BF16 = jnp.bfloat16

N_DEV = 8
N_CHIP = 4
DEPTH = 2
RMS_EPS = 1e-6
ROPE_THETA = 10000.0
HEADS = 8
MLA_Q_LORA = 512
MLA_KV_LORA = 256
MLA_NOPE = 128
MLA_ROPE = 64
MLA_V = 128
SWA_KV_HEADS = 2
HEAD_DIM = 64
WINDOW = 128
IN_COLS = 3144
IN_COLS_PAD = 3200

ADAM_LR = 0.001
ADAM_B1 = 0.9
ADAM_B2 = 0.999
ADAM_EPS = 1e-08
ADAM_WD = 0.01
ADAM_STEP = 10

LANES = 128
NEG = -1e30
LOG2E = 1.4426950408889634
LN2 = 0.6931471805599453
VMEM_LIMIT_BYTES = 48 * 1024 * 1024

EW_ROWS = 256
MM_TM = 1024
MM_TN = 1024
MM_TK = 1024
ATT_T = 512

BIG = ("ffn1_w_gate", "ffn1_w_up", "ffn1_w_down", "w_in", "mla_w_q_b", "mla_w_kv_b", "w_out",
       "ffn2_w_gate", "ffn2_w_up", "ffn2_w_down")
SMALL = ("ffn1_norm", "mix_norm", "mla_q_norm", "mla_kv_norm", "swa_sinks", "fox_forget_bias", "ffn2_norm",
         "final_norm")
WEIGHTS = ("ffn1_norm", "ffn1_w_gate", "ffn1_w_up", "ffn1_w_down", "mix_norm", "w_in", "mla_q_norm", "mla_w_q_b",
           "mla_kv_norm", "mla_w_kv_b", "swa_sinks", "fox_forget_bias", "w_out", "ffn2_norm", "ffn2_w_gate",
           "ffn2_w_up", "ffn2_w_down", "final_norm")


def _params(**kw):
    return pltpu.CompilerParams(vmem_limit_bytes=VMEM_LIMIT_BYTES, **kw)


def _tile(n, want):
    if n <= want:
        return n
    t = (want // LANES) * LANES
    while n % t:
        t -= LANES
    return t


def _rows(n, want):
    if n <= want:
        return n
    t = (want // 8) * 8
    while n % t:
        t -= 8
    return t


def _ew(name, fn, ins, outs, tm=EW_ROWS):
    tok = None
    for a, kind in ins:
        if kind == "row":
            tok = a.shape[-2]
    tm = _rows(tok, tm)
    steps = tok // tm

    def spec(shape, kind):
        if kind == "row":
            ax = len(shape) - 2
            blk = tuple(tm if d == ax else s for d, s in enumerate(shape))
            return pl.BlockSpec(blk, lambda i, ax=ax, n=len(shape): tuple(i if d == ax else 0 for d in range(n)))
        return pl.BlockSpec(tuple(shape), lambda i, n=len(shape): (0,) * n)

    n_in = len(ins)
    kinds = [k for _, _, k in outs]

    def body(*refs):
        vals = fn(*[r[...] for r in refs[:n_in]])
        for r, v, kind in zip(refs[n_in:], vals, kinds):
            if kind == "row":
                r[...] = v.astype(r.dtype)
            else:
                @pl.when(pl.program_id(0) == 0)
                def _(r=r):
                    r[...] = jnp.zeros(r.shape, r.dtype)
                r[...] += v.astype(r.dtype)

    res = pl.pallas_call(
        body, name=name, grid=(steps,),
        in_specs=[spec(a.shape, k) for a, k in ins],
        out_specs=[spec(s, k) for s, _, k in outs],
        out_shape=[jax.ShapeDtypeStruct(tuple(s), d) for s, d, _ in outs],
        compiler_params=_params(dimension_semantics=("arbitrary",)),
    )(*[a for a, _ in ins])
    return res


def _rms_fwd(name, x, g):
    def fn(x, g):
        r = lax.rsqrt(jnp.mean(x * x, axis=-1, keepdims=True) + RMS_EPS)
        return [x * r * g]
    return _ew(name, fn, [(x, "row"), (g, "full")], [(x.shape, BF16, "row")])[0]


def _rms_bwd(name, dh, x, g, res=None, also_bf16=False):
    def fn(dh, x, g, *rest):
        dh = dh.astype(F32)
        r = lax.rsqrt(jnp.mean(x * x, axis=-1, keepdims=True) + RMS_EPS)
        xh = x * r
        dxh = dh * g
        dx = r * (dxh - xh * jnp.mean(dxh * xh, axis=-1, keepdims=True))
        if rest:
            dx = dx + rest[0]
        return [dx, jnp.sum(dh * xh, axis=0, keepdims=True)] + ([dx] if also_bf16 else [])
    ins = [(dh, "row"), (x, "row"), (g, "full")] + ([(res, "row")] if res is not None else [])
    outs = [(x.shape, F32, "row"), (g.shape, F32, "acc")] + ([(x.shape, BF16, "row")] if also_bf16 else [])
    return _ew(name, fn, ins, outs)


def _rope(name, x, xs, cos, sin, sign=1.0):
    def fn(x, xs, c, s):
        return [x * c + sign * (xs * s)]
    return _ew(name, fn, [(x, "row"), (xs, "row"), (cos, "row"), (sin, "row")], [(x.shape, F32, "row")])[0]


def _rope_tables(positions, inv_freq2):
    def fn(pos, f):
        ang = pos.astype(F32) * f
        return [jnp.cos(ang), jnp.sin(ang)]
    t = positions.shape[0]
    return _ew("rope_tables", fn, [(positions, "row"), (inv_freq2, "full")],
               [((t, 2 * 32), F32, "row"), ((t, 2 * 32), F32, "row")])


def _mm(name, lhs, rhs, terms, epi, out_dtypes, extras=(), ta=False, tb=False, tm=MM_TM, tn=MM_TN, tk=MM_TK):
    if ta:
        kdim, m = lhs[0].shape
    else:
        m, kdim = lhs[0].shape
    n = rhs[0].shape[0] if tb else rhs[0].shape[1]
    tm, tn, tk = _tile(m, tm), _tile(n, tn), _tile(kdim, tk)
    nk = kdim // tk
    n_acc = 1 + max(a for _, _, a in terms)
    nl, nr, ne = len(lhs), len(rhs), len(extras)
    dims = (((0 if ta else 1,), (1 if tb else 0,)), ((), ()))

    def body(*refs):
        l_refs, r_refs = refs[:nl], refs[nl:nl + nr]
        e_refs = refs[nl + nr:nl + nr + ne]
        o_refs = refs[nl + nr + ne:len(refs) - n_acc]
        accs = refs[len(refs) - n_acc:]
        k = pl.program_id(2)

        @pl.when(k == 0)
        def _():
            for acc in accs:
                acc[...] = jnp.zeros(acc.shape, F32)

        lv, rv = {}, {}
        for li, ri, ai in terms:
            if li not in lv:
                lv[li] = l_refs[li][...].astype(BF16)
            if ri not in rv:
                rv[ri] = r_refs[ri][...].astype(BF16)
            accs[ai][...] += lax.dot_general(lv[li], rv[ri], dims, preferred_element_type=F32)

        @pl.when(k == nk - 1)
        def _():
            outs = epi([acc[...] for acc in accs], [e[...] for e in e_refs])
            for o, v in zip(o_refs, outs):
                o[...] = v.astype(o.dtype)

    l_spec = pl.BlockSpec((tk, tm), lambda i, j, k: (k, i)) if ta else pl.BlockSpec((tm, tk), lambda i, j, k: (i, k))
    r_spec = pl.BlockSpec((tn, tk), lambda i, j, k: (j, k)) if tb else pl.BlockSpec((tk, tn), lambda i, j, k: (k, j))
    o_spec = pl.BlockSpec((tm, tn), lambda i, j, k: (i, j))
    return pl.pallas_call(
        body, name=name, grid=(m // tm, n // tn, nk),
        in_specs=[l_spec] * nl + [r_spec] * nr + [o_spec] * ne,
        out_specs=[o_spec] * len(out_dtypes),
        out_shape=[jax.ShapeDtypeStruct((m, n), d) for d in out_dtypes],
        scratch_shapes=[pltpu.VMEM((tm, tn), F32)] * n_acc,
        compiler_params=_params(dimension_semantics=("parallel", "parallel", "arbitrary")),
    )(*lhs, *rhs, *extras)


def _mm1(name, a, b, out_dtype=F32, scale=None, add=None, **kw):
    def epi(accs, ex):
        v = accs[0] if scale is None else accs[0] * scale
        return [v + ex[0] if ex else v]
    return _mm(name, [a], [b], [(0, 0, 0)], epi, [out_dtype], extras=[] if add is None else [add], **kw)[0]


def _sink_grad(name, sink3, lse, delta):
    def fn(sk, lse, dl):
        return [-jnp.sum(jnp.exp2(sk * LOG2E - lse) * dl, axis=1, keepdims=True)]
    return _ew(name, fn, [(sink3, "full"), (lse, "row"), (delta, "row")], [(sink3.shape, F32, "acc")], tm=512)[0]


def _resident(block, index_map):
    return pl.BlockSpec(block, index_map, pipeline_mode=pl.Buffered(1))


def _attn_fwd(name, q_t, k, v_t4, scale, group=1, cq_row=None, ck_col=None, sink=None, window=False, t=ATT_T):
    h_n, dq, tok = q_t.shape
    dv = v_t4.shape[2]
    t = min(t, tok)
    nq = tok // t
    bias = cq_row is not None
    has_sink = sink is not None

    def body(*refs):
        q_ref, k_ref, v_ref = refs[:3]
        pos = 3
        cq_ref = ck_ref = sink_ref = None
        if bias:
            cq_ref, ck_ref = refs[pos], refs[pos + 1]
            pos += 2
        if has_sink:
            sink_ref = refs[pos]
            pos += 1
        o_ref, lse_ref, m_s, l_s, acc_s = refs[pos:]
        h, qi = pl.program_id(0), pl.program_id(1)
        qs = (q_ref[...].astype(F32) * (scale * LOG2E)).astype(BF16)
        m_s[...] = jnp.full(m_s.shape, sink_ref[h] * LOG2E if has_sink else NEG, F32)
        l_s[...] = jnp.full(l_s.shape, 1.0 if has_sink else 0.0, F32)
        acc_s[...] = jnp.zeros(acc_s.shape, F32)
        c_ref = cq_ref[:, 0:1] if bias else None

        def scores(j, masked):
            rows = pl.ds(pl.multiple_of(j * t, t), t)
            s = jnp.dot(k_ref[rows, :], qs, preferred_element_type=F32)
            if bias:
                s = s - (ck_ref[rows, :] - c_ref)
            if masked:
                kpos = j * t + lax.broadcasted_iota(jnp.int32, s.shape, 0)
                qpos = qi * t + lax.broadcasted_iota(jnp.int32, s.shape, 1)
                mask = kpos <= qpos
                if window:
                    mask = mask & (kpos > qpos - WINDOW)
                s = jnp.where(mask, s, NEG)
            return s

        def step(tiles):
            ss = [scores(j, masked) for j, masked in tiles]
            m_prev = m_s[...]
            m_new = m_prev
            for s in ss:
                m_new = jnp.maximum(m_new, jnp.max(s, axis=0, keepdims=True))
            alpha = jnp.exp2(m_prev - m_new)
            l_new, acc = alpha * l_s[...], alpha * acc_s[...]
            for (j, _), s in zip(tiles, ss):
                p = jnp.exp2(s - m_new)
                l_new = l_new + jnp.sum(p, axis=0, keepdims=True)
                acc = acc + jnp.dot(v_ref[j], p.astype(BF16), preferred_element_type=F32)
            l_s[...], acc_s[...], m_s[...] = l_new, acc, m_new

        if window:
            pl.when(qi > 0)(lambda: step([(qi - 1, True), (qi, True)]))
            pl.when(qi == 0)(lambda: step([(qi, True)]))
        else:
            def below(i, carry):
                step([(2 * i, False), (2 * i + 1, False)])
                return carry
            lax.fori_loop(0, qi // 2, below, 0)
            pl.when(qi % 2 == 1)(lambda: step([(qi - 1, False), (qi, True)]))
            pl.when(qi % 2 == 0)(lambda: step([(qi, True)]))
        o_ref[...] = (acc_s[...] / l_s[...]).astype(o_ref.dtype)
        lse_ref[...] = m_s[...] + jnp.log(l_s[...]) * LOG2E

    nk = tok // t
    in_specs = [
        pl.BlockSpec((None, dq, t), lambda h, qi: (h, 0, qi)),
        _resident((None, tok, dq), lambda h, qi: (h // group, 0, 0)),
        _resident((None, nk, dv, t), lambda h, qi: (h // group, 0, 0, 0)),
    ]
    args = [q_t, k, v_t4]
    if bias:
        in_specs += [pl.BlockSpec((None, 1, t), lambda h, qi: (h, 0, qi)),
                     _resident((None, tok, 1), lambda h, qi: (h, 0, 0))]
        args += [cq_row, ck_col]
    if has_sink:
        in_specs.append(pl.BlockSpec(memory_space=pltpu.SMEM))
        args.append(sink)
    return pl.pallas_call(
        body, name=name, grid=(h_n, nq),
        in_specs=in_specs,
        out_specs=[pl.BlockSpec((None, dv, t), lambda h, qi: (h, 0, qi)),
                   pl.BlockSpec((None, 1, t), lambda h, qi: (h, 0, qi))],
        out_shape=[jax.ShapeDtypeStruct((h_n, dv, tok), BF16), jax.ShapeDtypeStruct((h_n, 1, tok), F32)],
        scratch_shapes=[pltpu.VMEM((1, t), F32), pltpu.VMEM((1, t), F32), pltpu.VMEM((dv, t), F32)],
        compiler_params=_params(dimension_semantics=("parallel", "parallel")),
    )(*args)


def _attn_bwd(name, q, q_t, k, k_t4, v_t4, do, do_t, lse, delta, scale, group=1, cq_col=None, ck_row4=None,
              window=False, t=ATT_T):
    h_n, tok, dq = q.shape
    dv = do.shape[2]
    t = min(t, tok)
    nq = tok // t
    bias = cq_col is not None

    def body(*refs):
        q_ref, qt_ref, k_ref, kt_ref, vt_ref, do_ref, dot_ref, lse_ref, dl_ref = refs[:9]
        pos = 9
        cq_ref = ck_ref = dc_ref = dr_ref = None
        if bias:
            cq_ref, ck_ref = refs[pos], refs[pos + 1]
            pos += 2
        dq_ref, dk_ref, dv_ref = refs[pos:pos + 3]
        if bias:
            dc_ref, dr_ref = refs[pos + 3], refs[pos + 4]
        qi = pl.program_id(1)

        @pl.when(qi == 0)
        def _():
            dk_ref[...] = jnp.zeros(dk_ref.shape, F32)
            dv_ref[...] = jnp.zeros(dv_ref.shape, F32)
            if bias:
                dc_ref[...] = jnp.zeros(dc_ref.shape, F32)

        qs = (q_ref[...].astype(F32) * (scale * LOG2E)).astype(BF16)
        dq_ref[...] = jnp.zeros(dq_ref.shape, F32)
        if bias:
            dr_ref[...] = jnp.zeros(dr_ref.shape, F32)
        c_ref = cq_ref[0:1, :] if bias else None

        def step(j, masked):
            s = jnp.dot(qs, kt_ref[j], preferred_element_type=F32)
            if bias:
                s = s - (ck_ref[j] - c_ref)
            p = jnp.exp2(s - lse_ref[...])
            if masked:
                qpos = qi * t + lax.broadcasted_iota(jnp.int32, s.shape, 0)
                kpos = j * t + lax.broadcasted_iota(jnp.int32, s.shape, 1)
                mask = kpos <= qpos
                if window:
                    mask = mask & (kpos > qpos - WINDOW)
                p = jnp.where(mask, p, 0.0)
            pb = p.astype(BF16)
            dv_ref[j] += jnp.dot(dot_ref[...], pb, preferred_element_type=F32)
            dp = jnp.dot(do_ref[...], vt_ref[j], preferred_element_type=F32)
            ds = p * (dp - dl_ref[...])
            dsb = ds.astype(BF16)
            rows = pl.ds(pl.multiple_of(j * t, t), t)
            dq_ref[...] += jnp.dot(dsb, k_ref[rows, :], preferred_element_type=F32)
            dk_ref[j] += scale * jnp.dot(qt_ref[...], dsb, preferred_element_type=F32)
            if bias:
                dc_ref[j] += jnp.sum(ds, axis=0, keepdims=True)
                dr_ref[...] += jnp.sum(ds, axis=1, keepdims=True)

        if window:
            pl.when(qi > 0)(lambda: step(qi - 1, True))
        else:
            def below(j, carry):
                step(j, False)
                return carry
            lax.fori_loop(0, qi, below, 0)
        step(qi, True)
        dq_ref[...] = dq_ref[...] * scale

    nk = nq

    def q_tile(shape_tail):
        return pl.BlockSpec((None, t) + shape_tail, lambda h, qi: (h, qi, 0))

    in_specs = [
        q_tile((dq,)),
        pl.BlockSpec((None, dq, t), lambda h, qi: (h, 0, qi)),
        _resident((None, tok, dq), lambda h, qi: (h // group, 0, 0)),
        _resident((None, nk, dq, t), lambda h, qi: (h // group, 0, 0, 0)),
        _resident((None, nk, dv, t), lambda h, qi: (h // group, 0, 0, 0)),
        q_tile((dv,)),
        pl.BlockSpec((None, dv, t), lambda h, qi: (h, 0, qi)),
        q_tile((1,)),
        q_tile((1,)),
    ]
    args = [q, q_t, k, k_t4, v_t4, do, do_t, lse, delta]
    out_specs = [q_tile((dq,)),
                 pl.BlockSpec((None, nk, dq, t), lambda h, qi: (h, 0, 0, 0)),
                 pl.BlockSpec((None, nk, dv, t), lambda h, qi: (h, 0, 0, 0))]
    out_shape = [jax.ShapeDtypeStruct((h_n, tok, dq), F32), jax.ShapeDtypeStruct((h_n, nk, dq, t), F32),
                 jax.ShapeDtypeStruct((h_n, nk, dv, t), F32)]
    if bias:
        in_specs += [q_tile((1,)), _resident((None, nk, 1, t), lambda h, qi: (h, 0, 0, 0))]
        args += [cq_col, ck_row4]
        out_specs += [pl.BlockSpec((None, nk, 1, t), lambda h, qi: (h, 0, 0, 0)), q_tile((1,))]
        out_shape += [jax.ShapeDtypeStruct((h_n, nk, 1, t), F32), jax.ShapeDtypeStruct((h_n, tok, 1), F32)]
    return pl.pallas_call(
        body, name=name, grid=(h_n, nq),
        in_specs=in_specs, out_specs=out_specs, out_shape=out_shape,
        compiler_params=_params(dimension_semantics=("parallel", "arbitrary")),
    )(*args)


def _delta_t(name, do_t, o_t, tl=1024):
    h_n, dv, tok = do_t.shape
    tl = min(tl, tok)

    def body(do_ref, o_ref, out_ref):
        out_ref[...] = jnp.sum(do_ref[...].astype(F32) * o_ref[...].astype(F32), axis=1, keepdims=True)

    spec = pl.BlockSpec((h_n, dv, tl), lambda i: (0, 0, i))
    return pl.pallas_call(
        body, name=name, grid=(tok // tl,), in_specs=[spec, spec],
        out_specs=pl.BlockSpec((h_n, 1, tl), lambda i: (0, 0, i)),
        out_shape=jax.ShapeDtypeStruct((h_n, 1, tok), F32),
        compiler_params=_params(dimension_semantics=("parallel",)),
    )(do_t, o_t)


def _log_sigmoid(z):
    return jnp.minimum(z, 0.0) - jnp.log(1.0 + jnp.exp(-jnp.abs(z)))


def _gate_fwd(zt, bias):
    tok = zt.shape[1]

    def body(z_ref, b_ref, c_ref):
        x = _log_sigmoid(z_ref[...] + b_ref[...])
        lane = lax.broadcasted_iota(jnp.int32, x.shape, 1)
        k = 1
        while k < tok:
            x = x + jnp.where(lane >= k, pltpu.roll(x, k, axis=1), 0.0)
            k *= 2
        c_ref[...] = x * LOG2E

    return pl.pallas_call(body, name="fox_gate_fwd", out_shape=jax.ShapeDtypeStruct(zt.shape, F32),
                          compiler_params=_params())(zt, bias)


def _gate_bwd(d_rows, d_cols, zt, bias):
    tok = zt.shape[1]

    def body(dr_ref, dc_ref, z_ref, b_ref, dz_ref, db_ref):
        x = dr_ref[...] - dc_ref[...]
        lane = lax.broadcasted_iota(jnp.int32, x.shape, 1)
        k = 1
        while k < tok:
            x = x + jnp.where(lane < tok - k, pltpu.roll(x, tok - k, axis=1), 0.0)
            k *= 2
        dz = x / (1.0 + jnp.exp(z_ref[...] + b_ref[...]))
        dz_ref[...] = dz
        db_ref[...] = jnp.sum(dz, axis=1, keepdims=True)

    return pl.pallas_call(body, name="fox_gate_bwd",
                          out_shape=[jax.ShapeDtypeStruct(zt.shape, F32), jax.ShapeDtypeStruct(bias.shape, F32)],
                          compiler_params=_params())(d_rows, d_cols, zt, bias)


_NT = (((1,), (1,)), ((), ()))
_TN = (((0,), (0,)), ((), ()))


def _ffn_gate_up(name, h, wg, wu, l, tm=512):
    tok, d = h.shape
    nb, n = wg.shape[0], wg.shape[3]
    tm = min(tm, tok)

    def body(h_ref, wg_ref, wu_ref, u_ref, v_ref, a_ref):
        hv = h_ref[...]
        u = jnp.dot(hv, wg_ref[...], preferred_element_type=F32)
        v = jnp.dot(hv, wu_ref[...], preferred_element_type=F32)
        u_ref[...] = u.astype(BF16)
        v_ref[...] = v.astype(BF16)
        a_ref[...] = (u * jax.nn.sigmoid(u) * v).astype(BF16)

    w_spec = pl.BlockSpec((None, None, d, n), lambda j, i: (j, l, 0, 0))
    o_spec = pl.BlockSpec((None, tm, n), lambda j, i: (j, i, 0))
    return pl.pallas_call(
        body, name=name, grid=(nb, tok // tm),
        in_specs=[pl.BlockSpec((tm, d), lambda j, i: (i, 0)), w_spec, w_spec],
        out_specs=[o_spec] * 3, out_shape=[jax.ShapeDtypeStruct((nb, tok, n), BF16)] * 3,
        compiler_params=_params(dimension_semantics=("parallel", "parallel")),
    )(h, wg, wu)


def _ffn_down(name, a, wd, x, l, tm=1024, tn=1024):
    nb, tok, n = a.shape
    d = wd.shape[3]
    tm, tn = min(tm, tok), min(tn, d)

    def body(a_ref, wd_ref, x_ref, y_ref, acc):
        j = pl.program_id(2)

        @pl.when(j == 0)
        def _():
            acc[...] = jnp.zeros(acc.shape, F32)
        acc[...] += jnp.dot(a_ref[...], wd_ref[...], preferred_element_type=F32)

        @pl.when(j == nb - 1)
        def _():
            y_ref[...] = x_ref[...] + 0.5 * acc[...]

    return pl.pallas_call(
        body, name=name, grid=(tok // tm, d // tn, nb),
        in_specs=[pl.BlockSpec((None, tm, n), lambda i, c, j: (j, i, 0)),
                  pl.BlockSpec((None, None, n, tn), lambda i, c, j: (j, l, 0, c)),
                  pl.BlockSpec((tm, tn), lambda i, c, j: (i, c))],
        out_specs=pl.BlockSpec((tm, tn), lambda i, c, j: (i, c)),
        out_shape=jax.ShapeDtypeStruct((tok, d), F32),
        scratch_shapes=[pltpu.VMEM((tm, tn), F32)],
        compiler_params=_params(dimension_semantics=("parallel", "parallel", "arbitrary")),
    )(a, wd, x)


def _ffn_d_act(name, dy, u, v, wd, l, tm=512):
    nb, tok, n = u.shape
    d = dy.shape[1]
    tm = min(tm, tok)

    def body(dy_ref, u_ref, v_ref, wd_ref, du_ref, dv_ref):
        da = 0.5 * lax.dot_general(dy_ref[...], wd_ref[...], _NT, preferred_element_type=F32)
        uv, vv = u_ref[...].astype(F32), v_ref[...].astype(F32)
        sg = jax.nn.sigmoid(uv)
        du_ref[...] = (da * vv * (sg * (1.0 + uv * (1.0 - sg)))).astype(BF16)
        dv_ref[...] = (da * (uv * sg)).astype(BF16)

    t_spec = pl.BlockSpec((None, tm, n), lambda j, i: (j, i, 0))
    return pl.pallas_call(
        body, name=name, grid=(nb, tok // tm),
        in_specs=[pl.BlockSpec((tm, d), lambda j, i: (i, 0)), t_spec, t_spec,
                  pl.BlockSpec((None, None, n, d), lambda j, i: (j, l, 0, 0))],
        out_specs=[t_spec] * 2, out_shape=[jax.ShapeDtypeStruct((nb, tok, n), BF16)] * 2,
        compiler_params=_params(dimension_semantics=("parallel", "parallel")),
    )(dy, u, v, wd)


def _ffn_d_h(name, du, dv, wg, wu, l, tm=1024):
    nb, tok, n = du.shape
    d = wg.shape[2]
    tm = min(tm, tok)

    def body(du_ref, dv_ref, wg_ref, wu_ref, dh_ref, acc):
        j = pl.program_id(1)

        @pl.when(j == 0)
        def _():
            acc[...] = jnp.zeros(acc.shape, F32)
        acc[...] += (lax.dot_general(du_ref[...], wg_ref[...], _NT, preferred_element_type=F32)
                     + lax.dot_general(dv_ref[...], wu_ref[...], _NT, preferred_element_type=F32))

        @pl.when(j == nb - 1)
        def _():
            dh_ref[...] = acc[...].astype(BF16)

    t_spec = pl.BlockSpec((None, tm, n), lambda i, j: (j, i, 0))
    w_spec = pl.BlockSpec((None, None, d, n), lambda i, j: (j, l, 0, 0))
    return pl.pallas_call(
        body, name=name, grid=(tok // tm, nb),
        in_specs=[t_spec, t_spec, w_spec, w_spec],
        out_specs=pl.BlockSpec((tm, d), lambda i, j: (i, 0)),
        out_shape=jax.ShapeDtypeStruct((tok, d), BF16),
        scratch_shapes=[pltpu.VMEM((tm, d), F32)],
        compiler_params=_params(dimension_semantics=("parallel", "arbitrary")),
    )(du, dv, wg, wu)


def _ffn_wgrad_in(name, h, du, dv, l, like, prev=None, tk=512, td=1024):
    tok, d = h.shape
    nb, _, n = du.shape
    tk, td = min(tk, tok), min(td, d)
    nk = tok // tk
    n_in = 3

    def body(*refs):
        h_ref, du_ref, dv_ref = refs[:3]
        og_ref, ou_ref, accg, accu = refs[len(refs) - 4:]
        k = pl.program_id(2)

        @pl.when(k == 0)
        def _():
            accg[...] = jnp.zeros(accg.shape, F32)
            accu[...] = jnp.zeros(accu.shape, F32)
        hv = h_ref[...]
        accg[...] += lax.dot_general(hv, du_ref[...], _TN, preferred_element_type=F32)
        accu[...] += lax.dot_general(hv, dv_ref[...], _TN, preferred_element_type=F32)

        @pl.when(k == nk - 1)
        def _():
            og_ref[...] = accg[...].astype(BF16)
            ou_ref[...] = accu[...].astype(BF16)

    t_spec = pl.BlockSpec((None, tk, n), lambda j, c, k: (j, k, 0))
    o_spec = pl.BlockSpec((None, None, td, n), lambda j, c, k: (j, l, c, 0))
    in_specs = [pl.BlockSpec((tk, td), lambda j, c, k: (k, c)), t_spec, t_spec]
    args = [h, du, dv]
    aliases = {}
    if prev is not None:
        in_specs += [pl.BlockSpec(memory_space=pl.ANY)] * 2
        args += list(prev)
        aliases = {n_in: 0, n_in + 1: 1}
    return pl.pallas_call(
        body, name=name, grid=(nb, d // td, nk),
        in_specs=in_specs, out_specs=[o_spec] * 2,
        out_shape=[jax.ShapeDtypeStruct(like.shape, BF16)] * 2,
        scratch_shapes=[pltpu.VMEM((td, n), F32)] * 2,
        input_output_aliases=aliases,
        compiler_params=_params(dimension_semantics=("parallel", "parallel", "arbitrary")),
    )(*args)


def _ffn_wgrad_out(name, a, dy, l, like, prev=None, tk=512):
    nb, tok, n = a.shape
    d = dy.shape[1]
    tk = min(tk, tok)
    nk = tok // tk

    def body(*refs):
        a_ref, dy_ref = refs[:2]
        o_ref, acc = refs[len(refs) - 2:]
        k = pl.program_id(1)

        @pl.when(k == 0)
        def _():
            acc[...] = jnp.zeros(acc.shape, F32)
        acc[...] += lax.dot_general(a_ref[...], dy_ref[...], _TN, preferred_element_type=F32)

        @pl.when(k == nk - 1)
        def _():
            o_ref[...] = (0.5 * acc[...]).astype(BF16)

    in_specs = [pl.BlockSpec((None, tk, n), lambda j, k: (j, k, 0)), pl.BlockSpec((tk, d), lambda j, k: (k, 0))]
    args = [a, dy]
    aliases = {}
    if prev is not None:
        in_specs.append(pl.BlockSpec(memory_space=pl.ANY))
        args.append(prev)
        aliases = {2: 0}
    return pl.pallas_call(
        body, name=name, grid=(nb, nk),
        in_specs=in_specs, out_specs=pl.BlockSpec((None, None, n, d), lambda j, k: (j, l, 0, 0)),
        out_shape=jax.ShapeDtypeStruct(like.shape, BF16),
        scratch_shapes=[pltpu.VMEM((n, d), F32)],
        input_output_aliases=aliases,
        compiler_params=_params(dimension_semantics=("parallel", "arbitrary")),
    )(*args)


def _ffn_fwd(tag, x, g, wg, wu, wd, l):
    h = _rms_fwd(tag + "_norm", x, g)
    u, v, a = _ffn_gate_up(tag + "_gate_up", h, wg, wu, l)
    y = _ffn_down(tag + "_down", a, wd, x, l)
    return y, (x, h, u, v, a)


def _ffn_bwd(tag, dy, dyb, saved, g, wg, wu, wd, l, prev):
    x, h, u, v, a = saved
    du, dv = _ffn_d_act(tag + "_d_act", dyb, u, v, wd, l)
    d_wd = _ffn_wgrad_out(tag + "_d_wd", a, dyb, l, wd, None if prev is None else prev[2])
    d_wg, d_wu = _ffn_wgrad_in(tag + "_d_wgu", h, du, dv, l, wg, None if prev is None else prev[:2])
    dh = _ffn_d_h(tag + "_d_h", du, dv, wg, wu, l)
    dx, dg, dxb = _rms_bwd(tag + "_d_norm", dh, x, g, res=dy, also_bf16=True)
    return dx, dxb, dg, (d_wg, d_wu, d_wd)


def _swap_halves(x):
    t, w = x.shape
    return x.reshape(t, w // HEAD_DIM, 2, HEAD_DIM // 2)[:, :, ::-1, :].reshape(t, w)


def _unheads(x):
    n, t, d = x.shape
    return x.transpose(1, 0, 2).reshape(t, n * d)


def _nat(x3):
    return x3.transpose(1, 0, 2).astype(BF16)


def _tr(x3):
    return x3.transpose(1, 2, 0).astype(BF16)


def _tr4(x3, t):
    tok, n, d = x3.shape
    return x3.reshape(tok // t, t, n, d).transpose(2, 0, 3, 1).astype(BF16)


def _from_t(x_t):
    n, d, tok = x_t.shape
    return x_t.transpose(2, 0, 1).reshape(tok, n * d)


def _from_t4(x4):
    n, nk, d, t = x4.shape
    return x4.transpose(1, 3, 0, 2).reshape(nk * t, n, d)


def _col(row):
    return row.transpose(0, 2, 1)


def _mixer_fwd(tag, x, w, cos, sin):
    tok = x.shape[0]
    h2 = _rms_fwd(tag + "_norm", x, w["mix_norm"])
    p = _mm1(tag + "_in", h2, w["w_in"], tn=640)
    c_q, c_kv = p[:, :512], p[:, 512:768]
    q_s, k_s, v_s = p[:, 768:1280], p[:, 1280:1408], p[:, 1408:1536]
    q_f, k_f, v_f = p[:, 1536:2048], p[:, 2048:2560], p[:, 2560:3072]
    k_rope, f_logit = p[:, 3072:3136], p[:, 3136:3144]

    qn = _rms_fwd(tag + "_q_norm", c_q, w["mla_q_norm"])
    qm = _mm1(tag + "_q_b", qn, w["mla_w_q_b"])
    kvn = _rms_fwd(tag + "_kv_norm", c_kv, w["mla_kv_norm"])
    kvm = _mm1(tag + "_kv_b", kvn, w["mla_w_kv_b"])

    rin = jnp.concatenate([qm[:, 1024:], q_s, k_s, k_rope, jnp.zeros((tok, 64), F32)], axis=1)
    rout = _rope(tag + "_rope", rin, _swap_halves(rin), cos, sin)
    q_pe, q_sr, k_sr, k_pe = rout[:, :512], rout[:, 512:1024], rout[:, 1024:1152], rout[:, 1152:1216]

    t = min(ATT_T, tok)
    q_m = jnp.concatenate([qm[:, :1024].reshape(tok, HEADS, MLA_NOPE), q_pe.reshape(tok, HEADS, MLA_ROPE)], axis=-1)
    k_m = jnp.concatenate([kvm[:, :1024].reshape(tok, HEADS, MLA_NOPE),
                           jnp.broadcast_to(k_pe[:, None, :], (tok, HEADS, MLA_ROPE))], axis=-1)
    v_m = kvm[:, 1024:].reshape(tok, HEADS, MLA_V)
    mla = dict(q=_nat(q_m), q_t=_tr(q_m), k=_nat(k_m), k_t4=_tr4(k_m, t), v_t4=_tr4(v_m, t))
    mla["o_t"], mla["lse"] = _attn_fwd(tag + "_mla_fwd", mla["q_t"], mla["k"], mla["v_t4"],
                                       (MLA_NOPE + MLA_ROPE) ** -0.5, t=t)

    q_s3, k_s3 = q_sr.reshape(tok, HEADS, HEAD_DIM), k_sr.reshape(tok, SWA_KV_HEADS, HEAD_DIM)
    v_s3 = v_s.reshape(tok, SWA_KV_HEADS, HEAD_DIM)
    swa = dict(q=_nat(q_s3), q_t=_tr(q_s3), k=_nat(k_s3), k_t4=_tr4(k_s3, t), v_t4=_tr4(v_s3, t))
    swa["o_t"], swa["lse"] = _attn_fwd(tag + "_swa_fwd", swa["q_t"], swa["k"], swa["v_t4"], HEAD_DIM ** -0.5,
                                       group=HEADS // SWA_KV_HEADS, sink=w["swa_sinks"], window=True, t=t)

    zt = f_logit.T
    c = _gate_fwd(zt, w["fox_forget_bias"].reshape(HEADS, 1))
    q_f3, k_f3, v_f3 = (a.reshape(tok, HEADS, HEAD_DIM) for a in (q_f, k_f, v_f))
    fox = dict(q=_nat(q_f3), q_t=_tr(q_f3), k=_nat(k_f3), k_t4=_tr4(k_f3, t), v_t4=_tr4(v_f3, t),
               c_col=c[:, :, None], c_row4=c.reshape(HEADS, tok // t, 1, t))
    fox["o_t"], fox["lse"] = _attn_fwd(tag + "_fox_fwd", fox["q_t"], fox["k"], fox["v_t4"], HEAD_DIM ** -0.5,
                                       cq_row=c[:, None, :], ck_col=fox["c_col"], t=t)

    mixed = jnp.concatenate([_from_t(mla["o_t"]), _from_t(swa["o_t"]), _from_t(fox["o_t"])], axis=1)
    y = _mm1(tag + "_out", mixed, w["w_out"], add=x)
    saved = dict(x=x, h2=h2, c_q=c_q, c_kv=c_kv, qn=qn, kvn=kvn, zt=zt, mixed=mixed, mla=mla, swa=swa, fox=fox)
    return y, saved


def _mixer_bwd(tag, dy, dyb, s, w, cos, sin):
    tok = dy.shape[0]
    g = {}
    dmixed = _mm1(tag + "_d_mixed", dyb, w["w_out"], out_dtype=BF16, tb=True)
    g["w_out"] = _mm1(tag + "_d_wout", s["mixed"], dyb, ta=True)
    t = min(ATT_T, tok)

    def attn_bwd(name, a, d_out, scale, **kw):
        do3 = d_out.reshape(tok, HEADS, -1)
        do_t = _tr(do3)
        dl = _col(_delta_t(name + "_delta", do_t, a["o_t"]))
        return dl, _attn_bwd(name + "_bwd", a["q"], a["q_t"], a["k"], a["k_t4"], a["v_t4"], _nat(do3), do_t,
                             _col(a["lse"]), dl, scale, t=t, **kw)

    _, (dq_m, dk_m4, dv_m4) = attn_bwd(tag + "_mla", s["mla"], dmixed[:, :1024], (MLA_NOPE + MLA_ROPE) ** -0.5)
    dl, (dq_sh, dk_s4, dv_s4) = attn_bwd(tag + "_swa", s["swa"], dmixed[:, 1024:1536], HEAD_DIM ** -0.5,
                                         group=HEADS // SWA_KV_HEADS, window=True)
    g["swa_sinks"] = _sink_grad(tag + "_d_sink", w["swa_sinks"].reshape(HEADS, 1, 1), _col(s["swa"]["lse"]),
                                dl).reshape(HEADS)
    _, (dq_fh, dk_f4, dv_f4, d_cols, d_rows) = attn_bwd(tag + "_fox", s["fox"], dmixed[:, 1536:], HEAD_DIM ** -0.5,
                                                        cq_col=s["fox"]["c_col"], ck_row4=s["fox"]["c_row4"])
    dzt, dbias = _gate_bwd(d_rows[:, :, 0], d_cols.reshape(HEADS, tok), s["zt"],
                           w["fox_forget_bias"].reshape(HEADS, 1))
    g["fox_forget_bias"] = dbias.reshape(HEADS)

    grp = HEADS // SWA_KV_HEADS
    dq_mt = dq_m.transpose(1, 0, 2)
    dk_mt = _from_t4(dk_m4)
    d_qpe = dq_mt[:, :, MLA_NOPE:].reshape(tok, HEADS * MLA_ROPE)
    d_kpe_heads = dk_mt[:, :, MLA_NOPE:].reshape(tok, HEADS * MLA_ROPE)
    d_qs = _unheads(dq_sh)
    d_ks_heads = _from_t4(dk_s4).reshape(tok, SWA_KV_HEADS, grp, HEAD_DIM).transpose(0, 2, 1, 3).reshape(tok, grp * 128)
    d_vs_heads = _from_t4(dv_s4).reshape(tok, SWA_KV_HEADS, grp, HEAD_DIM).transpose(0, 2, 1, 3).reshape(tok, grp * 128)

    def fold(d_kpe_h, d_ks_h, d_vs_h):
        kpe = d_kpe_h[:, 0:64]
        for i in range(1, HEADS):
            kpe = kpe + d_kpe_h[:, 64 * i:64 * (i + 1)]
        ks, vs = d_ks_h[:, 0:128], d_vs_h[:, 0:128]
        for i in range(1, grp):
            ks = ks + d_ks_h[:, 128 * i:128 * (i + 1)]
            vs = vs + d_vs_h[:, 128 * i:128 * (i + 1)]
        return [jnp.concatenate([kpe, jnp.zeros_like(kpe)], axis=1), ks, vs]
    d_kpe2, d_ksr, d_vs = _ew(tag + "_fold_heads", fold,
                              [(d_kpe_heads, "row"), (d_ks_heads, "row"), (d_vs_heads, "row")],
                              [((tok, 128), F32, "row"), ((tok, 128), F32, "row"), ((tok, 128), F32, "row")])

    rin = jnp.concatenate([d_qpe, d_qs, d_ksr, d_kpe2], axis=1)
    rout = _rope(tag + "_d_rope", rin, _swap_halves(rin), cos, sin, sign=-1.0)
    d_qpe_pre, d_qs_pre, d_ks_pre, d_krope = rout[:, :512], rout[:, 512:1024], rout[:, 1024:1152], rout[:, 1152:1216]

    d_qm = jnp.concatenate([dq_mt[:, :, :MLA_NOPE].reshape(tok, HEADS * MLA_NOPE), d_qpe_pre], axis=1)
    d_kvm = jnp.concatenate([dk_mt[:, :, :MLA_NOPE].reshape(tok, HEADS * MLA_NOPE),
                             _from_t4(dv_m4).reshape(tok, HEADS * MLA_V)], axis=1)
    g["mla_w_q_b"] = _mm1(tag + "_d_wqb", s["qn"], d_qm, ta=True)
    d_qn = _mm1(tag + "_d_qn", d_qm, w["mla_w_q_b"], tb=True)
    d_cq, g["mla_q_norm"] = _rms_bwd(tag + "_d_q_norm", d_qn, s["c_q"], w["mla_q_norm"])
    g["mla_w_kv_b"] = _mm1(tag + "_d_wkvb", s["kvn"], d_kvm, ta=True)
    d_kvn = _mm1(tag + "_d_kvn", d_kvm, w["mla_w_kv_b"], tb=True)
    d_ckv, g["mla_kv_norm"] = _rms_bwd(tag + "_d_kv_norm", d_kvn, s["c_kv"], w["mla_kv_norm"])

    dp = jnp.concatenate([d_cq, d_ckv, d_qs_pre, d_ks_pre, d_vs, _unheads(dq_fh),
                          _from_t4(dk_f4).reshape(tok, HEADS * HEAD_DIM), _from_t4(dv_f4).reshape(tok, HEADS * HEAD_DIM),
                          d_krope, dzt.T, jnp.zeros((tok, IN_COLS_PAD - IN_COLS), F32)], axis=1).astype(BF16)
    g["w_in"] = _mm1(tag + "_d_win", s["h2"], dp, ta=True, tn=640)
    dh2 = _mm1(tag + "_d_h2", dp, w["w_in"], tb=True, tk=640)
    dx, g["mix_norm"], dxb = _rms_bwd(tag + "_d_norm", dh2, s["x"], w["mix_norm"], res=dy, also_bf16=True)
    return dx, dxb, g


def _loss_head(x, g, target):
    d = x.shape[1]

    def fn(x, g, tgt):
        r = lax.rsqrt(jnp.mean(x * x, axis=-1, keepdims=True) + RMS_EPS)
        xh = x * r
        err = xh * g - tgt
        loss = 0.5 * jnp.sum(jnp.sum(err * err, axis=-1, keepdims=True), axis=0, keepdims=True) / d
        dy = err / d
        dxh = dy * g
        dx = r * (dxh - xh * jnp.mean(dxh * xh, axis=-1, keepdims=True))
        return [dx, jnp.sum(dy * xh, axis=0, keepdims=True), loss, dx]
    return _ew("loss_head", fn, [(x, "row"), (g, "full"), (target, "row")],
               [(x.shape, F32, "row"), (g.shape, F32, "acc"), ((1, 1), F32, "acc"), (x.shape, BF16, "row")])


def _adam_update(w, g, m, v):
    m = ADAM_B1 * m + (1.0 - ADAM_B1) * g
    v = ADAM_B2 * v + (1.0 - ADAM_B2) * (g * g)
    m_hat = m / (1.0 - ADAM_B1 ** ADAM_STEP)
    v_hat = v / (1.0 - ADAM_B2 ** ADAM_STEP)
    return [-ADAM_LR * (m_hat / (jnp.sqrt(v_hat) + ADAM_EPS) + ADAM_WD * w), m, v]


def _adamw(name, w, g, m, v):
    return _ew(name, _adam_update, [(w, "row"), (g, "row"), (m, "row"), (v, "row")], [(w.shape, F32, "row")] * 3)


def _adamw_slots(name, r, w, m, v):
    def fn(r, w, m, v):
        g = r[0].astype(F32)
        for i in range(1, r.shape[0]):
            g = g + r[i].astype(F32)
        return [g] + _adam_update(w, g, m, v)
    return _ew(name, fn, [(r, "row"), (w, "row"), (m, "row"), (v, "row")], [(w.shape, F32, "row")] * 4)


def _pair_sum(name, g, from_sibling, core):
    _, rows, cols = g.shape
    tm = _rows(rows, 512)

    def body(c_ref, g_ref, s_ref, o_ref):
        o_ref[...] = (g_ref[...].astype(F32) + s_ref[...].astype(F32)).astype(BF16)

    return pl.pallas_call(
        body, name=name,
        grid_spec=pltpu.PrefetchScalarGridSpec(
            num_scalar_prefetch=1, grid=(N_CHIP, rows // tm),
            in_specs=[pl.BlockSpec((None, tm, cols), lambda ch, i, c: (2 * ch + c[0], i, 0)),
                      pl.BlockSpec((None, tm, cols), lambda ch, i, c: (ch, i, 0))],
            out_specs=pl.BlockSpec((None, tm, cols), lambda ch, i, c: (ch, i, 0))),
        out_shape=jax.ShapeDtypeStruct((N_CHIP, rows, cols), BF16),
        compiler_params=_params(dimension_semantics=("parallel", "parallel")),
    )(core, g, from_sibling)


def _coords(dev):
    return (dev // 4, (dev // 2) % 2, dev % 2)


def _all_gather(shards):
    n = len(shards)

    def body(*refs):
        x_refs, out_refs = refs[:n], refs[n:2 * n]
        send_sems, recv_sems, local_sems = refs[2 * n:]
        x, y, c = lax.axis_index("x"), lax.axis_index("y"), lax.axis_index("c")
        me, sibling = (x, y, c), (x, y, 1 - c)
        chips = [(1 - x, y), (x, 1 - y), (1 - x, 1 - y)]

        def slot(a, px, py, pc):
            return out_refs[a].at[4 * px + 2 * py + pc]

        def copy(a, k, block, to, src=None):
            return pltpu.make_async_remote_copy(
                src_ref=slot(a, *block) if src is None else src, dst_ref=slot(a, *block),
                send_sem=send_sems.at[a, k], recv_sem=recv_sems.at[a, k],
                device_id=to, device_id_type=pl.DeviceIdType.MESH)

        mine = [pltpu.make_async_copy(x_refs[a], slot(a, *me), local_sems.at[a]) for a in range(n)]
        first, passed = [], []
        for a in range(n):
            mine[a].start()
            first.append(copy(a, 0, me, sibling, src=x_refs[a]))
            first += [copy(a, 1 + j, me, (*chip, c), src=x_refs[a]) for j, chip in enumerate(chips)]
        for cp in first:
            cp.start()
        for a in range(n):
            for j, chip in enumerate(chips):
                copy(a, 1 + j, (*chip, c), me).wait_recv()
                passed.append(copy(a, 4 + j, (*chip, c), sibling))
                passed[-1].start()
        for a in range(n):
            copy(a, 0, sibling, me).wait_recv()
            for j, chip in enumerate(chips):
                copy(a, 4 + j, (*chip, 1 - c), me).wait_recv()
        for cp in first + passed:
            cp.wait_send()
        for a in range(n):
            mine[a].wait()

    return pl.pallas_call(
        body, name="all_gather_weights",
        out_shape=[jax.ShapeDtypeStruct((N_DEV,) + s.shape, s.dtype) for s in shards],
        in_specs=[pl.BlockSpec(memory_space=pl.ANY)] * n,
        out_specs=[pl.BlockSpec(memory_space=pl.ANY)] * n,
        scratch_shapes=[pltpu.SemaphoreType.DMA((n, 7)), pltpu.SemaphoreType.DMA((n, 7)),
                        pltpu.SemaphoreType.DMA((n,))],
    )(*shards)


def _pair_exchange(blocks):
    n = len(blocks)

    def body(*refs):
        g_refs, out_refs = refs[:n], refs[n:2 * n]
        send_sems, recv_sems = refs[2 * n:]
        x, y, c = lax.axis_index("x"), lax.axis_index("y"), lax.axis_index("c")
        copies = [pltpu.make_async_remote_copy(
            src_ref=g_refs[a].at[2 * chip + (1 - c)], dst_ref=out_refs[a].at[chip],
            send_sem=send_sems.at[a, chip], recv_sem=recv_sems.at[a, chip],
            device_id=(x, y, 1 - c), device_id_type=pl.DeviceIdType.MESH)
            for a in range(n) for chip in range(N_CHIP)]
        for cp in copies:
            cp.start()
        for cp in copies:
            cp.wait()

    return pl.pallas_call(
        body, name="pair_exchange_grads",
        out_shape=[jax.ShapeDtypeStruct((N_CHIP,) + b.shape[1:], b.dtype) for b in blocks],
        in_specs=[pl.BlockSpec(memory_space=pl.ANY)] * n,
        out_specs=[pl.BlockSpec(memory_space=pl.ANY)] * n,
        scratch_shapes=[pltpu.SemaphoreType.DMA((n, N_CHIP)), pltpu.SemaphoreType.DMA((n, N_CHIP))],
    )(*blocks)


def _chip_exchange(sums):
    n = len(sums)

    def body(*refs):
        g_refs, out_refs = refs[:n], refs[n:2 * n]
        send_sems, recv_sems, local_sems = refs[2 * n:]
        x, y, c = lax.axis_index("x"), lax.axis_index("y"), lax.axis_index("c")
        my_chip = 2 * x + y

        def copy(a, chip):
            return pltpu.make_async_remote_copy(
                src_ref=g_refs[a].at[chip], dst_ref=out_refs[a].at[my_chip],
                send_sem=send_sems.at[a, chip], recv_sem=recv_sems.at[a, my_chip],
                device_id=(chip // 2, chip % 2, c), device_id_type=pl.DeviceIdType.MESH)

        def arrival(a, chip):
            return pltpu.make_async_remote_copy(
                src_ref=g_refs[a].at[chip], dst_ref=out_refs[a].at[chip],
                send_sem=send_sems.at[a, chip], recv_sem=recv_sems.at[a, chip],
                device_id=(chip // 2, chip % 2, c), device_id_type=pl.DeviceIdType.MESH)

        mine = [pltpu.make_async_copy(g_refs[a].at[my_chip], out_refs[a].at[my_chip], local_sems.at[a])
                for a in range(n)]
        for a in range(n):
            mine[a].start()
        for step in ("start", "wait_recv", "wait_send"):
            for chip in range(N_CHIP):
                @pl.when(chip != my_chip)
                def _(chip=chip, step=step):
                    for a in range(n):
                        if step == "start":
                            copy(a, chip).start()
                        elif step == "wait_recv":
                            arrival(a, chip).wait_recv()
                        else:
                            copy(a, chip).wait_send()
        for a in range(n):
            mine[a].wait()

    return pl.pallas_call(
        body, name="chip_exchange_grads",
        out_shape=[jax.ShapeDtypeStruct(b.shape, b.dtype) for b in sums],
        in_specs=[pl.BlockSpec(memory_space=pl.ANY)] * n,
        out_specs=[pl.BlockSpec(memory_space=pl.ANY)] * n,
        scratch_shapes=[pltpu.SemaphoreType.DMA((n, N_CHIP)), pltpu.SemaphoreType.DMA((n, N_CHIP)),
                        pltpu.SemaphoreType.DMA((n,))],
    )(*sums)


def _all_reduce_small(buf):
    def body(x_ref, out_ref, slots, send_sems, recv_sems):
        me = 4 * lax.axis_index("x") + 2 * lax.axis_index("y") + lax.axis_index("c")

        def copy(peer):
            return pltpu.make_async_remote_copy(
                src_ref=x_ref, dst_ref=slots.at[me],
                send_sem=send_sems.at[peer], recv_sem=recv_sems.at[me],
                device_id=_coords(peer), device_id_type=pl.DeviceIdType.MESH)

        def arrival(peer):
            return pltpu.make_async_remote_copy(
                src_ref=x_ref, dst_ref=slots.at[peer],
                send_sem=send_sems.at[peer], recv_sem=recv_sems.at[peer],
                device_id=_coords(peer), device_id_type=pl.DeviceIdType.MESH)

        slots[pl.ds(me, 1)] = x_ref[...][None]
        for peer in range(N_DEV):
            @pl.when(peer != me)
            def _(peer=peer):
                copy(peer).start()
        for peer in range(N_DEV):
            @pl.when(peer != me)
            def _(peer=peer):
                arrival(peer).wait_recv()
        for peer in range(N_DEV):
            @pl.when(peer != me)
            def _(peer=peer):
                copy(peer).wait_send()
        acc = slots[0]
        for peer in range(1, N_DEV):
            acc = acc + slots[peer]
        out_ref[...] = acc

    return pl.pallas_call(
        body, name="all_reduce_small",
        out_shape=jax.ShapeDtypeStruct(buf.shape, F32),
        in_specs=[pl.BlockSpec(memory_space=pltpu.VMEM)],
        out_specs=pl.BlockSpec(memory_space=pltpu.VMEM),
        scratch_shapes=[pltpu.VMEM((N_DEV,) + buf.shape, F32), pltpu.SemaphoreType.DMA((N_DEV,)),
                        pltpu.SemaphoreType.DMA((N_DEV,))],
        compiler_params=_params(),
    )(buf)


def _in_to_kernel(w):
    pad = jnp.zeros(w.shape[:-1] + (IN_COLS_PAD - IN_COLS,), w.dtype)
    return jnp.concatenate([w[..., :768], w[..., 832:3136], w[..., 768:832], w[..., 3136:], pad], axis=-1)


def _in_from_kernel(w):
    return jnp.concatenate([w[..., :768], w[..., 3072:3136], w[..., 768:3072], w[..., 3136:3144]], axis=-1)


def _split_to_kernel(w, a, b):
    r = w.shape[0]
    w3 = w.reshape(r, HEADS, a + b)
    return jnp.concatenate([w3[:, :, :a].reshape(r, HEADS * a), w3[:, :, a:].reshape(r, HEADS * b)], axis=1)


def _split_from_kernel(w, a, b):
    r = w.shape[0]
    return jnp.concatenate([w[:, :HEADS * a].reshape(r, HEADS, a), w[:, HEADS * a:].reshape(r, HEADS, b)],
                           axis=-1).reshape(r, HEADS * (a + b))


MIXER_BIG = ("w_in", "mla_w_q_b", "mla_w_kv_b", "w_out")


def _col_blocks_to_full(blk):
    nb, k, n = blk.shape
    return blk.transpose(1, 0, 2).reshape(k, nb * n)


def _full_to_col_blocks(g):
    k, c = g.shape
    return g.reshape(k, N_DEV, c // N_DEV).transpose(1, 0, 2).astype(BF16)


def _local_step(x, positions, target, gathered, small):
    inv_freq = ROPE_THETA ** (-jnp.arange(0, HEAD_DIM, 2, dtype=F32) / HEAD_DIM)
    cos64, sin64 = _rope_tables(positions, jnp.concatenate([-inv_freq, inv_freq])[None, :])
    cos = jnp.tile(cos64, (1, 20))
    sin = jnp.tile(sin64, (1, 20))

    layers = []
    for l in range(DEPTH):
        w = {}
        w["w_in"] = _in_to_kernel(_col_blocks_to_full(gathered["w_in"][:, l]))
        w["mla_w_q_b"] = _split_to_kernel(_col_blocks_to_full(gathered["mla_w_q_b"][:, l]), MLA_NOPE, MLA_ROPE)
        w["mla_w_kv_b"] = _split_to_kernel(_col_blocks_to_full(gathered["mla_w_kv_b"][:, l]), MLA_NOPE, MLA_V)
        blk = gathered["w_out"][:, l]
        w["w_out"] = blk.reshape(blk.shape[0] * blk.shape[1], blk.shape[2])
        for k in SMALL:
            if k != "final_norm":
                w[k] = small[k][l][None, :] if "norm" in k else small[k][l]
        layers.append(w)

    def ffn_w(f):
        return gathered[f + "_w_gate"], gathered[f + "_w_up"], gathered[f + "_w_down"]

    saved = []
    for l, w in enumerate(layers):
        t = "l%d" % l
        x, s1 = _ffn_fwd(t + "_ffn1", x, w["ffn1_norm"], *ffn_w("ffn1"), l)
        x, s2 = _mixer_fwd(t + "_mix", x, w, cos, sin)
        x, s3 = _ffn_fwd(t + "_ffn2", x, w["ffn2_norm"], *ffn_w("ffn2"), l)
        saved.append((s1, s2, s3))

    dx, d_final, loss, dxb = _loss_head(x, small["final_norm"][None, :], target)

    mixer_g = [None] * DEPTH
    small_g = [None] * DEPTH
    ffn_g = {"ffn1": None, "ffn2": None}
    for l in reversed(range(DEPTH)):
        w, (s1, s2, s3) = layers[l], saved[l]
        t = "l%d" % l
        sg = {}
        dx, dxb, sg["ffn2_norm"], ffn_g["ffn2"] = _ffn_bwd(t + "_ffn2", dx, dxb, s3, w["ffn2_norm"], *ffn_w("ffn2"), l,
                                                           ffn_g["ffn2"])
        dx, dxb, g = _mixer_bwd(t + "_mix", dx, dxb, s2, w, cos, sin)
        dx, dxb, sg["ffn1_norm"], ffn_g["ffn1"] = _ffn_bwd(t + "_ffn1", dx, dxb, s1, w["ffn1_norm"], *ffn_w("ffn1"), l,
                                                           ffn_g["ffn1"])
        mixer_g[l] = {
            "w_in": _full_to_col_blocks(_in_from_kernel(g["w_in"])),
            "mla_w_q_b": _full_to_col_blocks(_split_from_kernel(g["mla_w_q_b"], MLA_NOPE, MLA_ROPE)),
            "mla_w_kv_b": _full_to_col_blocks(_split_from_kernel(g["mla_w_kv_b"], MLA_NOPE, MLA_V)),
            "w_out": g["w_out"].astype(BF16).reshape(N_DEV, -1, g["w_out"].shape[1]),
        }
        for k in SMALL:
            if k in g:
                sg[k] = g[k]
        small_g[l] = {k: v.reshape(-1) for k, v in sg.items()}

    big = {k: jnp.stack([mixer_g[l][k] for l in range(DEPTH)], axis=1) for k in MIXER_BIG}
    for f in ("ffn1", "ffn2"):
        big[f + "_w_gate"], big[f + "_w_up"], big[f + "_w_down"] = ffn_g[f]
    sm = {k: jnp.stack([small_g[l][k] for l in range(DEPTH)]) for k in small_g[0]}
    sm["final_norm"] = d_final.reshape(-1)
    return loss, dx, big, sm


def _flat_rows(a):
    return a.reshape(-1, LANES)


def kernel(x, positions, ffn1_norm, ffn1_w_gate, ffn1_w_up, ffn1_w_down, mix_norm, w_in, mla_q_norm, mla_w_q_b, mla_kv_norm, mla_w_kv_b, swa_sinks, fox_forget_bias, w_out, ffn2_norm, ffn2_w_gate, ffn2_w_up, ffn2_w_down, final_norm, loss_target, m_ffn1_norm, m_ffn1_w_gate, m_ffn1_w_up, m_ffn1_w_down, m_mix_norm, m_w_in, m_mla_q_norm, m_mla_w_q_b, m_mla_kv_norm, m_mla_w_kv_b, m_swa_sinks, m_fox_forget_bias, m_w_out, m_ffn2_norm, m_ffn2_w_gate, m_ffn2_w_up, m_ffn2_w_down, m_final_norm, v_ffn1_norm, v_ffn1_w_gate, v_ffn1_w_up, v_ffn1_w_down, v_mix_norm, v_w_in, v_mla_q_norm, v_mla_w_q_b, v_mla_kv_norm, v_mla_w_kv_b, v_swa_sinks, v_fox_forget_bias, v_w_out, v_ffn2_norm, v_ffn2_w_gate, v_ffn2_w_up, v_ffn2_w_down, v_final_norm):
    given = dict(locals())
    weights = {k: given[k] for k in WEIGHTS}
    mom_m = {k: given["m_" + k] for k in WEIGHTS}
    mom_v = {k: given["v_" + k] for k in WEIGHTS}

    gathered = dict(zip(BIG, _all_gather([weights[k].astype(BF16) for k in BIG])))

    small = {k: weights[k] for k in SMALL}
    loss, grad_x, big_g, small_g = _local_step(x[0], positions[0][:, None], loss_target[0], gathered, small)

    core = lax.axis_index("c").astype(jnp.int32).reshape(1)
    flat3 = {k: big_g[k].reshape(N_DEV, -1, big_g[k].shape[-1]) for k in BIG}
    from_sibling = _pair_exchange([flat3[k] for k in BIG])
    sums = [_pair_sum("pair_sum_" + k, flat3[k], s, core) for k, s in zip(BIG, from_sibling)]
    received = dict(zip(BIG, _chip_exchange(sums)))

    small_sizes = [weights[k].size for k in SMALL]
    sbuf = jnp.concatenate([small_g[k].reshape(-1) for k in SMALL] + [loss.reshape(-1)])
    pad = (-sbuf.size) % (8 * LANES)
    sbuf = jnp.concatenate([sbuf, jnp.zeros((pad,), F32)])
    stot = _all_reduce_small(_flat_rows(sbuf)).reshape(-1)

    grad_w, delta, new_m, new_v = {}, {}, {}, {}
    for k in BIG:
        shape = weights[k].shape
        two_d = (-1, shape[-1])
        g, d, m, v = _adamw_slots("adamw_" + k, received[k], weights[k].reshape(two_d), mom_m[k].reshape(two_d),
                                  mom_v[k].reshape(two_d))
        grad_w[k], delta[k], new_m[k], new_v[k] = (t.reshape(shape) for t in (g, d, m, v))
    off = 0
    for k, n in zip(SMALL, small_sizes):
        shape = weights[k].shape
        two_d = (-1, shape[-1]) if len(shape) > 1 else (1, -1)
        grad_w[k] = stot[off:off + n].reshape(shape)
        off += n
        d, m, v = _adamw("adamw_" + k, weights[k].reshape(two_d), grad_w[k].reshape(two_d),
                         mom_m[k].reshape(two_d), mom_v[k].reshape(two_d))
        delta[k], new_m[k], new_v[k] = d.reshape(shape), m.reshape(shape), v.reshape(shape)
    loss_total = stot[off]

    return (loss_total, grad_x[None], *[grad_w[k] for k in WEIGHTS], *[delta[k] for k in WEIGHTS],
            *[new_m[k] for k in WEIGHTS], *[new_v[k] for k in WEIGHTS])
```

```python
import jax
import jax.numpy as jnp
from jax import lax
from jax.experimental import pallas as pl
from jax.experimental.pallas import tpu as pltpu

F32 = jnp.float32
BF16 = jnp.bfloat16

N_DEV = 8
N_CHIP = 4
DEPTH = 2
RMS_EPS = 1e-6
ROPE_THETA = 10000.0
HEADS = 8
MLA_Q_LORA = 512
MLA_KV_LORA = 256
MLA_NOPE = 128
MLA_ROPE = 64
MLA_V = 128
SWA_KV_HEADS = 2
HEAD_DIM = 64
WINDOW = 128
IN_COLS = 3144
IN_COLS_PAD = 3200

ADAM_LR = 0.001
ADAM_B1 = 0.9
ADAM_B2 = 0.999
ADAM_EPS = 1e-08
ADAM_WD = 0.01
ADAM_STEP = 10

LANES = 128
NEG = -1e30
LOG2E = 1.4426950408889634
LN2 = 0.6931471805599453
VMEM_LIMIT_BYTES = 48 * 1024 * 1024

EW_ROWS = 256
MM_TM = 1024
MM_TN = 1024
MM_TK = 1024
ATT_T = 512

BIG = ("ffn1_w_gate", "ffn1_w_up", "ffn1_w_down", "w_in", "mla_w_q_b", "mla_w_kv_b", "w_out",
       "ffn2_w_gate", "ffn2_w_up", "ffn2_w_down")
SMALL = ("ffn1_norm", "mix_norm", "mla_q_norm", "mla_kv_norm", "swa_sinks", "fox_forget_bias", "ffn2_norm",
         "final_norm")
WEIGHTS = ("ffn1_norm", "ffn1_w_gate", "ffn1_w_up", "ffn1_w_down", "mix_norm", "w_in", "mla_q_norm", "mla_w_q_b",
           "mla_kv_norm", "mla_w_kv_b", "swa_sinks", "fox_forget_bias", "w_out", "ffn2_norm", "ffn2_w_gate",
           "ffn2_w_up", "ffn2_w_down", "final_norm")


def _params(**kw):
    return pltpu.CompilerParams(vmem_limit_bytes=VMEM_LIMIT_BYTES, **kw)


def _tile(n, want):
    if n <= want:
        return n
    t = (want // LANES) * LANES
    while n % t:
        t -= LANES
    return t


def _rows(n, want):
    if n <= want:
        return n
    t = (want // 8) * 8
    while n % t:
        t -= 8
    return t


def _ew(name, fn, ins, outs, tm=EW_ROWS):
    tok = None
    for a, kind in ins:
        if kind == "row":
            tok = a.shape[-2]
    tm = _rows(tok, tm)
    steps = tok // tm

    def spec(shape, kind):
        if kind == "row":
            ax = len(shape) - 2
            blk = tuple(tm if d == ax else s for d, s in enumerate(shape))
            return pl.BlockSpec(blk, lambda i, ax=ax, n=len(shape): tuple(i if d == ax else 0 for d in range(n)))
        return pl.BlockSpec(tuple(shape), lambda i, n=len(shape): (0,) * n)

    n_in = len(ins)
    kinds = [k for _, _, k in outs]

    def body(*refs):
        vals = fn(*[r[...] for r in refs[:n_in]])
        for r, v, kind in zip(refs[n_in:], vals, kinds):
            if kind == "row":
                r[...] = v.astype(r.dtype)
            else:
                @pl.when(pl.program_id(0) == 0)
                def _(r=r):
                    r[...] = jnp.zeros(r.shape, r.dtype)
                r[...] += v.astype(r.dtype)

    res = pl.pallas_call(
        body, name=name, grid=(steps,),
        in_specs=[spec(a.shape, k) for a, k in ins],
        out_specs=[spec(s, k) for s, _, k in outs],
        out_shape=[jax.ShapeDtypeStruct(tuple(s), d) for s, d, _ in outs],
        compiler_params=_params(dimension_semantics=("arbitrary",)),
    )(*[a for a, _ in ins])
    return res


def _rms_fwd(name, x, g):
    def fn(x, g):
        r = lax.rsqrt(jnp.mean(x * x, axis=-1, keepdims=True) + RMS_EPS)
        return [x * r * g]
    return _ew(name, fn, [(x, "row"), (g, "full")], [(x.shape, BF16, "row")])[0]


def _rms_bwd(name, dh, x, g, res=None, also_bf16=False):
    def fn(dh, x, g, *rest):
        dh = dh.astype(F32)
        r = lax.rsqrt(jnp.mean(x * x, axis=-1, keepdims=True) + RMS_EPS)
        xh = x * r
        dxh = dh * g
        dx = r * (dxh - xh * jnp.mean(dxh * xh, axis=-1, keepdims=True))
        if rest:
            dx = dx + rest[0]
        return [dx, jnp.sum(dh * xh, axis=0, keepdims=True)] + ([dx] if also_bf16 else [])
    ins = [(dh, "row"), (x, "row"), (g, "full")] + ([(res, "row")] if res is not None else [])
    outs = [(x.shape, F32, "row"), (g.shape, F32, "acc")] + ([(x.shape, BF16, "row")] if also_bf16 else [])
    return _ew(name, fn, ins, outs)


def _rope(name, x, xs, cos, sin, sign=1.0):
    def fn(x, xs, c, s):
        return [x * c + sign * (xs * s)]
    return _ew(name, fn, [(x, "row"), (xs, "row"), (cos, "row"), (sin, "row")], [(x.shape, F32, "row")])[0]


def _rope_tables(positions, inv_freq2):
    def fn(pos, f):
        ang = pos.astype(F32) * f
        return [jnp.cos(ang), jnp.sin(ang)]
    t = positions.shape[0]
    return _ew("rope_tables", fn, [(positions, "row"), (inv_freq2, "full")],
               [((t, 2 * 32), F32, "row"), ((t, 2 * 32), F32, "row")])


def _mm(name, lhs, rhs, terms, epi, out_dtypes, extras=(), ta=False, tb=False, tm=MM_TM, tn=MM_TN, tk=MM_TK):
    if ta:
        kdim, m = lhs[0].shape
    else:
        m, kdim = lhs[0].shape
    n = rhs[0].shape[0] if tb else rhs[0].shape[1]
    tm, tn, tk = _tile(m, tm), _tile(n, tn), _tile(kdim, tk)
    nk = kdim // tk
    n_acc = 1 + max(a for _, _, a in terms)
    nl, nr, ne = len(lhs), len(rhs), len(extras)
    dims = (((0 if ta else 1,), (1 if tb else 0,)), ((), ()))

    def body(*refs):
        l_refs, r_refs = refs[:nl], refs[nl:nl + nr]
        e_refs = refs[nl + nr:nl + nr + ne]
        o_refs = refs[nl + nr + ne:len(refs) - n_acc]
        accs = refs[len(refs) - n_acc:]
        k = pl.program_id(2)

        @pl.when(k == 0)
        def _():
            for acc in accs:
                acc[...] = jnp.zeros(acc.shape, F32)

        lv, rv = {}, {}
        for li, ri, ai in terms:
            if li not in lv:
                lv[li] = l_refs[li][...].astype(BF16)
            if ri not in rv:
                rv[ri] = r_refs[ri][...].astype(BF16)
            accs[ai][...] += lax.dot_general(lv[li], rv[ri], dims, preferred_element_type=F32)

        @pl.when(k == nk - 1)
        def _():
            outs = epi([acc[...] for acc in accs], [e[...] for e in e_refs])
            for o, v in zip(o_refs, outs):
                o[...] = v.astype(o.dtype)

    l_spec = pl.BlockSpec((tk, tm), lambda i, j, k: (k, i)) if ta else pl.BlockSpec((tm, tk), lambda i, j, k: (i, k))
    r_spec = pl.BlockSpec((tn, tk), lambda i, j, k: (j, k)) if tb else pl.BlockSpec((tk, tn), lambda i, j, k: (k, j))
    o_spec = pl.BlockSpec((tm, tn), lambda i, j, k: (i, j))
    return pl.pallas_call(
        body, name=name, grid=(m // tm, n // tn, nk),
        in_specs=[l_spec] * nl + [r_spec] * nr + [o_spec] * ne,
        out_specs=[o_spec] * len(out_dtypes),
        out_shape=[jax.ShapeDtypeStruct((m, n), d) for d in out_dtypes],
        scratch_shapes=[pltpu.VMEM((tm, tn), F32)] * n_acc,
        compiler_params=_params(dimension_semantics=("parallel", "parallel", "arbitrary")),
    )(*lhs, *rhs, *extras)


def _mm1(name, a, b, out_dtype=F32, scale=None, add=None, **kw):
    def epi(accs, ex):
        v = accs[0] if scale is None else accs[0] * scale
        return [v + ex[0] if ex else v]
    return _mm(name, [a], [b], [(0, 0, 0)], epi, [out_dtype], extras=[] if add is None else [add], **kw)[0]


def _sink_grad(name, sink3, lse, delta):
    def fn(sk, lse, dl):
        return [-jnp.sum(jnp.exp2(sk * LOG2E - lse) * dl, axis=1, keepdims=True)]
    return _ew(name, fn, [(sink3, "full"), (lse, "row"), (delta, "row")], [(sink3.shape, F32, "acc")], tm=512)[0]


def _resident(block, index_map):
    return pl.BlockSpec(block, index_map, pipeline_mode=pl.Buffered(1))


def _attn_fwd(name, q_t, k, v_t4, scale, group=1, cq_row=None, ck_col=None, sink=None, window=False, t=ATT_T):
    h_n, dq, tok = q_t.shape
    dv = v_t4.shape[2]
    t = min(t, tok)
    nq = tok // t
    bias = cq_row is not None
    has_sink = sink is not None

    def body(*refs):
        q_ref, k_ref, v_ref = refs[:3]
        pos = 3
        cq_ref = ck_ref = sink_ref = None
        if bias:
            cq_ref, ck_ref = refs[pos], refs[pos + 1]
            pos += 2
        if has_sink:
            sink_ref = refs[pos]
            pos += 1
        o_ref, lse_ref, m_s, l_s, acc_s = refs[pos:]
        h, qi = pl.program_id(0), pl.program_id(1)
        qs = (q_ref[...].astype(F32) * (scale * LOG2E)).astype(BF16)
        m_s[...] = jnp.full(m_s.shape, sink_ref[h] * LOG2E if has_sink else NEG, F32)
        l_s[...] = jnp.full(l_s.shape, 1.0 if has_sink else 0.0, F32)
        acc_s[...] = jnp.zeros(acc_s.shape, F32)
        c_ref = cq_ref[:, 0:1] if bias else None

        def scores(j, masked):
            rows = pl.ds(pl.multiple_of(j * t, t), t)
            s = jnp.dot(k_ref[rows, :], qs, preferred_element_type=F32)
            if bias:
                s = s - (ck_ref[rows, :] - c_ref)
            if masked:
                kpos = j * t + lax.broadcasted_iota(jnp.int32, s.shape, 0)
                qpos = qi * t + lax.broadcasted_iota(jnp.int32, s.shape, 1)
                mask = kpos <= qpos
                if window:
                    mask = mask & (kpos > qpos - WINDOW)
                s = jnp.where(mask, s, NEG)
            return s

        def step(tiles):
            ss = [scores(j, masked) for j, masked in tiles]
            m_prev = m_s[...]
            m_new = m_prev
            for s in ss:
                m_new = jnp.maximum(m_new, jnp.max(s, axis=0, keepdims=True))
            alpha = jnp.exp2(m_prev - m_new)
            l_new, acc = alpha * l_s[...], alpha * acc_s[...]
            for (j, _), s in zip(tiles, ss):
                p = jnp.exp2(s - m_new)
                l_new = l_new + jnp.sum(p, axis=0, keepdims=True)
                acc = acc + jnp.dot(v_ref[j], p.astype(BF16), preferred_element_type=F32)
            l_s[...], acc_s[...], m_s[...] = l_new, acc, m_new

        if window:
            pl.when(qi > 0)(lambda: step([(qi - 1, True), (qi, True)]))
            pl.when(qi == 0)(lambda: step([(qi, True)]))
        else:
            def below(i, carry):
                step([(2 * i, False), (2 * i + 1, False)])
                return carry
            lax.fori_loop(0, qi // 2, below, 0)
            pl.when(qi % 2 == 1)(lambda: step([(qi - 1, False), (qi, True)]))
            pl.when(qi % 2 == 0)(lambda: step([(qi, True)]))
        o_ref[...] = (acc_s[...] / l_s[...]).astype(o_ref.dtype)
        lse_ref[...] = m_s[...] + jnp.log(l_s[...]) * LOG2E

    nk = tok // t
    in_specs = [
        pl.BlockSpec((None, dq, t), lambda h, qi: (h, 0, qi)),
        _resident((None, tok, dq), lambda h, qi: (h // group, 0, 0)),
        _resident((None, nk, dv, t), lambda h, qi: (h // group, 0, 0, 0)),
    ]
    args = [q_t, k, v_t4]
    if bias:
        in_specs += [pl.BlockSpec((None, 1, t), lambda h, qi: (h, 0, qi)),
                     _resident((None, tok, 1), lambda h, qi: (h, 0, 0))]
        args += [cq_row, ck_col]
    if has_sink:
        in_specs.append(pl.BlockSpec(memory_space=pltpu.SMEM))
        args.append(sink)
    return pl.pallas_call(
        body, name=name, grid=(h_n, nq),
        in_specs=in_specs,
        out_specs=[pl.BlockSpec((None, dv, t), lambda h, qi: (h, 0, qi)),
                   pl.BlockSpec((None, 1, t), lambda h, qi: (h, 0, qi))],
        out_shape=[jax.ShapeDtypeStruct((h_n, dv, tok), BF16), jax.ShapeDtypeStruct((h_n, 1, tok), F32)],
        scratch_shapes=[pltpu.VMEM((1, t), F32), pltpu.VMEM((1, t), F32), pltpu.VMEM((dv, t), F32)],
        compiler_params=_params(dimension_semantics=("parallel", "parallel")),
    )(*args)


def _attn_bwd(name, q, q_t, k, k_t4, v_t4, do, do_t, lse, delta, scale, group=1, cq_col=None, ck_row4=None,
              window=False, t=ATT_T, side=None):
    h_n, tok, dq = q.shape
    dv = do.shape[2]
    t = min(t, tok)
    nq = tok // t
    bias = cq_col is not None

    n_in = 9 + (2 if bias else 0)
    n_out = 3 + (2 if bias else 0)
    side_args, side_in_specs, side_out_specs, side_out_shape, side_scratch, side_aliases = (
        side.specs() if side is not None else ([], [], [], [], [], {}))

    def body(*refs):
        q_ref, qt_ref, k_ref, kt_ref, vt_ref, do_ref, dot_ref, lse_ref, dl_ref = refs[:9]
        cq_ref = ck_ref = dc_ref = dr_ref = None
        if bias:
            cq_ref, ck_ref = refs[9], refs[10]
        pos = n_in + len(side_args)
        dq_ref, dk_ref, dv_ref = refs[pos:pos + 3]
        if bias:
            dc_ref, dr_ref = refs[pos + 3], refs[pos + 4]
        h, qi = pl.program_id(0), pl.program_id(1)
        if side is not None:
            side_start, side_finish = side.ops(refs[n_in:pos], refs[pos + n_out:pos + n_out + side.n],
                                               refs[pos + n_out + side.n:])
            pl.when((h == 0) & (qi == 0))(side_start)

        @pl.when(qi == 0)
        def _():
            dk_ref[...] = jnp.zeros(dk_ref.shape, F32)
            dv_ref[...] = jnp.zeros(dv_ref.shape, F32)
            if bias:
                dc_ref[...] = jnp.zeros(dc_ref.shape, F32)

        qs = (q_ref[...].astype(F32) * (scale * LOG2E)).astype(BF16)
        dq_ref[...] = jnp.zeros(dq_ref.shape, F32)
        if bias:
            dr_ref[...] = jnp.zeros(dr_ref.shape, F32)
        c_ref = cq_ref[0:1, :] if bias else None

        def step(j, masked):
            s = jnp.dot(qs, kt_ref[j], preferred_element_type=F32)
            if bias:
                s = s - (ck_ref[j] - c_ref)
            p = jnp.exp2(s - lse_ref[...])
            if masked:
                qpos = qi * t + lax.broadcasted_iota(jnp.int32, s.shape, 0)
                kpos = j * t + lax.broadcasted_iota(jnp.int32, s.shape, 1)
                mask = kpos <= qpos
                if window:
                    mask = mask & (kpos > qpos - WINDOW)
                p = jnp.where(mask, p, 0.0)
            pb = p.astype(BF16)
            dv_ref[j] += jnp.dot(dot_ref[...], pb, preferred_element_type=F32)
            dp = jnp.dot(do_ref[...], vt_ref[j], preferred_element_type=F32)
            ds = p * (dp - dl_ref[...])
            dsb = ds.astype(BF16)
            rows = pl.ds(pl.multiple_of(j * t, t), t)
            dq_ref[...] += jnp.dot(dsb, k_ref[rows, :], preferred_element_type=F32)
            dk_ref[j] += scale * jnp.dot(qt_ref[...], dsb, preferred_element_type=F32)
            if bias:
                dc_ref[j] += jnp.sum(ds, axis=0, keepdims=True)
                dr_ref[...] += jnp.sum(ds, axis=1, keepdims=True)

        if window:
            pl.when(qi > 0)(lambda: step(qi - 1, True))
        else:
            def below(j, carry):
                step(j, False)
                return carry
            lax.fori_loop(0, qi, below, 0)
        step(qi, True)
        dq_ref[...] = dq_ref[...] * scale
        if side is not None:
            pl.when((h == h_n - 1) & (qi == nq - 1))(side_finish)

    nk = nq

    def q_tile(shape_tail):
        return pl.BlockSpec((None, t) + shape_tail, lambda h, qi: (h, qi, 0))

    in_specs = [
        q_tile((dq,)),
        pl.BlockSpec((None, dq, t), lambda h, qi: (h, 0, qi)),
        _resident((None, tok, dq), lambda h, qi: (h // group, 0, 0)),
        _resident((None, nk, dq, t), lambda h, qi: (h // group, 0, 0, 0)),
        _resident((None, nk, dv, t), lambda h, qi: (h // group, 0, 0, 0)),
        q_tile((dv,)),
        pl.BlockSpec((None, dv, t), lambda h, qi: (h, 0, qi)),
        q_tile((1,)),
        q_tile((1,)),
    ]
    args = [q, q_t, k, k_t4, v_t4, do, do_t, lse, delta]
    out_specs = [q_tile((dq,)),
                 pl.BlockSpec((None, nk, dq, t), lambda h, qi: (h, 0, 0, 0)),
                 pl.BlockSpec((None, nk, dv, t), lambda h, qi: (h, 0, 0, 0))]
    out_shape = [jax.ShapeDtypeStruct((h_n, tok, dq), F32), jax.ShapeDtypeStruct((h_n, nk, dq, t), F32),
                 jax.ShapeDtypeStruct((h_n, nk, dv, t), F32)]
    if bias:
        in_specs += [q_tile((1,)), _resident((None, nk, 1, t), lambda h, qi: (h, 0, 0, 0))]
        args += [cq_col, ck_row4]
        out_specs += [pl.BlockSpec((None, nk, 1, t), lambda h, qi: (h, 0, 0, 0)), q_tile((1,))]
        out_shape += [jax.ShapeDtypeStruct((h_n, nk, 1, t), F32), jax.ShapeDtypeStruct((h_n, tok, 1), F32)]
    res = pl.pallas_call(
        body, name=name, grid=(h_n, nq),
        in_specs=in_specs + side_in_specs, out_specs=out_specs + side_out_specs,
        out_shape=out_shape + side_out_shape, scratch_shapes=side_scratch,
        input_output_aliases={n_in + i: n_out + o for i, o in side_aliases.items()},
        compiler_params=_params(dimension_semantics=("arbitrary" if side is not None else "parallel", "arbitrary")),
    )(*args, *side_args)
    return (res[:n_out], res[n_out:]) if side is not None else res


def _delta_t(name, do_t, o_t, tl=1024):
    h_n, dv, tok = do_t.shape
    tl = min(tl, tok)

    def body(do_ref, o_ref, out_ref):
        out_ref[...] = jnp.sum(do_ref[...].astype(F32) * o_ref[...].astype(F32), axis=1, keepdims=True)

    spec = pl.BlockSpec((h_n, dv, tl), lambda i: (0, 0, i))
    return pl.pallas_call(
        body, name=name, grid=(tok // tl,), in_specs=[spec, spec],
        out_specs=pl.BlockSpec((h_n, 1, tl), lambda i: (0, 0, i)),
        out_shape=jax.ShapeDtypeStruct((h_n, 1, tok), F32),
        compiler_params=_params(dimension_semantics=("parallel",)),
    )(do_t, o_t)


def _log_sigmoid(z):
    return jnp.minimum(z, 0.0) - jnp.log(1.0 + jnp.exp(-jnp.abs(z)))


def _gate_fwd(zt, bias):
    tok = zt.shape[1]

    def body(z_ref, b_ref, c_ref):
        x = _log_sigmoid(z_ref[...] + b_ref[...])
        lane = lax.broadcasted_iota(jnp.int32, x.shape, 1)
        k = 1
        while k < tok:
            x = x + jnp.where(lane >= k, pltpu.roll(x, k, axis=1), 0.0)
            k *= 2
        c_ref[...] = x * LOG2E

    return pl.pallas_call(body, name="fox_gate_fwd", out_shape=jax.ShapeDtypeStruct(zt.shape, F32),
                          compiler_params=_params())(zt, bias)


def _gate_bwd(d_rows, d_cols, zt, bias):
    tok = zt.shape[1]

    def body(dr_ref, dc_ref, z_ref, b_ref, dz_ref, db_ref):
        x = dr_ref[...] - dc_ref[...]
        lane = lax.broadcasted_iota(jnp.int32, x.shape, 1)
        k = 1
        while k < tok:
            x = x + jnp.where(lane < tok - k, pltpu.roll(x, tok - k, axis=1), 0.0)
            k *= 2
        dz = x / (1.0 + jnp.exp(z_ref[...] + b_ref[...]))
        dz_ref[...] = dz
        db_ref[...] = jnp.sum(dz, axis=1, keepdims=True)

    return pl.pallas_call(body, name="fox_gate_bwd",
                          out_shape=[jax.ShapeDtypeStruct(zt.shape, F32), jax.ShapeDtypeStruct(bias.shape, F32)],
                          compiler_params=_params())(d_rows, d_cols, zt, bias)


_NT = (((1,), (1,)), ((), ()))
_TN = (((0,), (0,)), ((), ()))


def _ffn_gate_up(name, h, wg, wu, l, tm=512):
    tok, d = h.shape
    nb, n = wg.shape[0], wg.shape[3]
    tm = min(tm, tok)

    def body(h_ref, wg_ref, wu_ref, u_ref, v_ref, a_ref):
        hv = h_ref[...]
        u = jnp.dot(hv, wg_ref[...], preferred_element_type=F32)
        v = jnp.dot(hv, wu_ref[...], preferred_element_type=F32)
        u_ref[...] = u.astype(BF16)
        v_ref[...] = v.astype(BF16)
        a_ref[...] = (u * jax.nn.sigmoid(u) * v).astype(BF16)

    w_spec = pl.BlockSpec((None, None, d, n), lambda j, i: (j, l, 0, 0))
    o_spec = pl.BlockSpec((None, tm, n), lambda j, i: (j, i, 0))
    return pl.pallas_call(
        body, name=name, grid=(nb, tok // tm),
        in_specs=[pl.BlockSpec((tm, d), lambda j, i: (i, 0)), w_spec, w_spec],
        out_specs=[o_spec] * 3, out_shape=[jax.ShapeDtypeStruct((nb, tok, n), BF16)] * 3,
        compiler_params=_params(dimension_semantics=("parallel", "parallel")),
    )(h, wg, wu)


def _ffn_down(name, a, wd, x, l, tm=1024, tn=1024):
    nb, tok, n = a.shape
    d = wd.shape[3]
    tm, tn = min(tm, tok), min(tn, d)

    def body(a_ref, wd_ref, x_ref, y_ref, acc):
        j = pl.program_id(2)

        @pl.when(j == 0)
        def _():
            acc[...] = jnp.zeros(acc.shape, F32)
        acc[...] += jnp.dot(a_ref[...], wd_ref[...], preferred_element_type=F32)

        @pl.when(j == nb - 1)
        def _():
            y_ref[...] = x_ref[...] + 0.5 * acc[...]

    return pl.pallas_call(
        body, name=name, grid=(tok // tm, d // tn, nb),
        in_specs=[pl.BlockSpec((None, tm, n), lambda i, c, j: (j, i, 0)),
                  pl.BlockSpec((None, None, n, tn), lambda i, c, j: (j, l, 0, c)),
                  pl.BlockSpec((tm, tn), lambda i, c, j: (i, c))],
        out_specs=pl.BlockSpec((tm, tn), lambda i, c, j: (i, c)),
        out_shape=jax.ShapeDtypeStruct((tok, d), F32),
        scratch_shapes=[pltpu.VMEM((tm, tn), F32)],
        compiler_params=_params(dimension_semantics=("parallel", "parallel", "arbitrary")),
    )(a, wd, x)


def _ffn_d_act(name, dy, u, v, wd, l, tm=512):
    nb, tok, n = u.shape
    d = dy.shape[1]
    tm = min(tm, tok)

    def body(dy_ref, u_ref, v_ref, wd_ref, du_ref, dv_ref):
        da = 0.5 * lax.dot_general(dy_ref[...], wd_ref[...], _NT, preferred_element_type=F32)
        uv, vv = u_ref[...].astype(F32), v_ref[...].astype(F32)
        sg = jax.nn.sigmoid(uv)
        du_ref[...] = (da * vv * (sg * (1.0 + uv * (1.0 - sg)))).astype(BF16)
        dv_ref[...] = (da * (uv * sg)).astype(BF16)

    t_spec = pl.BlockSpec((None, tm, n), lambda j, i: (j, i, 0))
    return pl.pallas_call(
        body, name=name, grid=(nb, tok // tm),
        in_specs=[pl.BlockSpec((tm, d), lambda j, i: (i, 0)), t_spec, t_spec,
                  pl.BlockSpec((None, None, n, d), lambda j, i: (j, l, 0, 0))],
        out_specs=[t_spec] * 2, out_shape=[jax.ShapeDtypeStruct((nb, tok, n), BF16)] * 2,
        compiler_params=_params(dimension_semantics=("parallel", "parallel")),
    )(dy, u, v, wd)


def _ffn_d_h(name, du, dv, wg, wu, l, tm=1024):
    nb, tok, n = du.shape
    d = wg.shape[2]
    tm = min(tm, tok)

    def body(du_ref, dv_ref, wg_ref, wu_ref, dh_ref, acc):
        j = pl.program_id(1)

        @pl.when(j == 0)
        def _():
            acc[...] = jnp.zeros(acc.shape, F32)
        acc[...] += (lax.dot_general(du_ref[...], wg_ref[...], _NT, preferred_element_type=F32)
                     + lax.dot_general(dv_ref[...], wu_ref[...], _NT, preferred_element_type=F32))

        @pl.when(j == nb - 1)
        def _():
            dh_ref[...] = acc[...].astype(BF16)

    t_spec = pl.BlockSpec((None, tm, n), lambda i, j: (j, i, 0))
    w_spec = pl.BlockSpec((None, None, d, n), lambda i, j: (j, l, 0, 0))
    return pl.pallas_call(
        body, name=name, grid=(tok // tm, nb),
        in_specs=[t_spec, t_spec, w_spec, w_spec],
        out_specs=pl.BlockSpec((tm, d), lambda i, j: (i, 0)),
        out_shape=jax.ShapeDtypeStruct((tok, d), BF16),
        scratch_shapes=[pltpu.VMEM((tm, d), F32)],
        compiler_params=_params(dimension_semantics=("parallel", "arbitrary")),
    )(du, dv, wg, wu)


def _ffn_wgrad_in(name, h, du, dv, l, like, prev=None, tk=1024, td=1024):
    tok, d = h.shape
    nb, _, n = du.shape
    tk, td = min(tk, tok), min(td, d)
    nk = tok // tk
    n_in = 3

    def body(*refs):
        h_ref, du_ref, dv_ref = refs[:3]
        og_ref, ou_ref, accg, accu = refs[len(refs) - 4:]
        k = pl.program_id(2)

        @pl.when(k == 0)
        def _():
            accg[...] = jnp.zeros(accg.shape, F32)
            accu[...] = jnp.zeros(accu.shape, F32)
        hv = h_ref[...]
        accg[...] += lax.dot_general(hv, du_ref[...], _TN, preferred_element_type=F32)
        accu[...] += lax.dot_general(hv, dv_ref[...], _TN, preferred_element_type=F32)

        @pl.when(k == nk - 1)
        def _():
            og_ref[...] = accg[...].astype(BF16)
            ou_ref[...] = accu[...].astype(BF16)

    t_spec = pl.BlockSpec((None, tk, n), lambda j, c, k: (j, k, 0))
    o_spec = pl.BlockSpec((None, None, td, n), lambda j, c, k: (j, l, c, 0))
    in_specs = [pl.BlockSpec((tk, td), lambda j, c, k: (k, c)), t_spec, t_spec]
    args = [h, du, dv]
    aliases = {}
    if prev is not None:
        in_specs += [pl.BlockSpec(memory_space=pl.ANY)] * 2
        args += list(prev)
        aliases = {n_in: 0, n_in + 1: 1}
    return pl.pallas_call(
        body, name=name, grid=(nb, d // td, nk),
        in_specs=in_specs, out_specs=[o_spec] * 2,
        out_shape=[jax.ShapeDtypeStruct(like.shape, BF16)] * 2,
        scratch_shapes=[pltpu.VMEM((td, n), F32)] * 2,
        input_output_aliases=aliases,
        compiler_params=_params(dimension_semantics=("parallel", "parallel", "arbitrary")),
    )(*args)


def _ffn_wgrad_out(name, a, dy, l, like, prev=None, tk=1024):
    nb, tok, n = a.shape
    d = dy.shape[1]
    tk = min(tk, tok)
    nk = tok // tk

    def body(*refs):
        a_ref, dy_ref = refs[:2]
        o_ref, acc = refs[len(refs) - 2:]
        k = pl.program_id(1)

        @pl.when(k == 0)
        def _():
            acc[...] = jnp.zeros(acc.shape, F32)
        acc[...] += lax.dot_general(a_ref[...], dy_ref[...], _TN, preferred_element_type=F32)

        @pl.when(k == nk - 1)
        def _():
            o_ref[...] = (0.5 * acc[...]).astype(BF16)

    in_specs = [pl.BlockSpec((None, tk, n), lambda j, k: (j, k, 0)), pl.BlockSpec((tk, d), lambda j, k: (k, 0))]
    args = [a, dy]
    aliases = {}
    if prev is not None:
        in_specs.append(pl.BlockSpec(memory_space=pl.ANY))
        args.append(prev)
        aliases = {2: 0}
    return pl.pallas_call(
        body, name=name, grid=(nb, nk),
        in_specs=in_specs, out_specs=pl.BlockSpec((None, None, n, d), lambda j, k: (j, l, 0, 0)),
        out_shape=jax.ShapeDtypeStruct(like.shape, BF16),
        scratch_shapes=[pltpu.VMEM((n, d), F32)],
        input_output_aliases=aliases,
        compiler_params=_params(dimension_semantics=("parallel", "arbitrary")),
    )(*args)


def _ffn_fwd(tag, x, g, wg, wu, wd, l):
    h = _rms_fwd(tag + "_norm", x, g)
    u, v, a = _ffn_gate_up(tag + "_gate_up", h, wg, wu, l)
    y = _ffn_down(tag + "_down", a, wd, x, l)
    return y, (x, h, u, v, a)


def _ffn_bwd(tag, dy, dyb, saved, g, wg, wu, wd, l, prev):
    x, h, u, v, a = saved
    du, dv = _ffn_d_act(tag + "_d_act", dyb, u, v, wd, l)
    d_wd = _ffn_wgrad_out(tag + "_d_wd", a, dyb, l, wd, None if prev is None else prev[2])
    d_wg, d_wu = _ffn_wgrad_in(tag + "_d_wgu", h, du, dv, l, wg, None if prev is None else prev[:2])
    dh = _ffn_d_h(tag + "_d_h", du, dv, wg, wu, l)
    dx, dg, dxb = _rms_bwd(tag + "_d_norm", dh, x, g, res=dy, also_bf16=True)
    return dx, dxb, dg, (d_wg, d_wu, d_wd)


def _swap_halves(x):
    t, w = x.shape
    return x.reshape(t, w // HEAD_DIM, 2, HEAD_DIM // 2)[:, :, ::-1, :].reshape(t, w)


def _unheads(x):
    n, t, d = x.shape
    return x.transpose(1, 0, 2).reshape(t, n * d)


def _nat(x3):
    return x3.transpose(1, 0, 2).astype(BF16)


def _tr(x3):
    return x3.transpose(1, 2, 0).astype(BF16)


def _tr4(x3, t):
    tok, n, d = x3.shape
    return x3.reshape(tok // t, t, n, d).transpose(2, 0, 3, 1).astype(BF16)


def _from_t(x_t):
    n, d, tok = x_t.shape
    return x_t.transpose(2, 0, 1).reshape(tok, n * d)


def _from_t4(x4):
    n, nk, d, t = x4.shape
    return x4.transpose(1, 3, 0, 2).reshape(nk * t, n, d)


def _col(row):
    return row.transpose(0, 2, 1)


def _mixer_fwd(tag, x, w, cos, sin):
    tok = x.shape[0]
    h2 = _rms_fwd(tag + "_norm", x, w["mix_norm"])
    p = _mm1(tag + "_in", h2, w["w_in"], tn=640)
    c_q, c_kv = p[:, :512], p[:, 512:768]
    q_s, k_s, v_s = p[:, 768:1280], p[:, 1280:1408], p[:, 1408:1536]
    q_f, k_f, v_f = p[:, 1536:2048], p[:, 2048:2560], p[:, 2560:3072]
    k_rope, f_logit = p[:, 3072:3136], p[:, 3136:3144]

    qn = _rms_fwd(tag + "_q_norm", c_q, w["mla_q_norm"])
    qm = _mm1(tag + "_q_b", qn, w["mla_w_q_b"])
    kvn = _rms_fwd(tag + "_kv_norm", c_kv, w["mla_kv_norm"])
    kvm = _mm1(tag + "_kv_b", kvn, w["mla_w_kv_b"])

    rin = jnp.concatenate([qm[:, 1024:], q_s, k_s, k_rope, jnp.zeros((tok, 64), F32)], axis=1)
    rout = _rope(tag + "_rope", rin, _swap_halves(rin), cos, sin)
    q_pe, q_sr, k_sr, k_pe = rout[:, :512], rout[:, 512:1024], rout[:, 1024:1152], rout[:, 1152:1216]

    t = min(ATT_T, tok)
    q_m = jnp.concatenate([qm[:, :1024].reshape(tok, HEADS, MLA_NOPE), q_pe.reshape(tok, HEADS, MLA_ROPE)], axis=-1)
    k_m = jnp.concatenate([kvm[:, :1024].reshape(tok, HEADS, MLA_NOPE),
                           jnp.broadcast_to(k_pe[:, None, :], (tok, HEADS, MLA_ROPE))], axis=-1)
    v_m = kvm[:, 1024:].reshape(tok, HEADS, MLA_V)
    mla = dict(q=_nat(q_m), q_t=_tr(q_m), k=_nat(k_m), k_t4=_tr4(k_m, t), v_t4=_tr4(v_m, t))
    mla["o_t"], mla["lse"] = _attn_fwd(tag + "_mla_fwd", mla["q_t"], mla["k"], mla["v_t4"],
                                       (MLA_NOPE + MLA_ROPE) ** -0.5, t=t)

    q_s3, k_s3 = q_sr.reshape(tok, HEADS, HEAD_DIM), k_sr.reshape(tok, SWA_KV_HEADS, HEAD_DIM)
    v_s3 = v_s.reshape(tok, SWA_KV_HEADS, HEAD_DIM)
    swa = dict(q=_nat(q_s3), q_t=_tr(q_s3), k=_nat(k_s3), k_t4=_tr4(k_s3, t), v_t4=_tr4(v_s3, t))
    swa["o_t"], swa["lse"] = _attn_fwd(tag + "_swa_fwd", swa["q_t"], swa["k"], swa["v_t4"], HEAD_DIM ** -0.5,
                                       group=HEADS // SWA_KV_HEADS, sink=w["swa_sinks"], window=True, t=t)

    zt = f_logit.T
    c = _gate_fwd(zt, w["fox_forget_bias"].reshape(HEADS, 1))
    q_f3, k_f3, v_f3 = (a.reshape(tok, HEADS, HEAD_DIM) for a in (q_f, k_f, v_f))
    fox = dict(q=_nat(q_f3), q_t=_tr(q_f3), k=_nat(k_f3), k_t4=_tr4(k_f3, t), v_t4=_tr4(v_f3, t),
               c_col=c[:, :, None], c_row4=c.reshape(HEADS, tok // t, 1, t))
    fox["o_t"], fox["lse"] = _attn_fwd(tag + "_fox_fwd", fox["q_t"], fox["k"], fox["v_t4"], HEAD_DIM ** -0.5,
                                       cq_row=c[:, None, :], ck_col=fox["c_col"], t=t)

    mixed = jnp.concatenate([_from_t(mla["o_t"]), _from_t(swa["o_t"]), _from_t(fox["o_t"])], axis=1)
    y = _mm1(tag + "_out", mixed, w["w_out"], add=x)
    saved = dict(x=x, h2=h2, c_q=c_q, c_kv=c_kv, qn=qn, kvn=kvn, zt=zt, mixed=mixed, mla=mla, swa=swa, fox=fox)
    return y, saved


def _mixer_bwd(tag, dy, dyb, s, w, cos, sin, side=None):
    tok = dy.shape[0]
    g = {}
    dmixed = _mm1(tag + "_d_mixed", dyb, w["w_out"], out_dtype=BF16, tb=True)
    g["w_out"] = _mm1(tag + "_d_wout", s["mixed"], dyb, ta=True)
    t = min(ATT_T, tok)

    def attn_bwd(name, a, d_out, scale, **kw):
        do3 = d_out.reshape(tok, HEADS, -1)
        do_t = _tr(do3)
        dl = _col(_delta_t(name + "_delta", do_t, a["o_t"]))
        return dl, _attn_bwd(name + "_bwd", a["q"], a["q_t"], a["k"], a["k_t4"], a["v_t4"], _nat(do3), do_t,
                             _col(a["lse"]), dl, scale, t=t, **kw)

    _, mla_res = attn_bwd(tag + "_mla", s["mla"], dmixed[:, :1024], (MLA_NOPE + MLA_ROPE) ** -0.5, side=side)
    (dq_m, dk_m4, dv_m4), exchanged = mla_res if side is not None else (mla_res, None)
    dl, (dq_sh, dk_s4, dv_s4) = attn_bwd(tag + "_swa", s["swa"], dmixed[:, 1024:1536], HEAD_DIM ** -0.5,
                                         group=HEADS // SWA_KV_HEADS, window=True)
    g["swa_sinks"] = _sink_grad(tag + "_d_sink", w["swa_sinks"].reshape(HEADS, 1, 1), _col(s["swa"]["lse"]),
                                dl).reshape(HEADS)
    _, (dq_fh, dk_f4, dv_f4, d_cols, d_rows) = attn_bwd(tag + "_fox", s["fox"], dmixed[:, 1536:], HEAD_DIM ** -0.5,
                                                        cq_col=s["fox"]["c_col"], ck_row4=s["fox"]["c_row4"])
    dzt, dbias = _gate_bwd(d_rows[:, :, 0], d_cols.reshape(HEADS, tok), s["zt"],
                           w["fox_forget_bias"].reshape(HEADS, 1))
    g["fox_forget_bias"] = dbias.reshape(HEADS)

    grp = HEADS // SWA_KV_HEADS
    dq_mt = dq_m.transpose(1, 0, 2)
    dk_mt = _from_t4(dk_m4)
    d_qpe = dq_mt[:, :, MLA_NOPE:].reshape(tok, HEADS * MLA_ROPE)
    d_kpe_heads = dk_mt[:, :, MLA_NOPE:].reshape(tok, HEADS * MLA_ROPE)
    d_qs = _unheads(dq_sh)
    d_ks_heads = _from_t4(dk_s4).reshape(tok, SWA_KV_HEADS, grp, HEAD_DIM).transpose(0, 2, 1, 3).reshape(tok, grp * 128)
    d_vs_heads = _from_t4(dv_s4).reshape(tok, SWA_KV_HEADS, grp, HEAD_DIM).transpose(0, 2, 1, 3).reshape(tok, grp * 128)

    def fold(d_kpe_h, d_ks_h, d_vs_h):
        kpe = d_kpe_h[:, 0:64]
        for i in range(1, HEADS):
            kpe = kpe + d_kpe_h[:, 64 * i:64 * (i + 1)]
        ks, vs = d_ks_h[:, 0:128], d_vs_h[:, 0:128]
        for i in range(1, grp):
            ks = ks + d_ks_h[:, 128 * i:128 * (i + 1)]
            vs = vs + d_vs_h[:, 128 * i:128 * (i + 1)]
        return [jnp.concatenate([kpe, jnp.zeros_like(kpe)], axis=1), ks, vs]
    d_kpe2, d_ksr, d_vs = _ew(tag + "_fold_heads", fold,
                              [(d_kpe_heads, "row"), (d_ks_heads, "row"), (d_vs_heads, "row")],
                              [((tok, 128), F32, "row"), ((tok, 128), F32, "row"), ((tok, 128), F32, "row")])

    rin = jnp.concatenate([d_qpe, d_qs, d_ksr, d_kpe2], axis=1)
    rout = _rope(tag + "_d_rope", rin, _swap_halves(rin), cos, sin, sign=-1.0)
    d_qpe_pre, d_qs_pre, d_ks_pre, d_krope = rout[:, :512], rout[:, 512:1024], rout[:, 1024:1152], rout[:, 1152:1216]

    d_qm = jnp.concatenate([dq_mt[:, :, :MLA_NOPE].reshape(tok, HEADS * MLA_NOPE), d_qpe_pre], axis=1)
    d_kvm = jnp.concatenate([dk_mt[:, :, :MLA_NOPE].reshape(tok, HEADS * MLA_NOPE),
                             _from_t4(dv_m4).reshape(tok, HEADS * MLA_V)], axis=1)
    g["mla_w_q_b"] = _mm1(tag + "_d_wqb", s["qn"], d_qm, ta=True)
    d_qn = _mm1(tag + "_d_qn", d_qm, w["mla_w_q_b"], tb=True)
    d_cq, g["mla_q_norm"] = _rms_bwd(tag + "_d_q_norm", d_qn, s["c_q"], w["mla_q_norm"])
    g["mla_w_kv_b"] = _mm1(tag + "_d_wkvb", s["kvn"], d_kvm, ta=True)
    d_kvn = _mm1(tag + "_d_kvn", d_kvm, w["mla_w_kv_b"], tb=True)
    d_ckv, g["mla_kv_norm"] = _rms_bwd(tag + "_d_kv_norm", d_kvn, s["c_kv"], w["mla_kv_norm"])

    dp = jnp.concatenate([d_cq, d_ckv, d_qs_pre, d_ks_pre, d_vs, _unheads(dq_fh),
                          _from_t4(dk_f4).reshape(tok, HEADS * HEAD_DIM), _from_t4(dv_f4).reshape(tok, HEADS * HEAD_DIM),
                          d_krope, dzt.T, jnp.zeros((tok, IN_COLS_PAD - IN_COLS), F32)], axis=1).astype(BF16)
    g["w_in"] = _mm1(tag + "_d_win", s["h2"], dp, ta=True, tn=640)
    dh2 = _mm1(tag + "_d_h2", dp, w["w_in"], tb=True, tk=640)
    dx, g["mix_norm"], dxb = _rms_bwd(tag + "_d_norm", dh2, s["x"], w["mix_norm"], res=dy, also_bf16=True)
    return dx, dxb, g, exchanged


def _loss_head(x, g, target):
    d = x.shape[1]

    def fn(x, g, tgt):
        r = lax.rsqrt(jnp.mean(x * x, axis=-1, keepdims=True) + RMS_EPS)
        xh = x * r
        err = xh * g - tgt
        loss = 0.5 * jnp.sum(jnp.sum(err * err, axis=-1, keepdims=True), axis=0, keepdims=True) / d
        dy = err / d
        dxh = dy * g
        dx = r * (dxh - xh * jnp.mean(dxh * xh, axis=-1, keepdims=True))
        return [dx, jnp.sum(dy * xh, axis=0, keepdims=True), loss, dx]
    return _ew("loss_head", fn, [(x, "row"), (g, "full"), (target, "row")],
               [(x.shape, F32, "row"), (g.shape, F32, "acc"), ((1, 1), F32, "acc"), (x.shape, BF16, "row")])


def _adam_update(w, g, m, v):
    m = ADAM_B1 * m + (1.0 - ADAM_B1) * g
    v = ADAM_B2 * v + (1.0 - ADAM_B2) * (g * g)
    m_hat = m / (1.0 - ADAM_B1 ** ADAM_STEP)
    v_hat = v / (1.0 - ADAM_B2 ** ADAM_STEP)
    return [-ADAM_LR * (m_hat / (jnp.sqrt(v_hat) + ADAM_EPS) + ADAM_WD * w), m, v]


def _adamw(name, w, g, m, v):
    return _ew(name, _adam_update, [(w, "row"), (g, "row"), (m, "row"), (v, "row")], [(w.shape, F32, "row")] * 3)


def _adamw_slots(name, r, w, m, v):
    def fn(r, w, m, v):
        g = r[0].astype(F32)
        for i in range(1, r.shape[0]):
            g = g + r[i].astype(F32)
        return [g] + _adam_update(w, g, m, v)
    return _ew(name, fn, [(r, "row"), (w, "row"), (m, "row"), (v, "row")], [(w.shape, F32, "row")] * 4)


def _pair_sum(name, g, layer, from_sibling, core):
    _, _, rows, cols = g.shape
    tm = _rows(rows, 512)

    def body(c_ref, g_ref, s_ref, o_ref):
        o_ref[...] = (g_ref[...].astype(F32) + s_ref[...].astype(F32)).astype(BF16)

    return pl.pallas_call(
        body, name=name,
        grid_spec=pltpu.PrefetchScalarGridSpec(
            num_scalar_prefetch=1, grid=(N_CHIP, rows // tm),
            in_specs=[pl.BlockSpec((None, None, tm, cols), lambda ch, i, c: (2 * ch + c[0], layer, i, 0)),
                      pl.BlockSpec((None, tm, cols), lambda ch, i, c: (ch, i, 0))],
            out_specs=pl.BlockSpec((None, tm, cols), lambda ch, i, c: (ch, i, 0))),
        out_shape=jax.ShapeDtypeStruct((N_CHIP, rows, cols), BF16),
        compiler_params=_params(dimension_semantics=("parallel", "parallel")),
    )(core, g, from_sibling)


def _chip_sums(blocks, layer):
    names = list(blocks)
    core = lax.axis_index("c").astype(jnp.int32).reshape(1)
    from_sibling = _pair_exchange([blocks[k] for k in names], "l%d" % layer)
    return [_pair_sum("pair_sum_l%d_%s" % (layer, k), blocks[k][0], blocks[k][1], s, core)
            for k, s in zip(names, from_sibling)]


def _coords(dev):
    return (dev // 4, (dev // 2) % 2, dev % 2)


def _all_gather(shards):
    n = len(shards)

    def body(*refs):
        x_refs, out_refs = refs[:n], refs[n:2 * n]
        send_sems, recv_sems, local_sems = refs[2 * n:]
        x, y, c = lax.axis_index("x"), lax.axis_index("y"), lax.axis_index("c")
        me, sibling = (x, y, c), (x, y, 1 - c)
        chips = [(1 - x, y), (x, 1 - y), (1 - x, 1 - y)]

        def slot(a, px, py, pc):
            return out_refs[a].at[4 * px + 2 * py + pc]

        def copy(a, k, block, to, src=None):
            return pltpu.make_async_remote_copy(
                src_ref=slot(a, *block) if src is None else src, dst_ref=slot(a, *block),
                send_sem=send_sems.at[a, k], recv_sem=recv_sems.at[a, k],
                device_id=to, device_id_type=pl.DeviceIdType.MESH)

        mine = [pltpu.make_async_copy(x_refs[a], slot(a, *me), local_sems.at[a]) for a in range(n)]
        first, passed = [], []
        for a in range(n):
            mine[a].start()
            first.append(copy(a, 0, me, sibling, src=x_refs[a]))
            first += [copy(a, 1 + j, me, (*chip, c), src=x_refs[a]) for j, chip in enumerate(chips)]
        for cp in first:
            cp.start()
        for a in range(n):
            for j, chip in enumerate(chips):
                copy(a, 1 + j, (*chip, c), me).wait_recv()
                passed.append(copy(a, 4 + j, (*chip, c), sibling))
                passed[-1].start()
        for a in range(n):
            copy(a, 0, sibling, me).wait_recv()
            for j, chip in enumerate(chips):
                copy(a, 4 + j, (*chip, 1 - c), me).wait_recv()
        for cp in first + passed:
            cp.wait_send()
        for a in range(n):
            mine[a].wait()

    return pl.pallas_call(
        body, name="all_gather_weights",
        out_shape=[jax.ShapeDtypeStruct((N_DEV,) + s.shape, s.dtype) for s in shards],
        in_specs=[pl.BlockSpec(memory_space=pl.ANY)] * n,
        out_specs=[pl.BlockSpec(memory_space=pl.ANY)] * n,
        scratch_shapes=[pltpu.SemaphoreType.DMA((n, 7)), pltpu.SemaphoreType.DMA((n, 7)),
                        pltpu.SemaphoreType.DMA((n,))],
    )(*shards)


def _pair_exchange(blocks, tag):
    n = len(blocks)
    layers = [li for _, li in blocks]
    arrays = [b for b, _ in blocks]

    def body(*refs):
        g_refs, out_refs = refs[:n], refs[n:2 * n]
        send_sems, recv_sems = refs[2 * n:]
        x, y, c = lax.axis_index("x"), lax.axis_index("y"), lax.axis_index("c")
        copies = [pltpu.make_async_remote_copy(
            src_ref=g_refs[a].at[2 * chip + (1 - c), layers[a]], dst_ref=out_refs[a].at[chip],
            send_sem=send_sems.at[a, chip], recv_sem=recv_sems.at[a, chip],
            device_id=(x, y, 1 - c), device_id_type=pl.DeviceIdType.MESH)
            for a in range(n) for chip in range(N_CHIP)]
        for cp in copies:
            cp.start()
        for cp in copies:
            cp.wait()

    return pl.pallas_call(
        body, name="pair_exchange_grads_" + tag,
        out_shape=[jax.ShapeDtypeStruct((N_CHIP,) + b.shape[2:], b.dtype) for b in arrays],
        in_specs=[pl.BlockSpec(memory_space=pl.ANY)] * n,
        out_specs=[pl.BlockSpec(memory_space=pl.ANY)] * n,
        scratch_shapes=[pltpu.SemaphoreType.DMA((n, N_CHIP)), pltpu.SemaphoreType.DMA((n, N_CHIP))],
    )(*arrays)


class _ChipExchange:
    def __init__(self, sums, layer, prev=None):
        self.sums, self.layer, self.prev = list(sums), layer, None if prev is None else list(prev)
        self.n = len(self.sums)

    def specs(self):
        n = self.n
        args = self.sums + (self.prev or [])
        any_spec = pl.BlockSpec(memory_space=pl.ANY)
        out_shape = [jax.ShapeDtypeStruct((N_CHIP, DEPTH) + s.shape[1:], s.dtype) for s in self.sums]
        scratch = [pltpu.SemaphoreType.DMA((n, N_CHIP)), pltpu.SemaphoreType.DMA((n, N_CHIP)),
                   pltpu.SemaphoreType.DMA((n,))]
        aliases = {n + a: a for a in range(n)} if self.prev else {}
        return args, [any_spec] * len(args), [any_spec] * n, out_shape, scratch, aliases

    def ops(self, in_refs, out_refs, sem_refs):
        n, layer = self.n, self.layer
        g_refs = in_refs[:n]
        send_sems, recv_sems, local_sems = sem_refs
        x, y, c = lax.axis_index("x"), lax.axis_index("y"), lax.axis_index("c")
        my_chip = 2 * x + y

        def copy(a, chip):
            return pltpu.make_async_remote_copy(
                src_ref=g_refs[a].at[chip], dst_ref=out_refs[a].at[my_chip, layer],
                send_sem=send_sems.at[a, chip], recv_sem=recv_sems.at[a, my_chip],
                device_id=(chip // 2, chip % 2, c), device_id_type=pl.DeviceIdType.MESH)

        def arrival(a, chip):
            return pltpu.make_async_remote_copy(
                src_ref=g_refs[a].at[chip], dst_ref=out_refs[a].at[chip, layer],
                send_sem=send_sems.at[a, chip], recv_sem=recv_sems.at[a, chip],
                device_id=(chip // 2, chip % 2, c), device_id_type=pl.DeviceIdType.MESH)

        def mine(a):
            return pltpu.make_async_copy(g_refs[a].at[my_chip], out_refs[a].at[my_chip, layer], local_sems.at[a])

        def others(step):
            for chip in range(N_CHIP):
                @pl.when(chip != my_chip)
                def _(chip=chip):
                    for a in range(n):
                        step(a, chip)

        def start():
            for a in range(n):
                mine(a).start()
            others(lambda a, chip: copy(a, chip).start())

        def finish():
            others(lambda a, chip: arrival(a, chip).wait_recv())
            others(lambda a, chip: copy(a, chip).wait_send())
            for a in range(n):
                mine(a).wait()

        return start, finish

    def call(self):
        args, in_specs, out_specs, out_shape, scratch, aliases = self.specs()
        n_in = len(args)

        def body(*refs):
            start, finish = self.ops(refs[:n_in], refs[n_in:n_in + self.n], refs[n_in + self.n:])
            start()
            finish()

        return pl.pallas_call(
            body, name="chip_exchange_grads_l%d" % self.layer, out_shape=out_shape, in_specs=in_specs,
            out_specs=out_specs, scratch_shapes=scratch, input_output_aliases=aliases,
        )(*args)


def _all_reduce_small(buf):
    def body(x_ref, out_ref, slots, send_sems, recv_sems):
        me = 4 * lax.axis_index("x") + 2 * lax.axis_index("y") + lax.axis_index("c")

        def copy(peer):
            return pltpu.make_async_remote_copy(
                src_ref=x_ref, dst_ref=slots.at[me],
                send_sem=send_sems.at[peer], recv_sem=recv_sems.at[me],
                device_id=_coords(peer), device_id_type=pl.DeviceIdType.MESH)

        def arrival(peer):
            return pltpu.make_async_remote_copy(
                src_ref=x_ref, dst_ref=slots.at[peer],
                send_sem=send_sems.at[peer], recv_sem=recv_sems.at[peer],
                device_id=_coords(peer), device_id_type=pl.DeviceIdType.MESH)

        slots[pl.ds(me, 1)] = x_ref[...][None]
        for peer in range(N_DEV):
            @pl.when(peer != me)
            def _(peer=peer):
                copy(peer).start()
        for peer in range(N_DEV):
            @pl.when(peer != me)
            def _(peer=peer):
                arrival(peer).wait_recv()
        for peer in range(N_DEV):
            @pl.when(peer != me)
            def _(peer=peer):
                copy(peer).wait_send()
        acc = slots[0]
        for peer in range(1, N_DEV):
            acc = acc + slots[peer]
        out_ref[...] = acc

    return pl.pallas_call(
        body, name="all_reduce_small",
        out_shape=jax.ShapeDtypeStruct(buf.shape, F32),
        in_specs=[pl.BlockSpec(memory_space=pltpu.VMEM)],
        out_specs=pl.BlockSpec(memory_space=pltpu.VMEM),
        scratch_shapes=[pltpu.VMEM((N_DEV,) + buf.shape, F32), pltpu.SemaphoreType.DMA((N_DEV,)),
                        pltpu.SemaphoreType.DMA((N_DEV,))],
        compiler_params=_params(),
    )(buf)


def _in_to_kernel(w):
    pad = jnp.zeros(w.shape[:-1] + (IN_COLS_PAD - IN_COLS,), w.dtype)
    return jnp.concatenate([w[..., :768], w[..., 832:3136], w[..., 768:832], w[..., 3136:], pad], axis=-1)


def _in_from_kernel(w):
    return jnp.concatenate([w[..., :768], w[..., 3072:3136], w[..., 768:3072], w[..., 3136:3144]], axis=-1)


def _split_to_kernel(w, a, b):
    r = w.shape[0]
    w3 = w.reshape(r, HEADS, a + b)
    return jnp.concatenate([w3[:, :, :a].reshape(r, HEADS * a), w3[:, :, a:].reshape(r, HEADS * b)], axis=1)


def _split_from_kernel(w, a, b):
    r = w.shape[0]
    return jnp.concatenate([w[:, :HEADS * a].reshape(r, HEADS, a), w[:, HEADS * a:].reshape(r, HEADS, b)],
                           axis=-1).reshape(r, HEADS * (a + b))


MIXER_BIG = ("w_in", "mla_w_q_b", "mla_w_kv_b", "w_out")


def _col_blocks_to_full(blk):
    nb, k, n = blk.shape
    return blk.transpose(1, 0, 2).reshape(k, nb * n)


def _full_to_col_blocks(g):
    k, c = g.shape
    return g.reshape(k, N_DEV, c // N_DEV).transpose(1, 0, 2).astype(BF16)


def _local_step(x, positions, target, gathered, small, exchange=False):
    inv_freq = ROPE_THETA ** (-jnp.arange(0, HEAD_DIM, 2, dtype=F32) / HEAD_DIM)
    cos64, sin64 = _rope_tables(positions, jnp.concatenate([-inv_freq, inv_freq])[None, :])
    cos = jnp.tile(cos64, (1, 20))
    sin = jnp.tile(sin64, (1, 20))

    layers = []
    for l in range(DEPTH):
        w = {}
        w["w_in"] = _in_to_kernel(_col_blocks_to_full(gathered["w_in"][:, l]))
        w["mla_w_q_b"] = _split_to_kernel(_col_blocks_to_full(gathered["mla_w_q_b"][:, l]), MLA_NOPE, MLA_ROPE)
        w["mla_w_kv_b"] = _split_to_kernel(_col_blocks_to_full(gathered["mla_w_kv_b"][:, l]), MLA_NOPE, MLA_V)
        blk = gathered["w_out"][:, l]
        w["w_out"] = blk.reshape(blk.shape[0] * blk.shape[1], blk.shape[2])
        for k in SMALL:
            if k != "final_norm":
                w[k] = small[k][l][None, :] if "norm" in k else small[k][l]
        layers.append(w)

    def ffn_w(f):
        return gathered[f + "_w_gate"], gathered[f + "_w_up"], gathered[f + "_w_down"]

    saved = []
    for l, w in enumerate(layers):
        t = "l%d" % l
        x, s1 = _ffn_fwd(t + "_ffn1", x, w["ffn1_norm"], *ffn_w("ffn1"), l)
        x, s2 = _mixer_fwd(t + "_mix", x, w, cos, sin)
        x, s3 = _ffn_fwd(t + "_ffn2", x, w["ffn2_norm"], *ffn_w("ffn2"), l)
        saved.append((s1, s2, s3))

    dx, d_final, loss, dxb = _loss_head(x, small["final_norm"][None, :], target)

    mixer_g = [None] * DEPTH
    small_g = [None] * DEPTH
    ffn_g = {"ffn1": None, "ffn2": None}
    pending = received = None
    for l in reversed(range(DEPTH)):
        w, (s1, s2, s3) = layers[l], saved[l]
        t = "l%d" % l
        sg = {}
        dx, dxb, sg["ffn2_norm"], ffn_g["ffn2"] = _ffn_bwd(t + "_ffn2", dx, dxb, s3, w["ffn2_norm"], *ffn_w("ffn2"), l,
                                                           ffn_g["ffn2"])
        dx, dxb, g, exchanged = _mixer_bwd(t + "_mix", dx, dxb, s2, w, cos, sin, side=pending)
        if pending is not None:
            received, pending = exchanged, None
        dx, dxb, sg["ffn1_norm"], ffn_g["ffn1"] = _ffn_bwd(t + "_ffn1", dx, dxb, s1, w["ffn1_norm"], *ffn_w("ffn1"), l,
                                                           ffn_g["ffn1"])
        mixer_g[l] = {
            "w_in": _full_to_col_blocks(_in_from_kernel(g["w_in"])),
            "mla_w_q_b": _full_to_col_blocks(_split_from_kernel(g["mla_w_q_b"], MLA_NOPE, MLA_ROPE)),
            "mla_w_kv_b": _full_to_col_blocks(_split_from_kernel(g["mla_w_kv_b"], MLA_NOPE, MLA_V)),
            "w_out": g["w_out"].astype(BF16).reshape(N_DEV, -1, g["w_out"].shape[1]),
        }
        for k in SMALL:
            if k in g:
                sg[k] = g[k]
        small_g[l] = {k: v.reshape(-1) for k, v in sg.items()}
        if exchange:
            blocks = {k: (mixer_g[l][k][:, None], 0) for k in MIXER_BIG}
            for f in ("ffn1", "ffn2"):
                for name, buf in zip(("_w_gate", "_w_up", "_w_down"), ffn_g[f]):
                    blocks[f + name] = (buf, l)
            pending = _ChipExchange(_chip_sums({k: blocks[k] for k in BIG}, l), l, received)

    sm = {k: jnp.stack([small_g[l][k] for l in range(DEPTH)]) for k in small_g[0]}
    sm["final_norm"] = d_final.reshape(-1)
    if exchange:
        return loss, dx, dict(zip(BIG, pending.call())), sm
    big = {k: jnp.stack([mixer_g[l][k] for l in range(DEPTH)], axis=1) for k in MIXER_BIG}
    for f in ("ffn1", "ffn2"):
        big[f + "_w_gate"], big[f + "_w_up"], big[f + "_w_down"] = ffn_g[f]
    return loss, dx, big, sm


def _flat_rows(a):
    return a.reshape(-1, LANES)


def kernel(x, positions, ffn1_norm, ffn1_w_gate, ffn1_w_up, ffn1_w_down, mix_norm, w_in, mla_q_norm, mla_w_q_b, mla_kv_norm, mla_w_kv_b, swa_sinks, fox_forget_bias, w_out, ffn2_norm, ffn2_w_gate, ffn2_w_up, ffn2_w_down, final_norm, loss_target, m_ffn1_norm, m_ffn1_w_gate, m_ffn1_w_up, m_ffn1_w_down, m_mix_norm, m_w_in, m_mla_q_norm, m_mla_w_q_b, m_mla_kv_norm, m_mla_w_kv_b, m_swa_sinks, m_fox_forget_bias, m_w_out, m_ffn2_norm, m_ffn2_w_gate, m_ffn2_w_up, m_ffn2_w_down, m_final_norm, v_ffn1_norm, v_ffn1_w_gate, v_ffn1_w_up, v_ffn1_w_down, v_mix_norm, v_w_in, v_mla_q_norm, v_mla_w_q_b, v_mla_kv_norm, v_mla_w_kv_b, v_swa_sinks, v_fox_forget_bias, v_w_out, v_ffn2_norm, v_ffn2_w_gate, v_ffn2_w_up, v_ffn2_w_down, v_final_norm):
    given = dict(locals())
    weights = {k: given[k] for k in WEIGHTS}
    mom_m = {k: given["m_" + k] for k in WEIGHTS}
    mom_v = {k: given["v_" + k] for k in WEIGHTS}

    gathered = dict(zip(BIG, _all_gather([weights[k].astype(BF16) for k in BIG])))

    small = {k: weights[k] for k in SMALL}
    loss, grad_x, received, small_g = _local_step(x[0], positions[0][:, None], loss_target[0], gathered, small,
                                                  exchange=True)

    small_sizes = [weights[k].size for k in SMALL]
    sbuf = jnp.concatenate([small_g[k].reshape(-1) for k in SMALL] + [loss.reshape(-1)])
    pad = (-sbuf.size) % (8 * LANES)
    sbuf = jnp.concatenate([sbuf, jnp.zeros((pad,), F32)])
    stot = _all_reduce_small(_flat_rows(sbuf)).reshape(-1)

    grad_w, delta, new_m, new_v = {}, {}, {}, {}
    for k in BIG:
        shape = weights[k].shape
        two_d = (-1, shape[-1])
        g, d, m, v = _adamw_slots("adamw_" + k, received[k].reshape(N_CHIP, -1, shape[-1]), weights[k].reshape(two_d),
                                  mom_m[k].reshape(two_d), mom_v[k].reshape(two_d))
        grad_w[k], delta[k], new_m[k], new_v[k] = (t.reshape(shape) for t in (g, d, m, v))
    off = 0
    for k, n in zip(SMALL, small_sizes):
        shape = weights[k].shape
        two_d = (-1, shape[-1]) if len(shape) > 1 else (1, -1)
        grad_w[k] = stot[off:off + n].reshape(shape)
        off += n
        d, m, v = _adamw("adamw_" + k, weights[k].reshape(two_d), grad_w[k].reshape(two_d),
                         mom_m[k].reshape(two_d), mom_v[k].reshape(two_d))
        delta[k], new_m[k], new_v[k] = d.reshape(shape), m.reshape(shape), v.reshape(shape)
    loss_total = stot[off]

    return (loss_total, grad_x[None], *[grad_w[k] for k in WEIGHTS], *[delta[k] for k in WEIGHTS],
            *[new_m[k] for k in WEIGHTS], *[new_v[k] for k in WEIGHTS])
```

```python
import jax
import jax.numpy as jnp
from jax import lax
from jax.experimental import pallas as pl
from jax.experimental.pallas import tpu as pltpu

F32 = jnp.float32
BF16 = jnp.bfloat16

N_DEV = 8
N_CHIP = 4
DEPTH = 2
RMS_EPS = 1e-6
ROPE_THETA = 10000.0
HEADS = 8
MLA_Q_LORA = 512
MLA_KV_LORA = 256
MLA_NOPE = 128
MLA_ROPE = 64
MLA_V = 128
SWA_KV_HEADS = 2
HEAD_DIM = 64
WINDOW = 128
IN_COLS = 3144
IN_COLS_PAD = 3200

ADAM_LR = 0.001
ADAM_B1 = 0.9
ADAM_B2 = 0.999
ADAM_EPS = 1e-08
ADAM_WD = 0.01
ADAM_STEP = 10

LANES = 128
NEG = -1e30
LOG2E = 1.4426950408889634
LN2 = 0.6931471805599453
VMEM_LIMIT_BYTES = 48 * 1024 * 1024

EW_ROWS = 256
MM_TM = 1024
MM_TN = 1024
MM_TK = 1024
ATT_T = 512

BIG = ("ffn1_w_gate", "ffn1_w_up", "ffn1_w_down", "w_in", "mla_w_q_b", "mla_w_kv_b", "w_out",
       "ffn2_w_gate", "ffn2_w_up", "ffn2_w_down")
SMALL = ("ffn1_norm", "mix_norm", "mla_q_norm", "mla_kv_norm", "swa_sinks", "fox_forget_bias", "ffn2_norm",
         "final_norm")
WEIGHTS = ("ffn1_norm", "ffn1_w_gate", "ffn1_w_up", "ffn1_w_down", "mix_norm", "w_in", "mla_q_norm", "mla_w_q_b",
           "mla_kv_norm", "mla_w_kv_b", "swa_sinks", "fox_forget_bias", "w_out", "ffn2_norm", "ffn2_w_gate",
           "ffn2_w_up", "ffn2_w_down", "final_norm")


def _params(**kw):
    return pltpu.CompilerParams(vmem_limit_bytes=VMEM_LIMIT_BYTES, **kw)


def _tile(n, want):
    if n <= want:
        return n
    t = (want // LANES) * LANES
    while n % t:
        t -= LANES
    return t


def _rows(n, want):
    if n <= want:
        return n
    t = (want // 8) * 8
    while n % t:
        t -= 8
    return t


def _ew(name, fn, ins, outs, tm=EW_ROWS):
    tok = None
    for a, kind in ins:
        if kind == "row":
            tok = a.shape[-2]
    tm = _rows(tok, tm)
    steps = tok // tm

    def spec(shape, kind):
        if kind == "row":
            ax = len(shape) - 2
            blk = tuple(tm if d == ax else s for d, s in enumerate(shape))
            return pl.BlockSpec(blk, lambda i, ax=ax, n=len(shape): tuple(i if d == ax else 0 for d in range(n)))
        return pl.BlockSpec(tuple(shape), lambda i, n=len(shape): (0,) * n)

    n_in = len(ins)
    kinds = [k for _, _, k in outs]

    def body(*refs):
        vals = fn(*[r[...] for r in refs[:n_in]])
        for r, v, kind in zip(refs[n_in:], vals, kinds):
            if kind == "row":
                r[...] = v.astype(r.dtype)
            else:
                @pl.when(pl.program_id(0) == 0)
                def _(r=r):
                    r[...] = jnp.zeros(r.shape, r.dtype)
                r[...] += v.astype(r.dtype)

    res = pl.pallas_call(
        body, name=name, grid=(steps,),
        in_specs=[spec(a.shape, k) for a, k in ins],
        out_specs=[spec(s, k) for s, _, k in outs],
        out_shape=[jax.ShapeDtypeStruct(tuple(s), d) for s, d, _ in outs],
        compiler_params=_params(dimension_semantics=("arbitrary",)),
    )(*[a for a, _ in ins])
    return res


def _rms_fwd(name, x, g):
    def fn(x, g):
        r = lax.rsqrt(jnp.mean(x * x, axis=-1, keepdims=True) + RMS_EPS)
        return [x * r * g]
    return _ew(name, fn, [(x, "row"), (g, "full")], [(x.shape, BF16, "row")])[0]


def _rms_bwd(name, dh, x, g, res=None, also_bf16=False):
    def fn(dh, x, g, *rest):
        dh = dh.astype(F32)
        r = lax.rsqrt(jnp.mean(x * x, axis=-1, keepdims=True) + RMS_EPS)
        xh = x * r
        dxh = dh * g
        dx = r * (dxh - xh * jnp.mean(dxh * xh, axis=-1, keepdims=True))
        if rest:
            dx = dx + rest[0]
        return [dx, jnp.sum(dh * xh, axis=0, keepdims=True)] + ([dx] if also_bf16 else [])
    ins = [(dh, "row"), (x, "row"), (g, "full")] + ([(res, "row")] if res is not None else [])
    outs = [(x.shape, F32, "row"), (g.shape, F32, "acc")] + ([(x.shape, BF16, "row")] if also_bf16 else [])
    return _ew(name, fn, ins, outs)


def _rope(name, x, xs, cos, sin, sign=1.0):
    def fn(x, xs, c, s):
        return [x * c + sign * (xs * s)]
    return _ew(name, fn, [(x, "row"), (xs, "row"), (cos, "row"), (sin, "row")], [(x.shape, F32, "row")])[0]


def _rope_tables(positions, inv_freq2):
    def fn(pos, f):
        ang = pos.astype(F32) * f
        return [jnp.cos(ang), jnp.sin(ang)]
    t = positions.shape[0]
    return _ew("rope_tables", fn, [(positions, "row"), (inv_freq2, "full")],
               [((t, 2 * 32), F32, "row"), ((t, 2 * 32), F32, "row")])


def _mm(name, lhs, rhs, terms, epi, out_dtypes, extras=(), ta=False, tb=False, tm=MM_TM, tn=MM_TN, tk=MM_TK):
    if ta:
        kdim, m = lhs[0].shape
    else:
        m, kdim = lhs[0].shape
    n = rhs[0].shape[0] if tb else rhs[0].shape[1]
    tm, tn, tk = _tile(m, tm), _tile(n, tn), _tile(kdim, tk)
    nk = kdim // tk
    n_acc = 1 + max(a for _, _, a in terms)
    nl, nr, ne = len(lhs), len(rhs), len(extras)
    dims = (((0 if ta else 1,), (1 if tb else 0,)), ((), ()))

    def body(*refs):
        l_refs, r_refs = refs[:nl], refs[nl:nl + nr]
        e_refs = refs[nl + nr:nl + nr + ne]
        o_refs = refs[nl + nr + ne:len(refs) - n_acc]
        accs = refs[len(refs) - n_acc:]
        k = pl.program_id(2)

        @pl.when(k == 0)
        def _():
            for acc in accs:
                acc[...] = jnp.zeros(acc.shape, F32)

        lv, rv = {}, {}
        for li, ri, ai in terms:
            if li not in lv:
                lv[li] = l_refs[li][...].astype(BF16)
            if ri not in rv:
                rv[ri] = r_refs[ri][...].astype(BF16)
            accs[ai][...] += lax.dot_general(lv[li], rv[ri], dims, preferred_element_type=F32)

        @pl.when(k == nk - 1)
        def _():
            outs = epi([acc[...] for acc in accs], [e[...] for e in e_refs])
            for o, v in zip(o_refs, outs):
                o[...] = v.astype(o.dtype)

    l_spec = pl.BlockSpec((tk, tm), lambda i, j, k: (k, i)) if ta else pl.BlockSpec((tm, tk), lambda i, j, k: (i, k))
    r_spec = pl.BlockSpec((tn, tk), lambda i, j, k: (j, k)) if tb else pl.BlockSpec((tk, tn), lambda i, j, k: (k, j))
    o_spec = pl.BlockSpec((tm, tn), lambda i, j, k: (i, j))
    return pl.pallas_call(
        body, name=name, grid=(m // tm, n // tn, nk),
        in_specs=[l_spec] * nl + [r_spec] * nr + [o_spec] * ne,
        out_specs=[o_spec] * len(out_dtypes),
        out_shape=[jax.ShapeDtypeStruct((m, n), d) for d in out_dtypes],
        scratch_shapes=[pltpu.VMEM((tm, tn), F32)] * n_acc,
        compiler_params=_params(dimension_semantics=("parallel", "parallel", "arbitrary")),
    )(*lhs, *rhs, *extras)


def _mm1(name, a, b, out_dtype=F32, scale=None, add=None, **kw):
    def epi(accs, ex):
        v = accs[0] if scale is None else accs[0] * scale
        return [v + ex[0] if ex else v]
    return _mm(name, [a], [b], [(0, 0, 0)], epi, [out_dtype], extras=[] if add is None else [add], **kw)[0]


def _sink_grad(name, sink3, lse, delta):
    def fn(sk, lse, dl):
        return [-jnp.sum(jnp.exp2(sk * LOG2E - lse) * dl, axis=1, keepdims=True)]
    return _ew(name, fn, [(sink3, "full"), (lse, "row"), (delta, "row")], [(sink3.shape, F32, "acc")], tm=512)[0]


def _resident(block, index_map):
    return pl.BlockSpec(block, index_map, pipeline_mode=pl.Buffered(1))


def _attn_fwd(name, q_t, k, v_t4, scale, group=1, cq_row=None, ck_col=None, sink=None, window=False, t=ATT_T):
    h_n, dq, tok = q_t.shape
    dv = v_t4.shape[2]
    t = min(t, tok)
    nq = tok // t
    bias = cq_row is not None
    has_sink = sink is not None

    def body(*refs):
        q_ref, k_ref, v_ref = refs[:3]
        pos = 3
        cq_ref = ck_ref = sink_ref = None
        if bias:
            cq_ref, ck_ref = refs[pos], refs[pos + 1]
            pos += 2
        if has_sink:
            sink_ref = refs[pos]
            pos += 1
        o_ref, lse_ref, m_s, l_s, acc_s = refs[pos:]
        h, qi = pl.program_id(0), pl.program_id(1)
        qs = (q_ref[...].astype(F32) * (scale * LOG2E)).astype(BF16)
        m_s[...] = jnp.full(m_s.shape, sink_ref[h] * LOG2E if has_sink else NEG, F32)
        l_s[...] = jnp.full(l_s.shape, 1.0 if has_sink else 0.0, F32)
        acc_s[...] = jnp.zeros(acc_s.shape, F32)
        c_ref = cq_ref[:, 0:1] if bias else None

        def scores(j, masked):
            rows = pl.ds(pl.multiple_of(j * t, t), t)
            s = jnp.dot(k_ref[rows, :], qs, preferred_element_type=F32)
            if bias:
                s = s - (ck_ref[rows, :] - c_ref)
            if masked:
                kpos = j * t + lax.broadcasted_iota(jnp.int32, s.shape, 0)
                qpos = qi * t + lax.broadcasted_iota(jnp.int32, s.shape, 1)
                mask = kpos <= qpos
                if window:
                    mask = mask & (kpos > qpos - WINDOW)
                s = jnp.where(mask, s, NEG)
            return s

        def step(tiles):
            ss = [scores(j, masked) for j, masked in tiles]
            m_prev = m_s[...]
            m_new = m_prev
            for s in ss:
                m_new = jnp.maximum(m_new, jnp.max(s, axis=0, keepdims=True))
            alpha = jnp.exp2(m_prev - m_new)
            l_new, acc = alpha * l_s[...], alpha * acc_s[...]
            for (j, _), s in zip(tiles, ss):
                p = jnp.exp2(s - m_new)
                l_new = l_new + jnp.sum(p, axis=0, keepdims=True)
                acc = acc + jnp.dot(v_ref[j], p.astype(BF16), preferred_element_type=F32)
            l_s[...], acc_s[...], m_s[...] = l_new, acc, m_new

        if window:
            pl.when(qi > 0)(lambda: step([(qi - 1, True), (qi, True)]))
            pl.when(qi == 0)(lambda: step([(qi, True)]))
        else:
            def below(i, carry):
                step([(2 * i, False), (2 * i + 1, False)])
                return carry
            lax.fori_loop(0, qi // 2, below, 0)
            pl.when(qi % 2 == 1)(lambda: step([(qi - 1, False), (qi, True)]))
            pl.when(qi % 2 == 0)(lambda: step([(qi, True)]))
        o_ref[...] = (acc_s[...] / l_s[...]).astype(o_ref.dtype)
        lse_ref[...] = m_s[...] + jnp.log(l_s[...]) * LOG2E

    nk = tok // t
    in_specs = [
        pl.BlockSpec((None, dq, t), lambda h, qi: (h, 0, qi)),
        _resident((None, tok, dq), lambda h, qi: (h // group, 0, 0)),
        _resident((None, nk, dv, t), lambda h, qi: (h // group, 0, 0, 0)),
    ]
    args = [q_t, k, v_t4]
    if bias:
        in_specs += [pl.BlockSpec((None, 1, t), lambda h, qi: (h, 0, qi)),
                     _resident((None, tok, 1), lambda h, qi: (h, 0, 0))]
        args += [cq_row, ck_col]
    if has_sink:
        in_specs.append(pl.BlockSpec(memory_space=pltpu.SMEM))
        args.append(sink)
    return pl.pallas_call(
        body, name=name, grid=(h_n, nq),
        in_specs=in_specs,
        out_specs=[pl.BlockSpec((None, dv, t), lambda h, qi: (h, 0, qi)),
                   pl.BlockSpec((None, 1, t), lambda h, qi: (h, 0, qi))],
        out_shape=[jax.ShapeDtypeStruct((h_n, dv, tok), BF16), jax.ShapeDtypeStruct((h_n, 1, tok), F32)],
        scratch_shapes=[pltpu.VMEM((1, t), F32), pltpu.VMEM((1, t), F32), pltpu.VMEM((dv, t), F32)],
        compiler_params=_params(dimension_semantics=("parallel", "parallel")),
    )(*args)


def _attn_bwd(name, q, q_t, k, k_t4, v_t4, do, do_t, lse, delta, scale, group=1, cq_col=None, ck_row4=None,
              window=False, t=ATT_T, side=None):
    h_n, tok, dq = q.shape
    dv = do.shape[2]
    t = min(t, tok)
    nq = tok // t
    bias = cq_col is not None

    n_in = 9 + (2 if bias else 0)
    n_out = 3 + (2 if bias else 0)
    side_args, side_in_specs, side_out_specs, side_out_shape, side_scratch, side_aliases = (
        side.specs() if side is not None else ([], [], [], [], [], {}))

    def body(*refs):
        q_ref, qt_ref, k_ref, kt_ref, vt_ref, do_ref, dot_ref, lse_ref, dl_ref = refs[:9]
        cq_ref = ck_ref = dc_ref = dr_ref = None
        if bias:
            cq_ref, ck_ref = refs[9], refs[10]
        pos = n_in + len(side_args)
        dq_ref, dk_ref, dv_ref = refs[pos:pos + 3]
        if bias:
            dc_ref, dr_ref = refs[pos + 3], refs[pos + 4]
        h, qi = pl.program_id(0), pl.program_id(1)
        if side is not None:
            side_start, side_finish = side.ops(refs[n_in:pos], refs[pos + n_out:pos + n_out + side.n],
                                               refs[pos + n_out + side.n:])
            pl.when((h == 0) & (qi == 0))(side_start)

        @pl.when(qi == 0)
        def _():
            dk_ref[...] = jnp.zeros(dk_ref.shape, F32)
            dv_ref[...] = jnp.zeros(dv_ref.shape, F32)
            if bias:
                dc_ref[...] = jnp.zeros(dc_ref.shape, F32)

        qs = (q_ref[...].astype(F32) * (scale * LOG2E)).astype(BF16)
        dq_ref[...] = jnp.zeros(dq_ref.shape, F32)
        if bias:
            dr_ref[...] = jnp.zeros(dr_ref.shape, F32)
        c_ref = cq_ref[0:1, :] if bias else None

        def step(j, masked):
            s = jnp.dot(qs, kt_ref[j], preferred_element_type=F32)
            if bias:
                s = s - (ck_ref[j] - c_ref)
            p = jnp.exp2(s - lse_ref[...])
            if masked:
                qpos = qi * t + lax.broadcasted_iota(jnp.int32, s.shape, 0)
                kpos = j * t + lax.broadcasted_iota(jnp.int32, s.shape, 1)
                mask = kpos <= qpos
                if window:
                    mask = mask & (kpos > qpos - WINDOW)
                p = jnp.where(mask, p, 0.0)
            pb = p.astype(BF16)
            dv_ref[j] += jnp.dot(dot_ref[...], pb, preferred_element_type=F32)
            dp = jnp.dot(do_ref[...], vt_ref[j], preferred_element_type=F32)
            ds = p * (dp - dl_ref[...])
            dsb = ds.astype(BF16)
            rows = pl.ds(pl.multiple_of(j * t, t), t)
            dq_ref[...] += jnp.dot(dsb, k_ref[rows, :], preferred_element_type=F32)
            dk_ref[j] += scale * jnp.dot(qt_ref[...], dsb, preferred_element_type=F32)
            if bias:
                dc_ref[j] += jnp.sum(ds, axis=0, keepdims=True)
                dr_ref[...] += jnp.sum(ds, axis=1, keepdims=True)

        if window:
            pl.when(qi > 0)(lambda: step(qi - 1, True))
        else:
            def below(j, carry):
                step(j, False)
                return carry
            lax.fori_loop(0, qi, below, 0)
        step(qi, True)
        dq_ref[...] = dq_ref[...] * scale
        if side is not None:
            pl.when((h == h_n - 1) & (qi == nq - 1))(side_finish)

    nk = nq

    def q_tile(shape_tail):
        return pl.BlockSpec((None, t) + shape_tail, lambda h, qi: (h, qi, 0))

    in_specs = [
        q_tile((dq,)),
        pl.BlockSpec((None, dq, t), lambda h, qi: (h, 0, qi)),
        _resident((None, tok, dq), lambda h, qi: (h // group, 0, 0)),
        _resident((None, nk, dq, t), lambda h, qi: (h // group, 0, 0, 0)),
        _resident((None, nk, dv, t), lambda h, qi: (h // group, 0, 0, 0)),
        q_tile((dv,)),
        pl.BlockSpec((None, dv, t), lambda h, qi: (h, 0, qi)),
        q_tile((1,)),
        q_tile((1,)),
    ]
    args = [q, q_t, k, k_t4, v_t4, do, do_t, lse, delta]
    out_specs = [q_tile((dq,)),
                 pl.BlockSpec((None, nk, dq, t), lambda h, qi: (h, 0, 0, 0)),
                 pl.BlockSpec((None, nk, dv, t), lambda h, qi: (h, 0, 0, 0))]
    out_shape = [jax.ShapeDtypeStruct((h_n, tok, dq), F32), jax.ShapeDtypeStruct((h_n, nk, dq, t), F32),
                 jax.ShapeDtypeStruct((h_n, nk, dv, t), F32)]
    if bias:
        in_specs += [q_tile((1,)), _resident((None, nk, 1, t), lambda h, qi: (h, 0, 0, 0))]
        args += [cq_col, ck_row4]
        out_specs += [pl.BlockSpec((None, nk, 1, t), lambda h, qi: (h, 0, 0, 0)), q_tile((1,))]
        out_shape += [jax.ShapeDtypeStruct((h_n, nk, 1, t), F32), jax.ShapeDtypeStruct((h_n, tok, 1), F32)]
    res = pl.pallas_call(
        body, name=name, grid=(h_n, nq),
        in_specs=in_specs + side_in_specs, out_specs=out_specs + side_out_specs,
        out_shape=out_shape + side_out_shape, scratch_shapes=side_scratch,
        input_output_aliases={n_in + i: n_out + o for i, o in side_aliases.items()},
        compiler_params=_params(dimension_semantics=("arbitrary" if side is not None else "parallel", "arbitrary")),
    )(*args, *side_args)
    return (res[:n_out], res[n_out:]) if side is not None else res


def _delta_t(name, do_t, o_t, tl=1024):
    h_n, dv, tok = do_t.shape
    tl = min(tl, tok)

    def body(do_ref, o_ref, out_ref):
        out_ref[...] = jnp.sum(do_ref[...].astype(F32) * o_ref[...].astype(F32), axis=1, keepdims=True)

    spec = pl.BlockSpec((h_n, dv, tl), lambda i: (0, 0, i))
    return pl.pallas_call(
        body, name=name, grid=(tok // tl,), in_specs=[spec, spec],
        out_specs=pl.BlockSpec((h_n, 1, tl), lambda i: (0, 0, i)),
        out_shape=jax.ShapeDtypeStruct((h_n, 1, tok), F32),
        compiler_params=_params(dimension_semantics=("parallel",)),
    )(do_t, o_t)


def _log_sigmoid(z):
    return jnp.minimum(z, 0.0) - jnp.log(1.0 + jnp.exp(-jnp.abs(z)))


def _gate_fwd(zt, bias):
    tok = zt.shape[1]

    def body(z_ref, b_ref, c_ref):
        x = _log_sigmoid(z_ref[...] + b_ref[...])
        lane = lax.broadcasted_iota(jnp.int32, x.shape, 1)
        k = 1
        while k < tok:
            x = x + jnp.where(lane >= k, pltpu.roll(x, k, axis=1), 0.0)
            k *= 2
        c_ref[...] = x * LOG2E

    return pl.pallas_call(body, name="fox_gate_fwd", out_shape=jax.ShapeDtypeStruct(zt.shape, F32),
                          compiler_params=_params())(zt, bias)


def _gate_bwd(d_rows, d_cols, zt, bias):
    tok = zt.shape[1]

    def body(dr_ref, dc_ref, z_ref, b_ref, dz_ref, db_ref):
        x = dr_ref[...] - dc_ref[...]
        lane = lax.broadcasted_iota(jnp.int32, x.shape, 1)
        k = 1
        while k < tok:
            x = x + jnp.where(lane < tok - k, pltpu.roll(x, tok - k, axis=1), 0.0)
            k *= 2
        dz = x / (1.0 + jnp.exp(z_ref[...] + b_ref[...]))
        dz_ref[...] = dz
        db_ref[...] = jnp.sum(dz, axis=1, keepdims=True)

    return pl.pallas_call(body, name="fox_gate_bwd",
                          out_shape=[jax.ShapeDtypeStruct(zt.shape, F32), jax.ShapeDtypeStruct(bias.shape, F32)],
                          compiler_params=_params())(d_rows, d_cols, zt, bias)


_NT = (((1,), (1,)), ((), ()))
_TN = (((0,), (0,)), ((), ()))


def _ffn_gate_up(name, h, wg, wu, l, tm=512):
    tok, d = h.shape
    nb, n = wg.shape[0], wg.shape[3]
    tm = min(tm, tok)

    def body(h_ref, wg_ref, wu_ref, u_ref, v_ref, a_ref):
        hv = h_ref[...]
        u = jnp.dot(hv, wg_ref[...], preferred_element_type=F32)
        v = jnp.dot(hv, wu_ref[...], preferred_element_type=F32)
        u_ref[...] = u.astype(BF16)
        v_ref[...] = v.astype(BF16)
        a_ref[...] = (u * jax.nn.sigmoid(u) * v).astype(BF16)

    w_spec = pl.BlockSpec((None, None, d, n), lambda j, i: (j, l, 0, 0))
    o_spec = pl.BlockSpec((None, tm, n), lambda j, i: (j, i, 0))
    return pl.pallas_call(
        body, name=name, grid=(nb, tok // tm),
        in_specs=[pl.BlockSpec((tm, d), lambda j, i: (i, 0)), w_spec, w_spec],
        out_specs=[o_spec] * 3, out_shape=[jax.ShapeDtypeStruct((nb, tok, n), BF16)] * 3,
        compiler_params=_params(dimension_semantics=("parallel", "parallel")),
    )(h, wg, wu)


def _ffn_down(name, a, wd, x, l, tm=1024, tn=1024):
    nb, tok, n = a.shape
    d = wd.shape[3]
    tm, tn = min(tm, tok), min(tn, d)

    def body(a_ref, wd_ref, x_ref, y_ref, acc):
        j = pl.program_id(2)

        @pl.when(j == 0)
        def _():
            acc[...] = jnp.zeros(acc.shape, F32)
        acc[...] += jnp.dot(a_ref[...], wd_ref[...], preferred_element_type=F32)

        @pl.when(j == nb - 1)
        def _():
            y_ref[...] = x_ref[...] + 0.5 * acc[...]

    return pl.pallas_call(
        body, name=name, grid=(tok // tm, d // tn, nb),
        in_specs=[pl.BlockSpec((None, tm, n), lambda i, c, j: (j, i, 0)),
                  pl.BlockSpec((None, None, n, tn), lambda i, c, j: (j, l, 0, c)),
                  pl.BlockSpec((tm, tn), lambda i, c, j: (i, c))],
        out_specs=pl.BlockSpec((tm, tn), lambda i, c, j: (i, c)),
        out_shape=jax.ShapeDtypeStruct((tok, d), F32),
        scratch_shapes=[pltpu.VMEM((tm, tn), F32)],
        compiler_params=_params(dimension_semantics=("parallel", "parallel", "arbitrary")),
    )(a, wd, x)


def _ffn_d_act(name, dy, u, v, wd, l, tm=512):
    nb, tok, n = u.shape
    d = dy.shape[1]
    tm = min(tm, tok)

    def body(dy_ref, u_ref, v_ref, wd_ref, du_ref, dv_ref):
        da = 0.5 * lax.dot_general(dy_ref[...], wd_ref[...], _NT, preferred_element_type=F32)
        uv, vv = u_ref[...].astype(F32), v_ref[...].astype(F32)
        sg = jax.nn.sigmoid(uv)
        du_ref[...] = (da * vv * (sg * (1.0 + uv * (1.0 - sg)))).astype(BF16)
        dv_ref[...] = (da * (uv * sg)).astype(BF16)

    t_spec = pl.BlockSpec((None, tm, n), lambda j, i: (j, i, 0))
    return pl.pallas_call(
        body, name=name, grid=(nb, tok // tm),
        in_specs=[pl.BlockSpec((tm, d), lambda j, i: (i, 0)), t_spec, t_spec,
                  pl.BlockSpec((None, None, n, d), lambda j, i: (j, l, 0, 0))],
        out_specs=[t_spec] * 2, out_shape=[jax.ShapeDtypeStruct((nb, tok, n), BF16)] * 2,
        compiler_params=_params(dimension_semantics=("parallel", "parallel")),
    )(dy, u, v, wd)


def _ffn_d_h(name, du, dv, wg, wu, l, tm=1024, side=None):
    nb, tok, n = du.shape
    d = wg.shape[2]
    tm = min(tm, tok)
    ni = tok // tm
    side_args, side_in_specs, side_out_specs, side_out_shape, side_scratch, side_aliases = (
        side.specs() if side is not None else ([], [], [], [], [], {}))
    n_in = 4

    def body(*refs):
        du_ref, dv_ref, wg_ref, wu_ref = refs[:n_in]
        pos = n_in + len(side_args)
        dh_ref, acc = refs[pos], refs[len(refs) - 1]
        i, j = pl.program_id(0), pl.program_id(1)
        if side is not None:
            side_start, side_finish = side.ops(refs[n_in:pos], refs[pos + 1:pos + 1 + side.n],
                                               refs[pos + 1 + side.n:len(refs) - 1])
            pl.when((i == 0) & (j == 0))(side_start)

        @pl.when(j == 0)
        def _():
            acc[...] = jnp.zeros(acc.shape, F32)
        acc[...] += (lax.dot_general(du_ref[...], wg_ref[...], _NT, preferred_element_type=F32)
                     + lax.dot_general(dv_ref[...], wu_ref[...], _NT, preferred_element_type=F32))

        @pl.when(j == nb - 1)
        def _():
            dh_ref[...] = acc[...].astype(BF16)
        if side is not None:
            pl.when((i == ni - 1) & (j == nb - 1))(side_finish)

    t_spec = pl.BlockSpec((None, tm, n), lambda i, j: (j, i, 0))
    w_spec = pl.BlockSpec((None, None, d, n), lambda i, j: (j, l, 0, 0))
    res = pl.pallas_call(
        body, name=name, grid=(ni, nb),
        in_specs=[t_spec, t_spec, w_spec, w_spec] + side_in_specs,
        out_specs=[pl.BlockSpec((tm, d), lambda i, j: (i, 0))] + side_out_specs,
        out_shape=[jax.ShapeDtypeStruct((tok, d), BF16)] + side_out_shape,
        scratch_shapes=side_scratch + [pltpu.VMEM((tm, d), F32)],
        input_output_aliases={n_in + i: 1 + o for i, o in side_aliases.items()},
        compiler_params=_params(dimension_semantics=("arbitrary" if side is not None else "parallel", "arbitrary")),
    )(du, dv, wg, wu, *side_args)
    return (res[0], res[1:]) if side is not None else res[0]


def _ffn_wgrad_in(name, h, du, dv, l, like, prev=None, tk=1024, td=1024):
    tok, d = h.shape
    nb, _, n = du.shape
    tk, td = min(tk, tok), min(td, d)
    nk = tok // tk
    n_in = 3

    def body(*refs):
        h_ref, du_ref, dv_ref = refs[:3]
        og_ref, ou_ref, accg, accu = refs[len(refs) - 4:]
        k = pl.program_id(2)

        @pl.when(k == 0)
        def _():
            accg[...] = jnp.zeros(accg.shape, F32)
            accu[...] = jnp.zeros(accu.shape, F32)
        hv = h_ref[...]
        accg[...] += lax.dot_general(hv, du_ref[...], _TN, preferred_element_type=F32)
        accu[...] += lax.dot_general(hv, dv_ref[...], _TN, preferred_element_type=F32)

        @pl.when(k == nk - 1)
        def _():
            og_ref[...] = accg[...].astype(BF16)
            ou_ref[...] = accu[...].astype(BF16)

    t_spec = pl.BlockSpec((None, tk, n), lambda j, c, k: (j, k, 0))
    o_spec = pl.BlockSpec((None, None, td, n), lambda j, c, k: (j, l, c, 0))
    in_specs = [pl.BlockSpec((tk, td), lambda j, c, k: (k, c)), t_spec, t_spec]
    args = [h, du, dv]
    aliases = {}
    if prev is not None:
        in_specs += [pl.BlockSpec(memory_space=pl.ANY)] * 2
        args += list(prev)
        aliases = {n_in: 0, n_in + 1: 1}
    return pl.pallas_call(
        body, name=name, grid=(nb, d // td, nk),
        in_specs=in_specs, out_specs=[o_spec] * 2,
        out_shape=[jax.ShapeDtypeStruct(like.shape, BF16)] * 2,
        scratch_shapes=[pltpu.VMEM((td, n), F32)] * 2,
        input_output_aliases=aliases,
        compiler_params=_params(dimension_semantics=("parallel", "parallel", "arbitrary")),
    )(*args)


def _ffn_wgrad_out(name, a, dy, l, like, prev=None, tk=1024):
    nb, tok, n = a.shape
    d = dy.shape[1]
    tk = min(tk, tok)
    nk = tok // tk

    def body(*refs):
        a_ref, dy_ref = refs[:2]
        o_ref, acc = refs[len(refs) - 2:]
        k = pl.program_id(1)

        @pl.when(k == 0)
        def _():
            acc[...] = jnp.zeros(acc.shape, F32)
        acc[...] += lax.dot_general(a_ref[...], dy_ref[...], _TN, preferred_element_type=F32)

        @pl.when(k == nk - 1)
        def _():
            o_ref[...] = (0.5 * acc[...]).astype(BF16)

    in_specs = [pl.BlockSpec((None, tk, n), lambda j, k: (j, k, 0)), pl.BlockSpec((tk, d), lambda j, k: (k, 0))]
    args = [a, dy]
    aliases = {}
    if prev is not None:
        in_specs.append(pl.BlockSpec(memory_space=pl.ANY))
        args.append(prev)
        aliases = {2: 0}
    return pl.pallas_call(
        body, name=name, grid=(nb, nk),
        in_specs=in_specs, out_specs=pl.BlockSpec((None, None, n, d), lambda j, k: (j, l, 0, 0)),
        out_shape=jax.ShapeDtypeStruct(like.shape, BF16),
        scratch_shapes=[pltpu.VMEM((n, d), F32)],
        input_output_aliases=aliases,
        compiler_params=_params(dimension_semantics=("parallel", "arbitrary")),
    )(*args)


def _ffn_fwd(tag, x, g, wg, wu, wd, l):
    h = _rms_fwd(tag + "_norm", x, g)
    u, v, a = _ffn_gate_up(tag + "_gate_up", h, wg, wu, l)
    y = _ffn_down(tag + "_down", a, wd, x, l)
    return y, (x, h, u, v, a)


def _ffn_bwd(tag, dy, dyb, saved, g, wg, wu, wd, l, prev, side=None):
    x, h, u, v, a = saved
    du, dv = _ffn_d_act(tag + "_d_act", dyb, u, v, wd, l)
    d_wd = _ffn_wgrad_out(tag + "_d_wd", a, dyb, l, wd, None if prev is None else prev[2])
    d_wg, d_wu = _ffn_wgrad_in(tag + "_d_wgu", h, du, dv, l, wg, None if prev is None else prev[:2])
    dh = _ffn_d_h(tag + "_d_h", du, dv, wg, wu, l, side=side)
    dh, exchanged = dh if side is not None else (dh, None)
    dx, dg, dxb = _rms_bwd(tag + "_d_norm", dh, x, g, res=dy, also_bf16=True)
    if side is not None:
        return dx, dxb, dg, (d_wg, d_wu, d_wd), exchanged
    return dx, dxb, dg, (d_wg, d_wu, d_wd)


def _swap_halves(x):
    t, w = x.shape
    return x.reshape(t, w // HEAD_DIM, 2, HEAD_DIM // 2)[:, :, ::-1, :].reshape(t, w)


def _unheads(x):
    n, t, d = x.shape
    return x.transpose(1, 0, 2).reshape(t, n * d)


def _nat(x3):
    return x3.transpose(1, 0, 2).astype(BF16)


def _tr(x3):
    return x3.transpose(1, 2, 0).astype(BF16)


def _tr4(x3, t):
    tok, n, d = x3.shape
    return x3.reshape(tok // t, t, n, d).transpose(2, 0, 3, 1).astype(BF16)


def _from_t(x_t):
    n, d, tok = x_t.shape
    return x_t.transpose(2, 0, 1).reshape(tok, n * d)


def _from_t4(x4):
    n, nk, d, t = x4.shape
    return x4.transpose(1, 3, 0, 2).reshape(nk * t, n, d)


def _col(row):
    return row.transpose(0, 2, 1)


def _mixer_fwd(tag, x, w, cos, sin):
    tok = x.shape[0]
    h2 = _rms_fwd(tag + "_norm", x, w["mix_norm"])
    p = _mm1(tag + "_in", h2, w["w_in"], tn=640)
    c_q, c_kv = p[:, :512], p[:, 512:768]
    q_s, k_s, v_s = p[:, 768:1280], p[:, 1280:1408], p[:, 1408:1536]
    q_f, k_f, v_f = p[:, 1536:2048], p[:, 2048:2560], p[:, 2560:3072]
    k_rope, f_logit = p[:, 3072:3136], p[:, 3136:3144]

    qn = _rms_fwd(tag + "_q_norm", c_q, w["mla_q_norm"])
    qm = _mm1(tag + "_q_b", qn, w["mla_w_q_b"])
    kvn = _rms_fwd(tag + "_kv_norm", c_kv, w["mla_kv_norm"])
    kvm = _mm1(tag + "_kv_b", kvn, w["mla_w_kv_b"])

    rin = jnp.concatenate([qm[:, 1024:], q_s, k_s, k_rope, jnp.zeros((tok, 64), F32)], axis=1)
    rout = _rope(tag + "_rope", rin, _swap_halves(rin), cos, sin)
    q_pe, q_sr, k_sr, k_pe = rout[:, :512], rout[:, 512:1024], rout[:, 1024:1152], rout[:, 1152:1216]

    t = min(ATT_T, tok)
    q_m = jnp.concatenate([qm[:, :1024].reshape(tok, HEADS, MLA_NOPE), q_pe.reshape(tok, HEADS, MLA_ROPE)], axis=-1)
    k_m = jnp.concatenate([kvm[:, :1024].reshape(tok, HEADS, MLA_NOPE),
                           jnp.broadcast_to(k_pe[:, None, :], (tok, HEADS, MLA_ROPE))], axis=-1)
    v_m = kvm[:, 1024:].reshape(tok, HEADS, MLA_V)
    mla = dict(q=_nat(q_m), q_t=_tr(q_m), k=_nat(k_m), k_t4=_tr4(k_m, t), v_t4=_tr4(v_m, t))
    mla["o_t"], mla["lse"] = _attn_fwd(tag + "_mla_fwd", mla["q_t"], mla["k"], mla["v_t4"],
                                       (MLA_NOPE + MLA_ROPE) ** -0.5, t=t)

    q_s3, k_s3 = q_sr.reshape(tok, HEADS, HEAD_DIM), k_sr.reshape(tok, SWA_KV_HEADS, HEAD_DIM)
    v_s3 = v_s.reshape(tok, SWA_KV_HEADS, HEAD_DIM)
    swa = dict(q=_nat(q_s3), q_t=_tr(q_s3), k=_nat(k_s3), k_t4=_tr4(k_s3, t), v_t4=_tr4(v_s3, t))
    swa["o_t"], swa["lse"] = _attn_fwd(tag + "_swa_fwd", swa["q_t"], swa["k"], swa["v_t4"], HEAD_DIM ** -0.5,
                                       group=HEADS // SWA_KV_HEADS, sink=w["swa_sinks"], window=True, t=t)

    zt = f_logit.T
    c = _gate_fwd(zt, w["fox_forget_bias"].reshape(HEADS, 1))
    q_f3, k_f3, v_f3 = (a.reshape(tok, HEADS, HEAD_DIM) for a in (q_f, k_f, v_f))
    fox = dict(q=_nat(q_f3), q_t=_tr(q_f3), k=_nat(k_f3), k_t4=_tr4(k_f3, t), v_t4=_tr4(v_f3, t),
               c_col=c[:, :, None], c_row4=c.reshape(HEADS, tok // t, 1, t))
    fox["o_t"], fox["lse"] = _attn_fwd(tag + "_fox_fwd", fox["q_t"], fox["k"], fox["v_t4"], HEAD_DIM ** -0.5,
                                       cq_row=c[:, None, :], ck_col=fox["c_col"], t=t)

    mixed = jnp.concatenate([_from_t(mla["o_t"]), _from_t(swa["o_t"]), _from_t(fox["o_t"])], axis=1)
    y = _mm1(tag + "_out", mixed, w["w_out"], add=x)
    saved = dict(x=x, h2=h2, c_q=c_q, c_kv=c_kv, qn=qn, kvn=kvn, zt=zt, mixed=mixed, mla=mla, swa=swa, fox=fox)
    return y, saved


def _mixer_bwd(tag, dy, dyb, s, w, cos, sin, side=None):
    tok = dy.shape[0]
    g = {}
    dmixed = _mm1(tag + "_d_mixed", dyb, w["w_out"], out_dtype=BF16, tb=True)
    g["w_out"] = _mm1(tag + "_d_wout", s["mixed"], dyb, ta=True)
    t = min(ATT_T, tok)

    def attn_bwd(name, a, d_out, scale, **kw):
        do3 = d_out.reshape(tok, HEADS, -1)
        do_t = _tr(do3)
        dl = _col(_delta_t(name + "_delta", do_t, a["o_t"]))
        return dl, _attn_bwd(name + "_bwd", a["q"], a["q_t"], a["k"], a["k_t4"], a["v_t4"], _nat(do3), do_t,
                             _col(a["lse"]), dl, scale, t=t, **kw)

    _, mla_res = attn_bwd(tag + "_mla", s["mla"], dmixed[:, :1024], (MLA_NOPE + MLA_ROPE) ** -0.5, side=side)
    (dq_m, dk_m4, dv_m4), exchanged = mla_res if side is not None else (mla_res, None)
    dl, (dq_sh, dk_s4, dv_s4) = attn_bwd(tag + "_swa", s["swa"], dmixed[:, 1024:1536], HEAD_DIM ** -0.5,
                                         group=HEADS // SWA_KV_HEADS, window=True)
    g["swa_sinks"] = _sink_grad(tag + "_d_sink", w["swa_sinks"].reshape(HEADS, 1, 1), _col(s["swa"]["lse"]),
                                dl).reshape(HEADS)
    _, (dq_fh, dk_f4, dv_f4, d_cols, d_rows) = attn_bwd(tag + "_fox", s["fox"], dmixed[:, 1536:], HEAD_DIM ** -0.5,
                                                        cq_col=s["fox"]["c_col"], ck_row4=s["fox"]["c_row4"])
    dzt, dbias = _gate_bwd(d_rows[:, :, 0], d_cols.reshape(HEADS, tok), s["zt"],
                           w["fox_forget_bias"].reshape(HEADS, 1))
    g["fox_forget_bias"] = dbias.reshape(HEADS)

    grp = HEADS // SWA_KV_HEADS
    dq_mt = dq_m.transpose(1, 0, 2)
    dk_mt = _from_t4(dk_m4)
    d_qpe = dq_mt[:, :, MLA_NOPE:].reshape(tok, HEADS * MLA_ROPE)
    d_kpe_heads = dk_mt[:, :, MLA_NOPE:].reshape(tok, HEADS * MLA_ROPE)
    d_qs = _unheads(dq_sh)
    d_ks_heads = _from_t4(dk_s4).reshape(tok, SWA_KV_HEADS, grp, HEAD_DIM).transpose(0, 2, 1, 3).reshape(tok, grp * 128)
    d_vs_heads = _from_t4(dv_s4).reshape(tok, SWA_KV_HEADS, grp, HEAD_DIM).transpose(0, 2, 1, 3).reshape(tok, grp * 128)

    def fold(d_kpe_h, d_ks_h, d_vs_h):
        kpe = d_kpe_h[:, 0:64]
        for i in range(1, HEADS):
            kpe = kpe + d_kpe_h[:, 64 * i:64 * (i + 1)]
        ks, vs = d_ks_h[:, 0:128], d_vs_h[:, 0:128]
        for i in range(1, grp):
            ks = ks + d_ks_h[:, 128 * i:128 * (i + 1)]
            vs = vs + d_vs_h[:, 128 * i:128 * (i + 1)]
        return [jnp.concatenate([kpe, jnp.zeros_like(kpe)], axis=1), ks, vs]
    d_kpe2, d_ksr, d_vs = _ew(tag + "_fold_heads", fold,
                              [(d_kpe_heads, "row"), (d_ks_heads, "row"), (d_vs_heads, "row")],
                              [((tok, 128), F32, "row"), ((tok, 128), F32, "row"), ((tok, 128), F32, "row")])

    rin = jnp.concatenate([d_qpe, d_qs, d_ksr, d_kpe2], axis=1)
    rout = _rope(tag + "_d_rope", rin, _swap_halves(rin), cos, sin, sign=-1.0)
    d_qpe_pre, d_qs_pre, d_ks_pre, d_krope = rout[:, :512], rout[:, 512:1024], rout[:, 1024:1152], rout[:, 1152:1216]

    d_qm = jnp.concatenate([dq_mt[:, :, :MLA_NOPE].reshape(tok, HEADS * MLA_NOPE), d_qpe_pre], axis=1)
    d_kvm = jnp.concatenate([dk_mt[:, :, :MLA_NOPE].reshape(tok, HEADS * MLA_NOPE),
                             _from_t4(dv_m4).reshape(tok, HEADS * MLA_V)], axis=1)
    g["mla_w_q_b"] = _mm1(tag + "_d_wqb", s["qn"], d_qm, ta=True)
    d_qn = _mm1(tag + "_d_qn", d_qm, w["mla_w_q_b"], tb=True)
    d_cq, g["mla_q_norm"] = _rms_bwd(tag + "_d_q_norm", d_qn, s["c_q"], w["mla_q_norm"])
    g["mla_w_kv_b"] = _mm1(tag + "_d_wkvb", s["kvn"], d_kvm, ta=True)
    d_kvn = _mm1(tag + "_d_kvn", d_kvm, w["mla_w_kv_b"], tb=True)
    d_ckv, g["mla_kv_norm"] = _rms_bwd(tag + "_d_kv_norm", d_kvn, s["c_kv"], w["mla_kv_norm"])

    dp = jnp.concatenate([d_cq, d_ckv, d_qs_pre, d_ks_pre, d_vs, _unheads(dq_fh),
                          _from_t4(dk_f4).reshape(tok, HEADS * HEAD_DIM), _from_t4(dv_f4).reshape(tok, HEADS * HEAD_DIM),
                          d_krope, dzt.T, jnp.zeros((tok, IN_COLS_PAD - IN_COLS), F32)], axis=1).astype(BF16)
    g["w_in"] = _mm1(tag + "_d_win", s["h2"], dp, ta=True, tn=640)
    dh2 = _mm1(tag + "_d_h2", dp, w["w_in"], tb=True, tk=640)
    dx, g["mix_norm"], dxb = _rms_bwd(tag + "_d_norm", dh2, s["x"], w["mix_norm"], res=dy, also_bf16=True)
    return dx, dxb, g, exchanged


def _loss_head(x, g, target):
    d = x.shape[1]

    def fn(x, g, tgt):
        r = lax.rsqrt(jnp.mean(x * x, axis=-1, keepdims=True) + RMS_EPS)
        xh = x * r
        err = xh * g - tgt
        loss = 0.5 * jnp.sum(jnp.sum(err * err, axis=-1, keepdims=True), axis=0, keepdims=True) / d
        dy = err / d
        dxh = dy * g
        dx = r * (dxh - xh * jnp.mean(dxh * xh, axis=-1, keepdims=True))
        return [dx, jnp.sum(dy * xh, axis=0, keepdims=True), loss, dx]
    return _ew("loss_head", fn, [(x, "row"), (g, "full"), (target, "row")],
               [(x.shape, F32, "row"), (g.shape, F32, "acc"), ((1, 1), F32, "acc"), (x.shape, BF16, "row")])


def _adam_update(w, g, m, v):
    m = ADAM_B1 * m + (1.0 - ADAM_B1) * g
    v = ADAM_B2 * v + (1.0 - ADAM_B2) * (g * g)
    m_hat = m / (1.0 - ADAM_B1 ** ADAM_STEP)
    v_hat = v / (1.0 - ADAM_B2 ** ADAM_STEP)
    return [-ADAM_LR * (m_hat / (jnp.sqrt(v_hat) + ADAM_EPS) + ADAM_WD * w), m, v]


def _adamw(name, w, g, m, v):
    return _ew(name, _adam_update, [(w, "row"), (g, "row"), (m, "row"), (v, "row")], [(w.shape, F32, "row")] * 3)


def _adamw_slots(name, r, w, m, v):
    def fn(r, w, m, v):
        g = r[0].astype(F32)
        for i in range(1, r.shape[0]):
            g = g + r[i].astype(F32)
        return [g] + _adam_update(w, g, m, v)
    return _ew(name, fn, [(r, "row"), (w, "row"), (m, "row"), (v, "row")], [(w.shape, F32, "row")] * 4)


def _pair_sum(name, g, layer, from_sibling, core):
    _, _, rows, cols = g.shape
    tm = _rows(rows, 512)

    def body(c_ref, g_ref, s_ref, o_ref):
        o_ref[...] = (g_ref[...].astype(F32) + s_ref[...].astype(F32)).astype(BF16)

    return pl.pallas_call(
        body, name=name,
        grid_spec=pltpu.PrefetchScalarGridSpec(
            num_scalar_prefetch=1, grid=(N_CHIP, rows // tm),
            in_specs=[pl.BlockSpec((None, None, tm, cols), lambda ch, i, c: (2 * ch + c[0], layer, i, 0)),
                      pl.BlockSpec((None, tm, cols), lambda ch, i, c: (ch, i, 0))],
            out_specs=pl.BlockSpec((None, tm, cols), lambda ch, i, c: (ch, i, 0))),
        out_shape=jax.ShapeDtypeStruct((N_CHIP, rows, cols), BF16),
        compiler_params=_params(dimension_semantics=("parallel", "parallel")),
    )(core, g, from_sibling)


def _chip_sums(blocks, layer, tag):
    names = list(blocks)
    core = lax.axis_index("c").astype(jnp.int32).reshape(1)
    from_sibling = _pair_exchange([blocks[k] for k in names], tag)
    return [_pair_sum("pair_sum_%s_%s" % (tag, k), blocks[k][0], blocks[k][1], s, core)
            for k, s in zip(names, from_sibling)]


def _coords(dev):
    return (dev // 4, (dev // 2) % 2, dev % 2)


def _all_gather(shards):
    n = len(shards)

    def body(*refs):
        x_refs, out_refs = refs[:n], refs[n:2 * n]
        send_sems, recv_sems, local_sems = refs[2 * n:]
        x, y, c = lax.axis_index("x"), lax.axis_index("y"), lax.axis_index("c")
        me, sibling = (x, y, c), (x, y, 1 - c)
        chips = [(1 - x, y), (x, 1 - y), (1 - x, 1 - y)]

        def slot(a, px, py, pc):
            return out_refs[a].at[4 * px + 2 * py + pc]

        def copy(a, k, block, to, src=None):
            return pltpu.make_async_remote_copy(
                src_ref=slot(a, *block) if src is None else src, dst_ref=slot(a, *block),
                send_sem=send_sems.at[a, k], recv_sem=recv_sems.at[a, k],
                device_id=to, device_id_type=pl.DeviceIdType.MESH)

        mine = [pltpu.make_async_copy(x_refs[a], slot(a, *me), local_sems.at[a]) for a in range(n)]
        first, passed = [], []
        for a in range(n):
            mine[a].start()
            first.append(copy(a, 0, me, sibling, src=x_refs[a]))
            first += [copy(a, 1 + j, me, (*chip, c), src=x_refs[a]) for j, chip in enumerate(chips)]
        for cp in first:
            cp.start()
        for a in range(n):
            for j, chip in enumerate(chips):
                copy(a, 1 + j, (*chip, c), me).wait_recv()
                passed.append(copy(a, 4 + j, (*chip, c), sibling))
                passed[-1].start()
        for a in range(n):
            copy(a, 0, sibling, me).wait_recv()
            for j, chip in enumerate(chips):
                copy(a, 4 + j, (*chip, 1 - c), me).wait_recv()
        for cp in first + passed:
            cp.wait_send()
        for a in range(n):
            mine[a].wait()

    return pl.pallas_call(
        body, name="all_gather_weights",
        out_shape=[jax.ShapeDtypeStruct((N_DEV,) + s.shape, s.dtype) for s in shards],
        in_specs=[pl.BlockSpec(memory_space=pl.ANY)] * n,
        out_specs=[pl.BlockSpec(memory_space=pl.ANY)] * n,
        scratch_shapes=[pltpu.SemaphoreType.DMA((n, 7)), pltpu.SemaphoreType.DMA((n, 7)),
                        pltpu.SemaphoreType.DMA((n,))],
    )(*shards)


def _pair_exchange(blocks, tag):
    n = len(blocks)
    layers = [li for _, li in blocks]
    arrays = [b for b, _ in blocks]

    def body(*refs):
        g_refs, out_refs = refs[:n], refs[n:2 * n]
        send_sems, recv_sems = refs[2 * n:]
        x, y, c = lax.axis_index("x"), lax.axis_index("y"), lax.axis_index("c")
        copies = [pltpu.make_async_remote_copy(
            src_ref=g_refs[a].at[2 * chip + (1 - c), layers[a]], dst_ref=out_refs[a].at[chip],
            send_sem=send_sems.at[a, chip], recv_sem=recv_sems.at[a, chip],
            device_id=(x, y, 1 - c), device_id_type=pl.DeviceIdType.MESH)
            for a in range(n) for chip in range(N_CHIP)]
        for cp in copies:
            cp.start()
        for cp in copies:
            cp.wait()

    return pl.pallas_call(
        body, name="pair_exchange_grads_" + tag,
        out_shape=[jax.ShapeDtypeStruct((N_CHIP,) + b.shape[2:], b.dtype) for b in arrays],
        in_specs=[pl.BlockSpec(memory_space=pl.ANY)] * n,
        out_specs=[pl.BlockSpec(memory_space=pl.ANY)] * n,
        scratch_shapes=[pltpu.SemaphoreType.DMA((n, N_CHIP)), pltpu.SemaphoreType.DMA((n, N_CHIP))],
    )(*arrays)


class _ChipExchange:
    def __init__(self, sums, layer, prev=None):
        self.sums, self.layer, self.prev = list(sums), layer, None if prev is None else list(prev)
        self.n = len(self.sums)

    def specs(self):
        n = self.n
        args = self.sums + (self.prev or [])
        any_spec = pl.BlockSpec(memory_space=pl.ANY)
        out_shape = [jax.ShapeDtypeStruct((N_CHIP, DEPTH) + s.shape[1:], s.dtype) for s in self.sums]
        scratch = [pltpu.SemaphoreType.DMA((n, N_CHIP)), pltpu.SemaphoreType.DMA((n, N_CHIP)),
                   pltpu.SemaphoreType.DMA((n,))]
        aliases = {n + a: a for a in range(n)} if self.prev else {}
        return args, [any_spec] * len(args), [any_spec] * n, out_shape, scratch, aliases

    def ops(self, in_refs, out_refs, sem_refs):
        n, layer = self.n, self.layer
        g_refs = in_refs[:n]
        send_sems, recv_sems, local_sems = sem_refs
        x, y, c = lax.axis_index("x"), lax.axis_index("y"), lax.axis_index("c")
        my_chip = 2 * x + y

        def copy(a, chip):
            return pltpu.make_async_remote_copy(
                src_ref=g_refs[a].at[chip], dst_ref=out_refs[a].at[my_chip, layer],
                send_sem=send_sems.at[a, chip], recv_sem=recv_sems.at[a, my_chip],
                device_id=(chip // 2, chip % 2, c), device_id_type=pl.DeviceIdType.MESH)

        def arrival(a, chip):
            return pltpu.make_async_remote_copy(
                src_ref=g_refs[a].at[chip], dst_ref=out_refs[a].at[chip, layer],
                send_sem=send_sems.at[a, chip], recv_sem=recv_sems.at[a, chip],
                device_id=(chip // 2, chip % 2, c), device_id_type=pl.DeviceIdType.MESH)

        def mine(a):
            return pltpu.make_async_copy(g_refs[a].at[my_chip], out_refs[a].at[my_chip, layer], local_sems.at[a])

        def others(step):
            for chip in range(N_CHIP):
                @pl.when(chip != my_chip)
                def _(chip=chip):
                    for a in range(n):
                        step(a, chip)

        def start():
            for a in range(n):
                mine(a).start()
            others(lambda a, chip: copy(a, chip).start())

        def finish():
            others(lambda a, chip: arrival(a, chip).wait_recv())
            others(lambda a, chip: copy(a, chip).wait_send())
            for a in range(n):
                mine(a).wait()

        return start, finish

    def call(self):
        args, in_specs, out_specs, out_shape, scratch, aliases = self.specs()
        n_in = len(args)

        def body(*refs):
            start, finish = self.ops(refs[:n_in], refs[n_in:n_in + self.n], refs[n_in + self.n:])
            start()
            finish()

        return pl.pallas_call(
            body, name="chip_exchange_grads_l%d" % self.layer, out_shape=out_shape, in_specs=in_specs,
            out_specs=out_specs, scratch_shapes=scratch, input_output_aliases=aliases,
        )(*args)


def _all_reduce_small(buf):
    def body(x_ref, out_ref, slots, send_sems, recv_sems):
        me = 4 * lax.axis_index("x") + 2 * lax.axis_index("y") + lax.axis_index("c")

        def copy(peer):
            return pltpu.make_async_remote_copy(
                src_ref=x_ref, dst_ref=slots.at[me],
                send_sem=send_sems.at[peer], recv_sem=recv_sems.at[me],
                device_id=_coords(peer), device_id_type=pl.DeviceIdType.MESH)

        def arrival(peer):
            return pltpu.make_async_remote_copy(
                src_ref=x_ref, dst_ref=slots.at[peer],
                send_sem=send_sems.at[peer], recv_sem=recv_sems.at[peer],
                device_id=_coords(peer), device_id_type=pl.DeviceIdType.MESH)

        slots[pl.ds(me, 1)] = x_ref[...][None]
        for peer in range(N_DEV):
            @pl.when(peer != me)
            def _(peer=peer):
                copy(peer).start()
        for peer in range(N_DEV):
            @pl.when(peer != me)
            def _(peer=peer):
                arrival(peer).wait_recv()
        for peer in range(N_DEV):
            @pl.when(peer != me)
            def _(peer=peer):
                copy(peer).wait_send()
        acc = slots[0]
        for peer in range(1, N_DEV):
            acc = acc + slots[peer]
        out_ref[...] = acc

    return pl.pallas_call(
        body, name="all_reduce_small",
        out_shape=jax.ShapeDtypeStruct(buf.shape, F32),
        in_specs=[pl.BlockSpec(memory_space=pltpu.VMEM)],
        out_specs=pl.BlockSpec(memory_space=pltpu.VMEM),
        scratch_shapes=[pltpu.VMEM((N_DEV,) + buf.shape, F32), pltpu.SemaphoreType.DMA((N_DEV,)),
                        pltpu.SemaphoreType.DMA((N_DEV,))],
        compiler_params=_params(),
    )(buf)


def _in_to_kernel(w):
    pad = jnp.zeros(w.shape[:-1] + (IN_COLS_PAD - IN_COLS,), w.dtype)
    return jnp.concatenate([w[..., :768], w[..., 832:3136], w[..., 768:832], w[..., 3136:], pad], axis=-1)


def _in_from_kernel(w):
    return jnp.concatenate([w[..., :768], w[..., 3072:3136], w[..., 768:3072], w[..., 3136:3144]], axis=-1)


def _split_to_kernel(w, a, b):
    r = w.shape[0]
    w3 = w.reshape(r, HEADS, a + b)
    return jnp.concatenate([w3[:, :, :a].reshape(r, HEADS * a), w3[:, :, a:].reshape(r, HEADS * b)], axis=1)


def _split_from_kernel(w, a, b):
    r = w.shape[0]
    return jnp.concatenate([w[:, :HEADS * a].reshape(r, HEADS, a), w[:, HEADS * a:].reshape(r, HEADS, b)],
                           axis=-1).reshape(r, HEADS * (a + b))


MIXER_BIG = ("w_in", "mla_w_q_b", "mla_w_kv_b", "w_out")


def _col_blocks_to_full(blk):
    nb, k, n = blk.shape
    return blk.transpose(1, 0, 2).reshape(k, nb * n)


def _full_to_col_blocks(g):
    k, c = g.shape
    return g.reshape(k, N_DEV, c // N_DEV).transpose(1, 0, 2).astype(BF16)


def _local_step(x, positions, target, gathered, small, exchange=False):
    inv_freq = ROPE_THETA ** (-jnp.arange(0, HEAD_DIM, 2, dtype=F32) / HEAD_DIM)
    cos64, sin64 = _rope_tables(positions, jnp.concatenate([-inv_freq, inv_freq])[None, :])
    cos = jnp.tile(cos64, (1, 20))
    sin = jnp.tile(sin64, (1, 20))

    layers = []
    for l in range(DEPTH):
        w = {}
        w["w_in"] = _in_to_kernel(_col_blocks_to_full(gathered["w_in"][:, l]))
        w["mla_w_q_b"] = _split_to_kernel(_col_blocks_to_full(gathered["mla_w_q_b"][:, l]), MLA_NOPE, MLA_ROPE)
        w["mla_w_kv_b"] = _split_to_kernel(_col_blocks_to_full(gathered["mla_w_kv_b"][:, l]), MLA_NOPE, MLA_V)
        blk = gathered["w_out"][:, l]
        w["w_out"] = blk.reshape(blk.shape[0] * blk.shape[1], blk.shape[2])
        for k in SMALL:
            if k != "final_norm":
                w[k] = small[k][l][None, :] if "norm" in k else small[k][l]
        layers.append(w)

    def ffn_w(f):
        return gathered[f + "_w_gate"], gathered[f + "_w_up"], gathered[f + "_w_down"]

    saved = []
    for l, w in enumerate(layers):
        t = "l%d" % l
        x, s1 = _ffn_fwd(t + "_ffn1", x, w["ffn1_norm"], *ffn_w("ffn1"), l)
        x, s2 = _mixer_fwd(t + "_mix", x, w, cos, sin)
        x, s3 = _ffn_fwd(t + "_ffn2", x, w["ffn2_norm"], *ffn_w("ffn2"), l)
        saved.append((s1, s2, s3))

    dx, d_final, loss, dxb = _loss_head(x, small["final_norm"][None, :], target)

    mixer_g = [None] * DEPTH
    small_g = [None] * DEPTH
    ffn_g = {"ffn1": None, "ffn2": None}
    received = {}
    pending = None

    def exchange_of(blocks, l, tag):
        names = list(blocks)
        prev = [received[k] for k in names] if all(k in received for k in names) else None
        return names, _ChipExchange(_chip_sums(blocks, l, tag), l, prev)

    def took(job, results):
        received.update(zip(job[0], results))

    def ffn_blocks(f, l):
        return {f + name: (buf, l) for name, buf in zip(("_w_gate", "_w_up", "_w_down"), ffn_g[f])}

    for l in reversed(range(DEPTH)):
        w, (s1, s2, s3) = layers[l], saved[l]
        t = "l%d" % l
        sg = {}
        dx, dxb, sg["ffn2_norm"], ffn_g["ffn2"] = _ffn_bwd(t + "_ffn2", dx, dxb, s3, w["ffn2_norm"], *ffn_w("ffn2"), l,
                                                           ffn_g["ffn2"])
        dx, dxb, g, exchanged = _mixer_bwd(t + "_mix", dx, dxb, s2, w, cos, sin,
                                           side=None if pending is None else pending[1])
        if pending is not None:
            took(pending, exchanged)
            pending = None
        mixer_g[l] = {
            "w_in": _full_to_col_blocks(_in_from_kernel(g["w_in"])),
            "mla_w_q_b": _full_to_col_blocks(_split_from_kernel(g["mla_w_q_b"], MLA_NOPE, MLA_ROPE)),
            "mla_w_kv_b": _full_to_col_blocks(_split_from_kernel(g["mla_w_kv_b"], MLA_NOPE, MLA_V)),
            "w_out": g["w_out"].astype(BF16).reshape(N_DEV, -1, g["w_out"].shape[1]),
        }
        job = None
        if exchange:
            blocks = {k: (mixer_g[l][k][:, None], 0) for k in MIXER_BIG}
            blocks.update(ffn_blocks("ffn2", l))
            job = exchange_of(blocks, l, t + "a")
        res = _ffn_bwd(t + "_ffn1", dx, dxb, s1, w["ffn1_norm"], *ffn_w("ffn1"), l, ffn_g["ffn1"],
                       side=None if job is None else job[1])
        dx, dxb, sg["ffn1_norm"], ffn_g["ffn1"] = res[:4]
        if job is not None:
            took(job, res[4])
            pending = exchange_of(ffn_blocks("ffn1", l), l, t + "b")
        for k in SMALL:
            if k in g:
                sg[k] = g[k]
        small_g[l] = {k: v.reshape(-1) for k, v in sg.items()}

    sm = {k: jnp.stack([small_g[l][k] for l in range(DEPTH)]) for k in small_g[0]}
    sm["final_norm"] = d_final.reshape(-1)
    if exchange:
        took(pending, pending[1].call())
        return loss, dx, received, sm
    big = {k: jnp.stack([mixer_g[l][k] for l in range(DEPTH)], axis=1) for k in MIXER_BIG}
    for f in ("ffn1", "ffn2"):
        big[f + "_w_gate"], big[f + "_w_up"], big[f + "_w_down"] = ffn_g[f]
    return loss, dx, big, sm


def _flat_rows(a):
    return a.reshape(-1, LANES)


def kernel(x, positions, ffn1_norm, ffn1_w_gate, ffn1_w_up, ffn1_w_down, mix_norm, w_in, mla_q_norm, mla_w_q_b, mla_kv_norm, mla_w_kv_b, swa_sinks, fox_forget_bias, w_out, ffn2_norm, ffn2_w_gate, ffn2_w_up, ffn2_w_down, final_norm, loss_target, m_ffn1_norm, m_ffn1_w_gate, m_ffn1_w_up, m_ffn1_w_down, m_mix_norm, m_w_in, m_mla_q_norm, m_mla_w_q_b, m_mla_kv_norm, m_mla_w_kv_b, m_swa_sinks, m_fox_forget_bias, m_w_out, m_ffn2_norm, m_ffn2_w_gate, m_ffn2_w_up, m_ffn2_w_down, m_final_norm, v_ffn1_norm, v_ffn1_w_gate, v_ffn1_w_up, v_ffn1_w_down, v_mix_norm, v_w_in, v_mla_q_norm, v_mla_w_q_b, v_mla_kv_norm, v_mla_w_kv_b, v_swa_sinks, v_fox_forget_bias, v_w_out, v_ffn2_norm, v_ffn2_w_gate, v_ffn2_w_up, v_ffn2_w_down, v_final_norm):
    given = dict(locals())
    weights = {k: given[k] for k in WEIGHTS}
    mom_m = {k: given["m_" + k] for k in WEIGHTS}
    mom_v = {k: given["v_" + k] for k in WEIGHTS}

    gathered = dict(zip(BIG, _all_gather([weights[k].astype(BF16) for k in BIG])))

    small = {k: weights[k] for k in SMALL}
    loss, grad_x, received, small_g = _local_step(x[0], positions[0][:, None], loss_target[0], gathered, small,
                                                  exchange=True)

    small_sizes = [weights[k].size for k in SMALL]
    sbuf = jnp.concatenate([small_g[k].reshape(-1) for k in SMALL] + [loss.reshape(-1)])
    pad = (-sbuf.size) % (8 * LANES)
    sbuf = jnp.concatenate([sbuf, jnp.zeros((pad,), F32)])
    stot = _all_reduce_small(_flat_rows(sbuf)).reshape(-1)

    grad_w, delta, new_m, new_v = {}, {}, {}, {}
    for k in BIG:
        shape = weights[k].shape
        two_d = (-1, shape[-1])
        g, d, m, v = _adamw_slots("adamw_" + k, received[k].reshape(N_CHIP, -1, shape[-1]), weights[k].reshape(two_d),
                                  mom_m[k].reshape(two_d), mom_v[k].reshape(two_d))
        grad_w[k], delta[k], new_m[k], new_v[k] = (t.reshape(shape) for t in (g, d, m, v))
    off = 0
    for k, n in zip(SMALL, small_sizes):
        shape = weights[k].shape
        two_d = (-1, shape[-1]) if len(shape) > 1 else (1, -1)
        grad_w[k] = stot[off:off + n].reshape(shape)
        off += n
        d, m, v = _adamw("adamw_" + k, weights[k].reshape(two_d), grad_w[k].reshape(two_d),
                         mom_m[k].reshape(two_d), mom_v[k].reshape(two_d))
        delta[k], new_m[k], new_v[k] = d.reshape(shape), m.reshape(shape), v.reshape(shape)
    loss_total = stot[off]

    return (loss_total, grad_x[None], *[grad_w[k] for k in WEIGHTS], *[delta[k] for k in WEIGHTS],
            *[new_m[k] for k in WEIGHTS], *[new_v[k] for k in WEIGHTS])
```

```python
import jax
import jax.numpy as jnp
from jax import lax
from jax.experimental import pallas as pl
from jax.experimental.pallas import tpu as pltpu

F32 = jnp.float32
BF16 = jnp.bfloat16

N_DEV = 8
N_CHIP = 4
DEPTH = 2
RMS_EPS = 1e-6
ROPE_THETA = 10000.0
HEADS = 8
MLA_Q_LORA = 512
MLA_KV_LORA = 256
MLA_NOPE = 128
MLA_ROPE = 64
MLA_V = 128
SWA_KV_HEADS = 2
HEAD_DIM = 64
WINDOW = 128
IN_COLS = 3144
IN_COLS_PAD = 3200

ADAM_LR = 0.001
ADAM_B1 = 0.9
ADAM_B2 = 0.999
ADAM_EPS = 1e-08
ADAM_WD = 0.01
ADAM_STEP = 10

LANES = 128
NEG = -1e30
LOG2E = 1.4426950408889634
LN2 = 0.6931471805599453
VMEM_LIMIT_BYTES = 48 * 1024 * 1024

EW_ROWS = 256
MM_TM = 1024
MM_TN = 1024
MM_TK = 1024
ATT_T = 512

BIG = ("ffn1_w_gate", "ffn1_w_up", "ffn1_w_down", "w_in", "mla_w_q_b", "mla_w_kv_b", "w_out",
       "ffn2_w_gate", "ffn2_w_up", "ffn2_w_down")
SMALL = ("ffn1_norm", "mix_norm", "mla_q_norm", "mla_kv_norm", "swa_sinks", "fox_forget_bias", "ffn2_norm",
         "final_norm")
WEIGHTS = ("ffn1_norm", "ffn1_w_gate", "ffn1_w_up", "ffn1_w_down", "mix_norm", "w_in", "mla_q_norm", "mla_w_q_b",
           "mla_kv_norm", "mla_w_kv_b", "swa_sinks", "fox_forget_bias", "w_out", "ffn2_norm", "ffn2_w_gate",
           "ffn2_w_up", "ffn2_w_down", "final_norm")


def _params(**kw):
    return pltpu.CompilerParams(vmem_limit_bytes=VMEM_LIMIT_BYTES, **kw)


def _tile(n, want):
    if n <= want:
        return n
    t = (want // LANES) * LANES
    while n % t:
        t -= LANES
    return t


def _rows(n, want):
    if n <= want:
        return n
    t = (want // 8) * 8
    while n % t:
        t -= 8
    return t


def _ew(name, fn, ins, outs, tm=EW_ROWS):
    tok = None
    for a, kind in ins:
        if kind == "row":
            tok = a.shape[-2]
    tm = _rows(tok, tm)
    steps = tok // tm

    def spec(shape, kind):
        if kind == "row":
            ax = len(shape) - 2
            blk = tuple(tm if d == ax else s for d, s in enumerate(shape))
            return pl.BlockSpec(blk, lambda i, ax=ax, n=len(shape): tuple(i if d == ax else 0 for d in range(n)))
        return pl.BlockSpec(tuple(shape), lambda i, n=len(shape): (0,) * n)

    n_in = len(ins)
    kinds = [k for _, _, k in outs]

    def body(*refs):
        vals = fn(*[r[...] for r in refs[:n_in]])
        for r, v, kind in zip(refs[n_in:], vals, kinds):
            if kind == "row":
                r[...] = v.astype(r.dtype)
            else:
                @pl.when(pl.program_id(0) == 0)
                def _(r=r):
                    r[...] = jnp.zeros(r.shape, r.dtype)
                r[...] += v.astype(r.dtype)

    res = pl.pallas_call(
        body, name=name, grid=(steps,),
        in_specs=[spec(a.shape, k) for a, k in ins],
        out_specs=[spec(s, k) for s, _, k in outs],
        out_shape=[jax.ShapeDtypeStruct(tuple(s), d) for s, d, _ in outs],
        compiler_params=_params(dimension_semantics=("arbitrary",)),
    )(*[a for a, _ in ins])
    return res


def _rms_fwd(name, x, g):
    def fn(x, g):
        r = lax.rsqrt(jnp.mean(x * x, axis=-1, keepdims=True) + RMS_EPS)
        return [x * r * g]
    return _ew(name, fn, [(x, "row"), (g, "full")], [(x.shape, BF16, "row")])[0]


def _rms_bwd(name, dh, x, g, res=None, also_bf16=False):
    def fn(dh, x, g, *rest):
        dh = dh.astype(F32)
        r = lax.rsqrt(jnp.mean(x * x, axis=-1, keepdims=True) + RMS_EPS)
        xh = x * r
        dxh = dh * g
        dx = r * (dxh - xh * jnp.mean(dxh * xh, axis=-1, keepdims=True))
        if rest:
            dx = dx + rest[0]
        return [dx, jnp.sum(dh * xh, axis=0, keepdims=True)] + ([dx] if also_bf16 else [])
    ins = [(dh, "row"), (x, "row"), (g, "full")] + ([(res, "row")] if res is not None else [])
    outs = [(x.shape, F32, "row"), (g.shape, F32, "acc")] + ([(x.shape, BF16, "row")] if also_bf16 else [])
    return _ew(name, fn, ins, outs)


def _rope(name, x, xs, cos, sin, sign=1.0):
    def fn(x, xs, c, s):
        return [x * c + sign * (xs * s)]
    return _ew(name, fn, [(x, "row"), (xs, "row"), (cos, "row"), (sin, "row")], [(x.shape, F32, "row")])[0]


def _rope_tables(positions, inv_freq2):
    def fn(pos, f):
        ang = pos.astype(F32) * f
        return [jnp.cos(ang), jnp.sin(ang)]
    t = positions.shape[0]
    return _ew("rope_tables", fn, [(positions, "row"), (inv_freq2, "full")],
               [((t, 2 * 32), F32, "row"), ((t, 2 * 32), F32, "row")])


def _mm(name, lhs, rhs, terms, epi, out_dtypes, extras=(), ta=False, tb=False, tm=MM_TM, tn=MM_TN, tk=MM_TK):
    if ta:
        kdim, m = lhs[0].shape
    else:
        m, kdim = lhs[0].shape
    n = rhs[0].shape[0] if tb else rhs[0].shape[1]
    tm, tn, tk = _tile(m, tm), _tile(n, tn), _tile(kdim, tk)
    nk = kdim // tk
    n_acc = 1 + max(a for _, _, a in terms)
    nl, nr, ne = len(lhs), len(rhs), len(extras)
    dims = (((0 if ta else 1,), (1 if tb else 0,)), ((), ()))

    def body(*refs):
        l_refs, r_refs = refs[:nl], refs[nl:nl + nr]
        e_refs = refs[nl + nr:nl + nr + ne]
        o_refs = refs[nl + nr + ne:len(refs) - n_acc]
        accs = refs[len(refs) - n_acc:]
        k = pl.program_id(2)

        @pl.when(k == 0)
        def _():
            for acc in accs:
                acc[...] = jnp.zeros(acc.shape, F32)

        lv, rv = {}, {}
        for li, ri, ai in terms:
            if li not in lv:
                lv[li] = l_refs[li][...].astype(BF16)
            if ri not in rv:
                rv[ri] = r_refs[ri][...].astype(BF16)
            accs[ai][...] += lax.dot_general(lv[li], rv[ri], dims, preferred_element_type=F32)

        @pl.when(k == nk - 1)
        def _():
            outs = epi([acc[...] for acc in accs], [e[...] for e in e_refs])
            for o, v in zip(o_refs, outs):
                o[...] = v.astype(o.dtype)

    l_spec = pl.BlockSpec((tk, tm), lambda i, j, k: (k, i)) if ta else pl.BlockSpec((tm, tk), lambda i, j, k: (i, k))
    r_spec = pl.BlockSpec((tn, tk), lambda i, j, k: (j, k)) if tb else pl.BlockSpec((tk, tn), lambda i, j, k: (k, j))
    o_spec = pl.BlockSpec((tm, tn), lambda i, j, k: (i, j))
    return pl.pallas_call(
        body, name=name, grid=(m // tm, n // tn, nk),
        in_specs=[l_spec] * nl + [r_spec] * nr + [o_spec] * ne,
        out_specs=[o_spec] * len(out_dtypes),
        out_shape=[jax.ShapeDtypeStruct((m, n), d) for d in out_dtypes],
        scratch_shapes=[pltpu.VMEM((tm, tn), F32)] * n_acc,
        compiler_params=_params(dimension_semantics=("parallel", "parallel", "arbitrary")),
    )(*lhs, *rhs, *extras)


def _mm1(name, a, b, out_dtype=F32, scale=None, add=None, **kw):
    def epi(accs, ex):
        v = accs[0] if scale is None else accs[0] * scale
        return [v + ex[0] if ex else v]
    return _mm(name, [a], [b], [(0, 0, 0)], epi, [out_dtype], extras=[] if add is None else [add], **kw)[0]


def _sink_grad(name, sink3, lse, delta):
    def fn(sk, lse, dl):
        return [-jnp.sum(jnp.exp2(sk * LOG2E - lse) * dl, axis=1, keepdims=True)]
    return _ew(name, fn, [(sink3, "full"), (lse, "row"), (delta, "row")], [(sink3.shape, F32, "acc")], tm=512)[0]


def _resident(block, index_map):
    return pl.BlockSpec(block, index_map, pipeline_mode=pl.Buffered(1))


def _attn_fwd(name, q_t, k, v_t4, scale, group=1, cq_row=None, ck_col=None, sink=None, window=False, t=ATT_T):
    h_n, dq, tok = q_t.shape
    dv = v_t4.shape[2]
    t = min(t, tok)
    nq = tok // t
    bias = cq_row is not None
    has_sink = sink is not None

    def body(*refs):
        q_ref, k_ref, v_ref = refs[:3]
        pos = 3
        cq_ref = ck_ref = sink_ref = None
        if bias:
            cq_ref, ck_ref = refs[pos], refs[pos + 1]
            pos += 2
        if has_sink:
            sink_ref = refs[pos]
            pos += 1
        o_ref, lse_ref, m_s, l_s, acc_s = refs[pos:]
        h, qi = pl.program_id(0), pl.program_id(1)
        qs = (q_ref[...].astype(F32) * (scale * LOG2E)).astype(BF16)
        m_s[...] = jnp.full(m_s.shape, sink_ref[h] * LOG2E if has_sink else NEG, F32)
        l_s[...] = jnp.full(l_s.shape, 1.0 if has_sink else 0.0, F32)
        acc_s[...] = jnp.zeros(acc_s.shape, F32)
        c_ref = cq_ref[:, 0:1] if bias else None

        def scores(j, masked):
            rows = pl.ds(pl.multiple_of(j * t, t), t)
            s = jnp.dot(k_ref[rows, :], qs, preferred_element_type=F32)
            if bias:
                s = s - (ck_ref[rows, :] - c_ref)
            if masked:
                kpos = j * t + lax.broadcasted_iota(jnp.int32, s.shape, 0)
                qpos = qi * t + lax.broadcasted_iota(jnp.int32, s.shape, 1)
                mask = kpos <= qpos
                if window:
                    mask = mask & (kpos > qpos - WINDOW)
                s = jnp.where(mask, s, NEG)
            return s

        def step(tiles):
            ss = [scores(j, masked) for j, masked in tiles]
            m_prev = m_s[...]
            m_new = m_prev
            for s in ss:
                m_new = jnp.maximum(m_new, jnp.max(s, axis=0, keepdims=True))
            alpha = jnp.exp2(m_prev - m_new)
            l_new, acc = alpha * l_s[...], alpha * acc_s[...]
            for (j, _), s in zip(tiles, ss):
                p = jnp.exp2(s - m_new)
                l_new = l_new + jnp.sum(p, axis=0, keepdims=True)
                acc = acc + jnp.dot(v_ref[j], p.astype(BF16), preferred_element_type=F32)
            l_s[...], acc_s[...], m_s[...] = l_new, acc, m_new

        if window:
            pl.when(qi > 0)(lambda: step([(qi - 1, True), (qi, True)]))
            pl.when(qi == 0)(lambda: step([(qi, True)]))
        else:
            def below(i, carry):
                step([(2 * i, False), (2 * i + 1, False)])
                return carry
            lax.fori_loop(0, qi // 2, below, 0)
            pl.when(qi % 2 == 1)(lambda: step([(qi - 1, False), (qi, True)]))
            pl.when(qi % 2 == 0)(lambda: step([(qi, True)]))
        o_ref[...] = (acc_s[...] / l_s[...]).astype(o_ref.dtype)
        lse_ref[...] = m_s[...] + jnp.log(l_s[...]) * LOG2E

    nk = tok // t
    in_specs = [
        pl.BlockSpec((None, dq, t), lambda h, qi: (h, 0, qi)),
        _resident((None, tok, dq), lambda h, qi: (h // group, 0, 0)),
        _resident((None, nk, dv, t), lambda h, qi: (h // group, 0, 0, 0)),
    ]
    args = [q_t, k, v_t4]
    if bias:
        in_specs += [pl.BlockSpec((None, 1, t), lambda h, qi: (h, 0, qi)),
                     _resident((None, tok, 1), lambda h, qi: (h, 0, 0))]
        args += [cq_row, ck_col]
    if has_sink:
        in_specs.append(pl.BlockSpec(memory_space=pltpu.SMEM))
        args.append(sink)
    return pl.pallas_call(
        body, name=name, grid=(h_n, nq),
        in_specs=in_specs,
        out_specs=[pl.BlockSpec((None, dv, t), lambda h, qi: (h, 0, qi)),
                   pl.BlockSpec((None, 1, t), lambda h, qi: (h, 0, qi))],
        out_shape=[jax.ShapeDtypeStruct((h_n, dv, tok), BF16), jax.ShapeDtypeStruct((h_n, 1, tok), F32)],
        scratch_shapes=[pltpu.VMEM((1, t), F32), pltpu.VMEM((1, t), F32), pltpu.VMEM((dv, t), F32)],
        compiler_params=_params(dimension_semantics=("parallel", "parallel")),
    )(*args)


def _attn_bwd(name, q, q_t, k, k_t4, v_t4, do, do_t, lse, delta, scale, group=1, cq_col=None, ck_row4=None,
              window=False, t=ATT_T, side=None):
    h_n, tok, dq = q.shape
    dv = do.shape[2]
    t = min(t, tok)
    nq = tok // t
    bias = cq_col is not None

    n_in = 9 + (2 if bias else 0)
    n_out = 3 + (2 if bias else 0)
    side_args, side_in_specs, side_out_specs, side_out_shape, side_scratch, side_aliases = (
        side.specs() if side is not None else ([], [], [], [], [], {}))

    def body(*refs):
        q_ref, qt_ref, k_ref, kt_ref, vt_ref, do_ref, dot_ref, lse_ref, dl_ref = refs[:9]
        cq_ref = ck_ref = dc_ref = dr_ref = None
        if bias:
            cq_ref, ck_ref = refs[9], refs[10]
        pos = n_in + len(side_args)
        dq_ref, dk_ref, dv_ref = refs[pos:pos + 3]
        if bias:
            dc_ref, dr_ref = refs[pos + 3], refs[pos + 4]
        h, qi = pl.program_id(0), pl.program_id(1)
        if side is not None:
            side_start, side_finish = side.ops(refs[n_in:pos], refs[pos + n_out:pos + n_out + side.n],
                                               refs[pos + n_out + side.n:])
            pl.when((h == 0) & (qi == 0))(side_start)

        @pl.when(qi == 0)
        def _():
            dk_ref[...] = jnp.zeros(dk_ref.shape, F32)
            dv_ref[...] = jnp.zeros(dv_ref.shape, F32)
            if bias:
                dc_ref[...] = jnp.zeros(dc_ref.shape, F32)

        qs = (q_ref[...].astype(F32) * (scale * LOG2E)).astype(BF16)
        dq_ref[...] = jnp.zeros(dq_ref.shape, F32)
        if bias:
            dr_ref[...] = jnp.zeros(dr_ref.shape, F32)
        c_ref = cq_ref[0:1, :] if bias else None

        def step(j, masked):
            s = jnp.dot(qs, kt_ref[j], preferred_element_type=F32)
            if bias:
                s = s - (ck_ref[j] - c_ref)
            p = jnp.exp2(s - lse_ref[...])
            if masked:
                qpos = qi * t + lax.broadcasted_iota(jnp.int32, s.shape, 0)
                kpos = j * t + lax.broadcasted_iota(jnp.int32, s.shape, 1)
                mask = kpos <= qpos
                if window:
                    mask = mask & (kpos > qpos - WINDOW)
                p = jnp.where(mask, p, 0.0)
            pb = p.astype(BF16)
            dv_ref[j] += jnp.dot(dot_ref[...], pb, preferred_element_type=F32)
            dp = jnp.dot(do_ref[...], vt_ref[j], preferred_element_type=F32)
            ds = p * (dp - dl_ref[...])
            dsb = ds.astype(BF16)
            rows = pl.ds(pl.multiple_of(j * t, t), t)
            dq_ref[...] += jnp.dot(dsb, k_ref[rows, :], preferred_element_type=F32)
            dk_ref[j] += scale * jnp.dot(qt_ref[...], dsb, preferred_element_type=F32)
            if bias:
                dc_ref[j] += jnp.sum(ds, axis=0, keepdims=True)
                dr_ref[...] += jnp.sum(ds, axis=1, keepdims=True)

        if window:
            pl.when(qi > 0)(lambda: step(qi - 1, True))
        else:
            def below(j, carry):
                step(j, False)
                return carry
            lax.fori_loop(0, qi, below, 0)
        step(qi, True)
        dq_ref[...] = dq_ref[...] * scale
        if side is not None:
            pl.when((h == h_n - 1) & (qi == nq - 1))(side_finish)

    nk = nq

    def q_tile(shape_tail):
        return pl.BlockSpec((None, t) + shape_tail, lambda h, qi: (h, qi, 0))

    in_specs = [
        q_tile((dq,)),
        pl.BlockSpec((None, dq, t), lambda h, qi: (h, 0, qi)),
        _resident((None, tok, dq), lambda h, qi: (h // group, 0, 0)),
        _resident((None, nk, dq, t), lambda h, qi: (h // group, 0, 0, 0)),
        _resident((None, nk, dv, t), lambda h, qi: (h // group, 0, 0, 0)),
        q_tile((dv,)),
        pl.BlockSpec((None, dv, t), lambda h, qi: (h, 0, qi)),
        q_tile((1,)),
        q_tile((1,)),
    ]
    args = [q, q_t, k, k_t4, v_t4, do, do_t, lse, delta]
    out_specs = [q_tile((dq,)),
                 pl.BlockSpec((None, nk, dq, t), lambda h, qi: (h, 0, 0, 0)),
                 pl.BlockSpec((None, nk, dv, t), lambda h, qi: (h, 0, 0, 0))]
    out_shape = [jax.ShapeDtypeStruct((h_n, tok, dq), F32), jax.ShapeDtypeStruct((h_n, nk, dq, t), F32),
                 jax.ShapeDtypeStruct((h_n, nk, dv, t), F32)]
    if bias:
        in_specs += [q_tile((1,)), _resident((None, nk, 1, t), lambda h, qi: (h, 0, 0, 0))]
        args += [cq_col, ck_row4]
        out_specs += [pl.BlockSpec((None, nk, 1, t), lambda h, qi: (h, 0, 0, 0)), q_tile((1,))]
        out_shape += [jax.ShapeDtypeStruct((h_n, nk, 1, t), F32), jax.ShapeDtypeStruct((h_n, tok, 1), F32)]
    res = pl.pallas_call(
        body, name=name, grid=(h_n, nq),
        in_specs=in_specs + side_in_specs, out_specs=out_specs + side_out_specs,
        out_shape=out_shape + side_out_shape, scratch_shapes=side_scratch,
        input_output_aliases={n_in + i: n_out + o for i, o in side_aliases.items()},
        compiler_params=_params(dimension_semantics=("arbitrary" if side is not None else "parallel", "arbitrary")),
    )(*args, *side_args)
    return (res[:n_out], res[n_out:]) if side is not None else res


def _delta_t(name, do_t, o_t, tl=1024):
    h_n, dv, tok = do_t.shape
    tl = min(tl, tok)

    def body(do_ref, o_ref, out_ref):
        out_ref[...] = jnp.sum(do_ref[...].astype(F32) * o_ref[...].astype(F32), axis=1, keepdims=True)

    spec = pl.BlockSpec((h_n, dv, tl), lambda i: (0, 0, i))
    return pl.pallas_call(
        body, name=name, grid=(tok // tl,), in_specs=[spec, spec],
        out_specs=pl.BlockSpec((h_n, 1, tl), lambda i: (0, 0, i)),
        out_shape=jax.ShapeDtypeStruct((h_n, 1, tok), F32),
        compiler_params=_params(dimension_semantics=("parallel",)),
    )(do_t, o_t)


def _log_sigmoid(z):
    return jnp.minimum(z, 0.0) - jnp.log(1.0 + jnp.exp(-jnp.abs(z)))


def _gate_fwd(zt, bias):
    tok = zt.shape[1]

    def body(z_ref, b_ref, c_ref):
        x = _log_sigmoid(z_ref[...] + b_ref[...])
        lane = lax.broadcasted_iota(jnp.int32, x.shape, 1)
        k = 1
        while k < tok:
            x = x + jnp.where(lane >= k, pltpu.roll(x, k, axis=1), 0.0)
            k *= 2
        c_ref[...] = x * LOG2E

    return pl.pallas_call(body, name="fox_gate_fwd", out_shape=jax.ShapeDtypeStruct(zt.shape, F32),
                          compiler_params=_params())(zt, bias)


def _gate_bwd(d_rows, d_cols, zt, bias):
    tok = zt.shape[1]

    def body(dr_ref, dc_ref, z_ref, b_ref, dz_ref, db_ref):
        x = dr_ref[...] - dc_ref[...]
        lane = lax.broadcasted_iota(jnp.int32, x.shape, 1)
        k = 1
        while k < tok:
            x = x + jnp.where(lane < tok - k, pltpu.roll(x, tok - k, axis=1), 0.0)
            k *= 2
        dz = x / (1.0 + jnp.exp(z_ref[...] + b_ref[...]))
        dz_ref[...] = dz
        db_ref[...] = jnp.sum(dz, axis=1, keepdims=True)

    return pl.pallas_call(body, name="fox_gate_bwd",
                          out_shape=[jax.ShapeDtypeStruct(zt.shape, F32), jax.ShapeDtypeStruct(bias.shape, F32)],
                          compiler_params=_params())(d_rows, d_cols, zt, bias)


_NT = (((1,), (1,)), ((), ()))
_TN = (((0,), (0,)), ((), ()))


def _ffn_gate_up(name, h, wg, wu, l, tm=512):
    tok, d = h.shape
    nb, n = wg.shape[0], wg.shape[3]
    tm = min(tm, tok)

    def body(h_ref, wg_ref, wu_ref, u_ref, v_ref, a_ref):
        hv = h_ref[...]
        u = jnp.dot(hv, wg_ref[...], preferred_element_type=F32)
        v = jnp.dot(hv, wu_ref[...], preferred_element_type=F32)
        u_ref[...] = u.astype(BF16)
        v_ref[...] = v.astype(BF16)
        a_ref[...] = (u * jax.nn.sigmoid(u) * v).astype(BF16)

    w_spec = pl.BlockSpec((None, None, d, n), lambda j, i: (j, l, 0, 0))
    o_spec = pl.BlockSpec((None, tm, n), lambda j, i: (j, i, 0))
    return pl.pallas_call(
        body, name=name, grid=(nb, tok // tm),
        in_specs=[pl.BlockSpec((tm, d), lambda j, i: (i, 0)), w_spec, w_spec],
        out_specs=[o_spec] * 3, out_shape=[jax.ShapeDtypeStruct((nb, tok, n), BF16)] * 3,
        compiler_params=_params(dimension_semantics=("parallel", "parallel")),
    )(h, wg, wu)


def _ffn_down(name, a, wd, x, l, tm=1024, tn=1024):
    nb, tok, n = a.shape
    d = wd.shape[3]
    tm, tn = min(tm, tok), min(tn, d)

    def body(a_ref, wd_ref, x_ref, y_ref, acc):
        j = pl.program_id(2)

        @pl.when(j == 0)
        def _():
            acc[...] = jnp.zeros(acc.shape, F32)
        acc[...] += jnp.dot(a_ref[...], wd_ref[...], preferred_element_type=F32)

        @pl.when(j == nb - 1)
        def _():
            y_ref[...] = x_ref[...] + 0.5 * acc[...]

    return pl.pallas_call(
        body, name=name, grid=(tok // tm, d // tn, nb),
        in_specs=[pl.BlockSpec((None, tm, n), lambda i, c, j: (j, i, 0)),
                  pl.BlockSpec((None, None, n, tn), lambda i, c, j: (j, l, 0, c)),
                  pl.BlockSpec((tm, tn), lambda i, c, j: (i, c))],
        out_specs=pl.BlockSpec((tm, tn), lambda i, c, j: (i, c)),
        out_shape=jax.ShapeDtypeStruct((tok, d), F32),
        scratch_shapes=[pltpu.VMEM((tm, tn), F32)],
        compiler_params=_params(dimension_semantics=("parallel", "parallel", "arbitrary")),
    )(a, wd, x)


def _ffn_d_act(name, dy, u, v, wd, l, tm=512):
    nb, tok, n = u.shape
    d = dy.shape[1]
    tm = min(tm, tok)

    def body(dy_ref, u_ref, v_ref, wd_ref, du_ref, dv_ref):
        da = 0.5 * lax.dot_general(dy_ref[...], wd_ref[...], _NT, preferred_element_type=F32)
        uv, vv = u_ref[...].astype(F32), v_ref[...].astype(F32)
        sg = jax.nn.sigmoid(uv)
        du_ref[...] = (da * vv * (sg * (1.0 + uv * (1.0 - sg)))).astype(BF16)
        dv_ref[...] = (da * (uv * sg)).astype(BF16)

    t_spec = pl.BlockSpec((None, tm, n), lambda j, i: (j, i, 0))
    return pl.pallas_call(
        body, name=name, grid=(nb, tok // tm),
        in_specs=[pl.BlockSpec((tm, d), lambda j, i: (i, 0)), t_spec, t_spec,
                  pl.BlockSpec((None, None, n, d), lambda j, i: (j, l, 0, 0))],
        out_specs=[t_spec] * 2, out_shape=[jax.ShapeDtypeStruct((nb, tok, n), BF16)] * 2,
        compiler_params=_params(dimension_semantics=("parallel", "parallel")),
    )(dy, u, v, wd)


def _ffn_d_h(name, du, dv, wg, wu, l, tm=1024, side=None):
    nb, tok, n = du.shape
    d = wg.shape[2]
    tm = min(tm, tok)
    ni = tok // tm
    side_args, side_in_specs, side_out_specs, side_out_shape, side_scratch, side_aliases = (
        side.specs() if side is not None else ([], [], [], [], [], {}))
    n_in = 4

    def body(*refs):
        du_ref, dv_ref, wg_ref, wu_ref = refs[:n_in]
        pos = n_in + len(side_args)
        dh_ref, acc = refs[pos], refs[len(refs) - 1]
        i, j = pl.program_id(0), pl.program_id(1)
        if side is not None:
            side_start, side_finish = side.ops(refs[n_in:pos], refs[pos + 1:pos + 1 + side.n],
                                               refs[pos + 1 + side.n:len(refs) - 1])
            pl.when((i == 0) & (j == 0))(side_start)

        @pl.when(j == 0)
        def _():
            acc[...] = jnp.zeros(acc.shape, F32)
        acc[...] += (lax.dot_general(du_ref[...], wg_ref[...], _NT, preferred_element_type=F32)
                     + lax.dot_general(dv_ref[...], wu_ref[...], _NT, preferred_element_type=F32))

        @pl.when(j == nb - 1)
        def _():
            dh_ref[...] = acc[...].astype(BF16)
        if side is not None:
            pl.when((i == ni - 1) & (j == nb - 1))(side_finish)

    t_spec = pl.BlockSpec((None, tm, n), lambda i, j: (j, i, 0))
    w_spec = pl.BlockSpec((None, None, d, n), lambda i, j: (j, l, 0, 0))
    res = pl.pallas_call(
        body, name=name, grid=(ni, nb),
        in_specs=[t_spec, t_spec, w_spec, w_spec] + side_in_specs,
        out_specs=[pl.BlockSpec((tm, d), lambda i, j: (i, 0))] + side_out_specs,
        out_shape=[jax.ShapeDtypeStruct((tok, d), BF16)] + side_out_shape,
        scratch_shapes=side_scratch + [pltpu.VMEM((tm, d), F32)],
        input_output_aliases={n_in + i: 1 + o for i, o in side_aliases.items()},
        compiler_params=_params(dimension_semantics=("arbitrary" if side is not None else "parallel", "arbitrary")),
    )(du, dv, wg, wu, *side_args)
    return (res[0], res[1:]) if side is not None else res[0]


def _ffn_wgrad_in(name, h, du, dv, l, like, prev=None, tk=1024, td=1024):
    tok, d = h.shape
    nb, _, n = du.shape
    tk, td = min(tk, tok), min(td, d)
    nk = tok // tk
    n_in = 3

    def body(*refs):
        h_ref, du_ref, dv_ref = refs[:3]
        og_ref, ou_ref, accg, accu = refs[len(refs) - 4:]
        k = pl.program_id(2)

        @pl.when(k == 0)
        def _():
            accg[...] = jnp.zeros(accg.shape, F32)
            accu[...] = jnp.zeros(accu.shape, F32)
        hv = h_ref[...]
        accg[...] += lax.dot_general(hv, du_ref[...], _TN, preferred_element_type=F32)
        accu[...] += lax.dot_general(hv, dv_ref[...], _TN, preferred_element_type=F32)

        @pl.when(k == nk - 1)
        def _():
            og_ref[...] = accg[...].astype(BF16)
            ou_ref[...] = accu[...].astype(BF16)

    t_spec = pl.BlockSpec((None, tk, n), lambda j, c, k: (j, k, 0))
    o_spec = pl.BlockSpec((None, None, td, n), lambda j, c, k: (j, l, c, 0))
    in_specs = [pl.BlockSpec((tk, td), lambda j, c, k: (k, c)), t_spec, t_spec]
    args = [h, du, dv]
    aliases = {}
    if prev is not None:
        in_specs += [pl.BlockSpec(memory_space=pl.ANY)] * 2
        args += list(prev)
        aliases = {n_in: 0, n_in + 1: 1}
    return pl.pallas_call(
        body, name=name, grid=(nb, d // td, nk),
        in_specs=in_specs, out_specs=[o_spec] * 2,
        out_shape=[jax.ShapeDtypeStruct(like.shape, BF16)] * 2,
        scratch_shapes=[pltpu.VMEM((td, n), F32)] * 2,
        input_output_aliases=aliases,
        compiler_params=_params(dimension_semantics=("parallel", "parallel", "arbitrary")),
    )(*args)


def _ffn_wgrad_out(name, a, dy, l, like, prev=None, tk=1024):
    nb, tok, n = a.shape
    d = dy.shape[1]
    tk = min(tk, tok)
    nk = tok // tk

    def body(*refs):
        a_ref, dy_ref = refs[:2]
        o_ref, acc = refs[len(refs) - 2:]
        k = pl.program_id(1)

        @pl.when(k == 0)
        def _():
            acc[...] = jnp.zeros(acc.shape, F32)
        acc[...] += lax.dot_general(a_ref[...], dy_ref[...], _TN, preferred_element_type=F32)

        @pl.when(k == nk - 1)
        def _():
            o_ref[...] = (0.5 * acc[...]).astype(BF16)

    in_specs = [pl.BlockSpec((None, tk, n), lambda j, k: (j, k, 0)), pl.BlockSpec((tk, d), lambda j, k: (k, 0))]
    args = [a, dy]
    aliases = {}
    if prev is not None:
        in_specs.append(pl.BlockSpec(memory_space=pl.ANY))
        args.append(prev)
        aliases = {2: 0}
    return pl.pallas_call(
        body, name=name, grid=(nb, nk),
        in_specs=in_specs, out_specs=pl.BlockSpec((None, None, n, d), lambda j, k: (j, l, 0, 0)),
        out_shape=jax.ShapeDtypeStruct(like.shape, BF16),
        scratch_shapes=[pltpu.VMEM((n, d), F32)],
        input_output_aliases=aliases,
        compiler_params=_params(dimension_semantics=("parallel", "arbitrary")),
    )(*args)


def _ffn_fwd(tag, x, g, wg, wu, wd, l):
    h = _rms_fwd(tag + "_norm", x, g)
    u, v, a = _ffn_gate_up(tag + "_gate_up", h, wg, wu, l)
    y = _ffn_down(tag + "_down", a, wd, x, l)
    return y, (x, h, u, v, a)


def _ffn_bwd(tag, dy, dyb, saved, g, wg, wu, wd, l, prev, side=None):
    x, h, u, v, a = saved
    du, dv = _ffn_d_act(tag + "_d_act", dyb, u, v, wd, l)
    d_wd = _ffn_wgrad_out(tag + "_d_wd", a, dyb, l, wd, None if prev is None else prev[2])
    d_wg, d_wu = _ffn_wgrad_in(tag + "_d_wgu", h, du, dv, l, wg, None if prev is None else prev[:2])
    dh = _ffn_d_h(tag + "_d_h", du, dv, wg, wu, l, side=side)
    dh, exchanged = dh if side is not None else (dh, None)
    dx, dg, dxb = _rms_bwd(tag + "_d_norm", dh, x, g, res=dy, also_bf16=True)
    if side is not None:
        return dx, dxb, dg, (d_wg, d_wu, d_wd), exchanged
    return dx, dxb, dg, (d_wg, d_wu, d_wd)


def _swap_halves(x):
    t, w = x.shape
    return x.reshape(t, w // HEAD_DIM, 2, HEAD_DIM // 2)[:, :, ::-1, :].reshape(t, w)


def _unheads(x):
    n, t, d = x.shape
    return x.transpose(1, 0, 2).reshape(t, n * d)


def _nat(x3):
    return x3.transpose(1, 0, 2).astype(BF16)


def _tr(x3):
    return x3.transpose(1, 2, 0).astype(BF16)


def _tr4(x3, t):
    tok, n, d = x3.shape
    return x3.reshape(tok // t, t, n, d).transpose(2, 0, 3, 1).astype(BF16)


def _from_t(x_t):
    n, d, tok = x_t.shape
    return x_t.transpose(2, 0, 1).reshape(tok, n * d)


def _from_t4(x4):
    n, nk, d, t = x4.shape
    return x4.transpose(1, 3, 0, 2).reshape(nk * t, n, d)


def _col(row):
    return row.transpose(0, 2, 1)


def _mixer_fwd(tag, x, w, cos, sin):
    tok = x.shape[0]
    h2 = _rms_fwd(tag + "_norm", x, w["mix_norm"])
    p = _mm1(tag + "_in", h2, w["w_in"], tn=640)
    c_q, c_kv = p[:, :512], p[:, 512:768]
    q_s, k_s, v_s = p[:, 768:1280], p[:, 1280:1408], p[:, 1408:1536]
    q_f, k_f, v_f = p[:, 1536:2048], p[:, 2048:2560], p[:, 2560:3072]
    k_rope, f_logit = p[:, 3072:3136], p[:, 3136:3144]

    qn = _rms_fwd(tag + "_q_norm", c_q, w["mla_q_norm"])
    qm = _mm1(tag + "_q_b", qn, w["mla_w_q_b"])
    kvn = _rms_fwd(tag + "_kv_norm", c_kv, w["mla_kv_norm"])
    kvm = _mm1(tag + "_kv_b", kvn, w["mla_w_kv_b"])

    rin = jnp.concatenate([qm[:, 1024:], q_s, k_s, k_rope, jnp.zeros((tok, 64), F32)], axis=1)
    rout = _rope(tag + "_rope", rin, _swap_halves(rin), cos, sin)
    q_pe, q_sr, k_sr, k_pe = rout[:, :512], rout[:, 512:1024], rout[:, 1024:1152], rout[:, 1152:1216]

    t = min(ATT_T, tok)
    q_m = jnp.concatenate([qm[:, :1024].reshape(tok, HEADS, MLA_NOPE), q_pe.reshape(tok, HEADS, MLA_ROPE)], axis=-1)
    k_m = jnp.concatenate([kvm[:, :1024].reshape(tok, HEADS, MLA_NOPE),
                           jnp.broadcast_to(k_pe[:, None, :], (tok, HEADS, MLA_ROPE))], axis=-1)
    v_m = kvm[:, 1024:].reshape(tok, HEADS, MLA_V)
    mla = dict(q=_nat(q_m), q_t=_tr(q_m), k=_nat(k_m), k_t4=_tr4(k_m, t), v_t4=_tr4(v_m, t))
    mla["o_t"], mla["lse"] = _attn_fwd(tag + "_mla_fwd", mla["q_t"], mla["k"], mla["v_t4"],
                                       (MLA_NOPE + MLA_ROPE) ** -0.5, t=t)

    q_s3, k_s3 = q_sr.reshape(tok, HEADS, HEAD_DIM), k_sr.reshape(tok, SWA_KV_HEADS, HEAD_DIM)
    v_s3 = v_s.reshape(tok, SWA_KV_HEADS, HEAD_DIM)
    swa = dict(q=_nat(q_s3), q_t=_tr(q_s3), k=_nat(k_s3), k_t4=_tr4(k_s3, t), v_t4=_tr4(v_s3, t))
    swa["o_t"], swa["lse"] = _attn_fwd(tag + "_swa_fwd", swa["q_t"], swa["k"], swa["v_t4"], HEAD_DIM ** -0.5,
                                       group=HEADS // SWA_KV_HEADS, sink=w["swa_sinks"], window=True, t=t)

    zt = f_logit.T
    c = _gate_fwd(zt, w["fox_forget_bias"].reshape(HEADS, 1))
    q_f3, k_f3, v_f3 = (a.reshape(tok, HEADS, HEAD_DIM) for a in (q_f, k_f, v_f))
    fox = dict(q=_nat(q_f3), q_t=_tr(q_f3), k=_nat(k_f3), k_t4=_tr4(k_f3, t), v_t4=_tr4(v_f3, t),
               c_col=c[:, :, None], c_row4=c.reshape(HEADS, tok // t, 1, t))
    fox["o_t"], fox["lse"] = _attn_fwd(tag + "_fox_fwd", fox["q_t"], fox["k"], fox["v_t4"], HEAD_DIM ** -0.5,
                                       cq_row=c[:, None, :], ck_col=fox["c_col"], t=t)

    mixed = jnp.concatenate([_from_t(mla["o_t"]), _from_t(swa["o_t"]), _from_t(fox["o_t"])], axis=1)
    y = _mm1(tag + "_out", mixed, w["w_out"], add=x)
    saved = dict(x=x, h2=h2, c_q=c_q, c_kv=c_kv, qn=qn, kvn=kvn, zt=zt, mixed=mixed, mla=mla, swa=swa, fox=fox)
    return y, saved


def _mixer_bwd(tag, dy, dyb, s, w, cos, sin, side=None):
    tok = dy.shape[0]
    g = {}
    dmixed = _mm1(tag + "_d_mixed", dyb, w["w_out"], out_dtype=BF16, tb=True)
    g["w_out"] = _mm1(tag + "_d_wout", s["mixed"], dyb, ta=True)
    t = min(ATT_T, tok)

    def attn_bwd(name, a, d_out, scale, **kw):
        do3 = d_out.reshape(tok, HEADS, -1)
        do_t = _tr(do3)
        dl = _col(_delta_t(name + "_delta", do_t, a["o_t"]))
        return dl, _attn_bwd(name + "_bwd", a["q"], a["q_t"], a["k"], a["k_t4"], a["v_t4"], _nat(do3), do_t,
                             _col(a["lse"]), dl, scale, t=t, **kw)

    _, mla_res = attn_bwd(tag + "_mla", s["mla"], dmixed[:, :1024], (MLA_NOPE + MLA_ROPE) ** -0.5, side=side)
    (dq_m, dk_m4, dv_m4), exchanged = mla_res if side is not None else (mla_res, None)
    dl, (dq_sh, dk_s4, dv_s4) = attn_bwd(tag + "_swa", s["swa"], dmixed[:, 1024:1536], HEAD_DIM ** -0.5,
                                         group=HEADS // SWA_KV_HEADS, window=True)
    g["swa_sinks"] = _sink_grad(tag + "_d_sink", w["swa_sinks"].reshape(HEADS, 1, 1), _col(s["swa"]["lse"]),
                                dl).reshape(HEADS)
    _, (dq_fh, dk_f4, dv_f4, d_cols, d_rows) = attn_bwd(tag + "_fox", s["fox"], dmixed[:, 1536:], HEAD_DIM ** -0.5,
                                                        cq_col=s["fox"]["c_col"], ck_row4=s["fox"]["c_row4"])
    dzt, dbias = _gate_bwd(d_rows[:, :, 0], d_cols.reshape(HEADS, tok), s["zt"],
                           w["fox_forget_bias"].reshape(HEADS, 1))
    g["fox_forget_bias"] = dbias.reshape(HEADS)

    grp = HEADS // SWA_KV_HEADS
    dq_mt = dq_m.transpose(1, 0, 2)
    dk_mt = _from_t4(dk_m4)
    d_qpe = dq_mt[:, :, MLA_NOPE:].reshape(tok, HEADS * MLA_ROPE)
    d_kpe_heads = dk_mt[:, :, MLA_NOPE:].reshape(tok, HEADS * MLA_ROPE)
    d_qs = _unheads(dq_sh)
    d_ks_heads = _from_t4(dk_s4).reshape(tok, SWA_KV_HEADS, grp, HEAD_DIM).transpose(0, 2, 1, 3).reshape(tok, grp * 128)
    d_vs_heads = _from_t4(dv_s4).reshape(tok, SWA_KV_HEADS, grp, HEAD_DIM).transpose(0, 2, 1, 3).reshape(tok, grp * 128)

    def fold(d_kpe_h, d_ks_h, d_vs_h):
        kpe = d_kpe_h[:, 0:64]
        for i in range(1, HEADS):
            kpe = kpe + d_kpe_h[:, 64 * i:64 * (i + 1)]
        ks, vs = d_ks_h[:, 0:128], d_vs_h[:, 0:128]
        for i in range(1, grp):
            ks = ks + d_ks_h[:, 128 * i:128 * (i + 1)]
            vs = vs + d_vs_h[:, 128 * i:128 * (i + 1)]
        return [jnp.concatenate([kpe, jnp.zeros_like(kpe)], axis=1), ks, vs]
    d_kpe2, d_ksr, d_vs = _ew(tag + "_fold_heads", fold,
                              [(d_kpe_heads, "row"), (d_ks_heads, "row"), (d_vs_heads, "row")],
                              [((tok, 128), F32, "row"), ((tok, 128), F32, "row"), ((tok, 128), F32, "row")])

    rin = jnp.concatenate([d_qpe, d_qs, d_ksr, d_kpe2], axis=1)
    rout = _rope(tag + "_d_rope", rin, _swap_halves(rin), cos, sin, sign=-1.0)
    d_qpe_pre, d_qs_pre, d_ks_pre, d_krope = rout[:, :512], rout[:, 512:1024], rout[:, 1024:1152], rout[:, 1152:1216]

    d_qm = jnp.concatenate([dq_mt[:, :, :MLA_NOPE].reshape(tok, HEADS * MLA_NOPE), d_qpe_pre], axis=1)
    d_kvm = jnp.concatenate([dk_mt[:, :, :MLA_NOPE].reshape(tok, HEADS * MLA_NOPE),
                             _from_t4(dv_m4).reshape(tok, HEADS * MLA_V)], axis=1)
    g["mla_w_q_b"] = _mm1(tag + "_d_wqb", s["qn"], d_qm, ta=True)
    d_qn = _mm1(tag + "_d_qn", d_qm, w["mla_w_q_b"], tb=True)
    d_cq, g["mla_q_norm"] = _rms_bwd(tag + "_d_q_norm", d_qn, s["c_q"], w["mla_q_norm"])
    g["mla_w_kv_b"] = _mm1(tag + "_d_wkvb", s["kvn"], d_kvm, ta=True)
    d_kvn = _mm1(tag + "_d_kvn", d_kvm, w["mla_w_kv_b"], tb=True)
    d_ckv, g["mla_kv_norm"] = _rms_bwd(tag + "_d_kv_norm", d_kvn, s["c_kv"], w["mla_kv_norm"])

    dp = jnp.concatenate([d_cq, d_ckv, d_qs_pre, d_ks_pre, d_vs, _unheads(dq_fh),
                          _from_t4(dk_f4).reshape(tok, HEADS * HEAD_DIM), _from_t4(dv_f4).reshape(tok, HEADS * HEAD_DIM),
                          d_krope, dzt.T, jnp.zeros((tok, IN_COLS_PAD - IN_COLS), F32)], axis=1).astype(BF16)
    g["w_in"] = _mm1(tag + "_d_win", s["h2"], dp, ta=True, tn=640)
    dh2 = _mm1(tag + "_d_h2", dp, w["w_in"], tb=True, tk=640)
    dx, g["mix_norm"], dxb = _rms_bwd(tag + "_d_norm", dh2, s["x"], w["mix_norm"], res=dy, also_bf16=True)
    return dx, dxb, g, exchanged


def _loss_head(x, g, target):
    d = x.shape[1]

    def fn(x, g, tgt):
        r = lax.rsqrt(jnp.mean(x * x, axis=-1, keepdims=True) + RMS_EPS)
        xh = x * r
        err = xh * g - tgt
        loss = 0.5 * jnp.sum(jnp.sum(err * err, axis=-1, keepdims=True), axis=0, keepdims=True) / d
        dy = err / d
        dxh = dy * g
        dx = r * (dxh - xh * jnp.mean(dxh * xh, axis=-1, keepdims=True))
        return [dx, jnp.sum(dy * xh, axis=0, keepdims=True), loss, dx]
    return _ew("loss_head", fn, [(x, "row"), (g, "full"), (target, "row")],
               [(x.shape, F32, "row"), (g.shape, F32, "acc"), ((1, 1), F32, "acc"), (x.shape, BF16, "row")])


def _adam_update(w, g, m, v):
    m = ADAM_B1 * m + (1.0 - ADAM_B1) * g
    v = ADAM_B2 * v + (1.0 - ADAM_B2) * (g * g)
    m_hat = m / (1.0 - ADAM_B1 ** ADAM_STEP)
    v_hat = v / (1.0 - ADAM_B2 ** ADAM_STEP)
    return [-ADAM_LR * (m_hat / (jnp.sqrt(v_hat) + ADAM_EPS) + ADAM_WD * w), m, v]


def _adamw(name, w, g, m, v):
    return _ew(name, _adam_update, [(w, "row"), (g, "row"), (m, "row"), (v, "row")], [(w.shape, F32, "row")] * 3)


def _adamw_slots(name, r, w, m, v):
    def fn(r, w, m, v):
        g = r[0].astype(F32)
        for i in range(1, r.shape[0]):
            g = g + r[i].astype(F32)
        return [g] + _adam_update(w, g, m, v)
    return _ew(name, fn, [(r, "row"), (w, "row"), (m, "row"), (v, "row")], [(w.shape, F32, "row")] * 4)


def _pair_sum(name, g, layer, from_sibling, core):
    _, _, rows, cols = g.shape
    tm = _rows(rows, 512)

    def body(c_ref, g_ref, s_ref, o_ref):
        o_ref[...] = (g_ref[...].astype(F32) + s_ref[...].astype(F32)).astype(BF16)

    return pl.pallas_call(
        body, name=name,
        grid_spec=pltpu.PrefetchScalarGridSpec(
            num_scalar_prefetch=1, grid=(N_CHIP, rows // tm),
            in_specs=[pl.BlockSpec((None, None, tm, cols), lambda ch, i, c: (2 * ch + c[0], layer, i, 0)),
                      pl.BlockSpec((None, tm, cols), lambda ch, i, c: (ch, i, 0))],
            out_specs=pl.BlockSpec((None, tm, cols), lambda ch, i, c: (ch, i, 0))),
        out_shape=jax.ShapeDtypeStruct((N_CHIP, rows, cols), BF16),
        compiler_params=_params(dimension_semantics=("parallel", "parallel")),
    )(core, g, from_sibling)


def _chip_sums(blocks, layer, tag):
    names = list(blocks)
    core = lax.axis_index("c").astype(jnp.int32).reshape(1)
    from_sibling = _pair_exchange([blocks[k] for k in names], tag)
    return [_pair_sum("pair_sum_%s_%s" % (tag, k), blocks[k][0], blocks[k][1], s, core)
            for k, s in zip(names, from_sibling)]


def _coords(dev):
    return (dev // 4, (dev // 2) % 2, dev % 2)


def _all_gather(shards):
    n = len(shards)

    def body(*refs):
        x_refs, out_refs = refs[:n], refs[n:2 * n]
        send_sems, recv_sems, local_sems = refs[2 * n:]
        x, y, c = lax.axis_index("x"), lax.axis_index("y"), lax.axis_index("c")
        me, sibling = (x, y, c), (x, y, 1 - c)
        chips = [(1 - x, y), (x, 1 - y), (1 - x, 1 - y)]

        def slot(a, px, py, pc):
            return out_refs[a].at[4 * px + 2 * py + pc]

        def copy(a, k, block, to, src=None):
            return pltpu.make_async_remote_copy(
                src_ref=slot(a, *block) if src is None else src, dst_ref=slot(a, *block),
                send_sem=send_sems.at[a, k], recv_sem=recv_sems.at[a, k],
                device_id=to, device_id_type=pl.DeviceIdType.MESH)

        mine = [pltpu.make_async_copy(x_refs[a], slot(a, *me), local_sems.at[a]) for a in range(n)]
        first, passed = [], []
        for a in range(n):
            mine[a].start()
            first.append(copy(a, 0, me, sibling, src=x_refs[a]))
            first += [copy(a, 1 + j, me, (*chip, c), src=x_refs[a]) for j, chip in enumerate(chips)]
        for cp in first:
            cp.start()
        for a in range(n):
            for j, chip in enumerate(chips):
                copy(a, 1 + j, (*chip, c), me).wait_recv()
                passed.append(copy(a, 4 + j, (*chip, c), sibling))
                passed[-1].start()
        for a in range(n):
            copy(a, 0, sibling, me).wait_recv()
            for j, chip in enumerate(chips):
                copy(a, 4 + j, (*chip, 1 - c), me).wait_recv()
        for cp in first + passed:
            cp.wait_send()
        for a in range(n):
            mine[a].wait()

    return pl.pallas_call(
        body, name="all_gather_weights",
        out_shape=[jax.ShapeDtypeStruct((N_DEV,) + s.shape, s.dtype) for s in shards],
        in_specs=[pl.BlockSpec(memory_space=pl.ANY)] * n,
        out_specs=[pl.BlockSpec(memory_space=pl.ANY)] * n,
        scratch_shapes=[pltpu.SemaphoreType.DMA((n, 7)), pltpu.SemaphoreType.DMA((n, 7)),
                        pltpu.SemaphoreType.DMA((n,))],
    )(*shards)


def _pair_exchange(blocks, tag):
    n = len(blocks)
    layers = [li for _, li in blocks]
    arrays = [b for b, _ in blocks]

    def body(*refs):
        g_refs, out_refs = refs[:n], refs[n:2 * n]
        send_sems, recv_sems = refs[2 * n:]
        x, y, c = lax.axis_index("x"), lax.axis_index("y"), lax.axis_index("c")
        copies = [pltpu.make_async_remote_copy(
            src_ref=g_refs[a].at[2 * chip + (1 - c), layers[a]], dst_ref=out_refs[a].at[chip],
            send_sem=send_sems.at[a, chip], recv_sem=recv_sems.at[a, chip],
            device_id=(x, y, 1 - c), device_id_type=pl.DeviceIdType.MESH)
            for a in range(n) for chip in range(N_CHIP)]
        for cp in copies:
            cp.start()
        for cp in copies:
            cp.wait()

    return pl.pallas_call(
        body, name="pair_exchange_grads_" + tag,
        out_shape=[jax.ShapeDtypeStruct((N_CHIP,) + b.shape[2:], b.dtype) for b in arrays],
        in_specs=[pl.BlockSpec(memory_space=pl.ANY)] * n,
        out_specs=[pl.BlockSpec(memory_space=pl.ANY)] * n,
        scratch_shapes=[pltpu.SemaphoreType.DMA((n, N_CHIP)), pltpu.SemaphoreType.DMA((n, N_CHIP))],
    )(*arrays)


class _ChipExchange:
    def __init__(self, sums, layer, prev=None):
        self.sums, self.layer, self.prev = list(sums), layer, None if prev is None else list(prev)
        self.n = len(self.sums)

    def specs(self):
        n = self.n
        args = self.sums + (self.prev or [])
        any_spec = pl.BlockSpec(memory_space=pl.ANY)
        out_shape = [jax.ShapeDtypeStruct((N_CHIP, DEPTH) + s.shape[1:], s.dtype) for s in self.sums]
        scratch = [pltpu.SemaphoreType.DMA((n, N_CHIP)), pltpu.SemaphoreType.DMA((n, N_CHIP)),
                   pltpu.SemaphoreType.DMA((n,))]
        aliases = {n + a: a for a in range(n)} if self.prev else {}
        return args, [any_spec] * len(args), [any_spec] * n, out_shape, scratch, aliases

    def ops(self, in_refs, out_refs, sem_refs):
        n, layer = self.n, self.layer
        g_refs = in_refs[:n]
        send_sems, recv_sems, local_sems = sem_refs
        x, y, c = lax.axis_index("x"), lax.axis_index("y"), lax.axis_index("c")
        my_chip = 2 * x + y

        def copy(a, chip):
            return pltpu.make_async_remote_copy(
                src_ref=g_refs[a].at[chip], dst_ref=out_refs[a].at[my_chip, layer],
                send_sem=send_sems.at[a, chip], recv_sem=recv_sems.at[a, my_chip],
                device_id=(chip // 2, chip % 2, c), device_id_type=pl.DeviceIdType.MESH)

        def arrival(a, chip):
            return pltpu.make_async_remote_copy(
                src_ref=g_refs[a].at[chip], dst_ref=out_refs[a].at[chip, layer],
                send_sem=send_sems.at[a, chip], recv_sem=recv_sems.at[a, chip],
                device_id=(chip // 2, chip % 2, c), device_id_type=pl.DeviceIdType.MESH)

        def mine(a):
            return pltpu.make_async_copy(g_refs[a].at[my_chip], out_refs[a].at[my_chip, layer], local_sems.at[a])

        def others(step):
            for chip in range(N_CHIP):
                @pl.when(chip != my_chip)
                def _(chip=chip):
                    for a in range(n):
                        step(a, chip)

        def start():
            for a in range(n):
                mine(a).start()
            others(lambda a, chip: copy(a, chip).start())

        def finish():
            others(lambda a, chip: arrival(a, chip).wait_recv())
            others(lambda a, chip: copy(a, chip).wait_send())
            for a in range(n):
                mine(a).wait()

        return start, finish

    def call(self):
        args, in_specs, out_specs, out_shape, scratch, aliases = self.specs()
        n_in = len(args)

        def body(*refs):
            start, finish = self.ops(refs[:n_in], refs[n_in:n_in + self.n], refs[n_in + self.n:])
            start()
            finish()

        return pl.pallas_call(
            body, name="chip_exchange_grads_l%d" % self.layer, out_shape=out_shape, in_specs=in_specs,
            out_specs=out_specs, scratch_shapes=scratch, input_output_aliases=aliases,
        )(*args)


def _all_reduce_small(buf):
    def body(x_ref, out_ref, slots, send_sems, recv_sems):
        me = 4 * lax.axis_index("x") + 2 * lax.axis_index("y") + lax.axis_index("c")

        def copy(peer):
            return pltpu.make_async_remote_copy(
                src_ref=x_ref, dst_ref=slots.at[me],
                send_sem=send_sems.at[peer], recv_sem=recv_sems.at[me],
                device_id=_coords(peer), device_id_type=pl.DeviceIdType.MESH)

        def arrival(peer):
            return pltpu.make_async_remote_copy(
                src_ref=x_ref, dst_ref=slots.at[peer],
                send_sem=send_sems.at[peer], recv_sem=recv_sems.at[peer],
                device_id=_coords(peer), device_id_type=pl.DeviceIdType.MESH)

        slots[pl.ds(me, 1)] = x_ref[...][None]
        for peer in range(N_DEV):
            @pl.when(peer != me)
            def _(peer=peer):
                copy(peer).start()
        for peer in range(N_DEV):
            @pl.when(peer != me)
            def _(peer=peer):
                arrival(peer).wait_recv()
        for peer in range(N_DEV):
            @pl.when(peer != me)
            def _(peer=peer):
                copy(peer).wait_send()
        acc = slots[0]
        for peer in range(1, N_DEV):
            acc = acc + slots[peer]
        out_ref[...] = acc

    return pl.pallas_call(
        body, name="all_reduce_small",
        out_shape=jax.ShapeDtypeStruct(buf.shape, F32),
        in_specs=[pl.BlockSpec(memory_space=pltpu.VMEM)],
        out_specs=pl.BlockSpec(memory_space=pltpu.VMEM),
        scratch_shapes=[pltpu.VMEM((N_DEV,) + buf.shape, F32), pltpu.SemaphoreType.DMA((N_DEV,)),
                        pltpu.SemaphoreType.DMA((N_DEV,))],
        compiler_params=_params(),
    )(buf)


def _in_to_kernel(w):
    pad = jnp.zeros(w.shape[:-1] + (IN_COLS_PAD - IN_COLS,), w.dtype)
    return jnp.concatenate([w[..., :768], w[..., 832:3136], w[..., 768:832], w[..., 3136:], pad], axis=-1)


def _in_from_kernel(w):
    return jnp.concatenate([w[..., :768], w[..., 3072:3136], w[..., 768:3072], w[..., 3136:3144]], axis=-1)


def _split_to_kernel(w, a, b):
    r = w.shape[0]
    w3 = w.reshape(r, HEADS, a + b)
    return jnp.concatenate([w3[:, :, :a].reshape(r, HEADS * a), w3[:, :, a:].reshape(r, HEADS * b)], axis=1)


def _split_from_kernel(w, a, b):
    r = w.shape[0]
    return jnp.concatenate([w[:, :HEADS * a].reshape(r, HEADS, a), w[:, HEADS * a:].reshape(r, HEADS, b)],
                           axis=-1).reshape(r, HEADS * (a + b))


MIXER_BIG = ("w_in", "mla_w_q_b", "mla_w_kv_b", "w_out")


def _col_blocks_to_full(blk):
    nb, k, n = blk.shape
    return blk.transpose(1, 0, 2).reshape(k, nb * n)


def _full_to_col_blocks(g):
    k, c = g.shape
    return g.reshape(k, N_DEV, c // N_DEV).transpose(1, 0, 2).astype(BF16)


def _local_step(x, positions, target, gathered, small, exchange=False):
    inv_freq = ROPE_THETA ** (-jnp.arange(0, HEAD_DIM, 2, dtype=F32) / HEAD_DIM)
    cos64, sin64 = _rope_tables(positions, jnp.concatenate([-inv_freq, inv_freq])[None, :])
    cos = jnp.tile(cos64, (1, 20))
    sin = jnp.tile(sin64, (1, 20))

    layers = []
    for l in range(DEPTH):
        w = {}
        w["w_in"] = _in_to_kernel(_col_blocks_to_full(gathered["w_in"][:, l]))
        w["mla_w_q_b"] = _split_to_kernel(_col_blocks_to_full(gathered["mla_w_q_b"][:, l]), MLA_NOPE, MLA_ROPE)
        w["mla_w_kv_b"] = _split_to_kernel(_col_blocks_to_full(gathered["mla_w_kv_b"][:, l]), MLA_NOPE, MLA_V)
        blk = gathered["w_out"][:, l]
        w["w_out"] = blk.reshape(blk.shape[0] * blk.shape[1], blk.shape[2])
        for k in SMALL:
            if k != "final_norm":
                w[k] = small[k][l][None, :] if "norm" in k else small[k][l]
        layers.append(w)

    def ffn_w(f):
        return gathered[f + "_w_gate"], gathered[f + "_w_up"], gathered[f + "_w_down"]

    saved = []
    for l, w in enumerate(layers):
        t = "l%d" % l
        x, s1 = _ffn_fwd(t + "_ffn1", x, w["ffn1_norm"], *ffn_w("ffn1"), l)
        x, s2 = _mixer_fwd(t + "_mix", x, w, cos, sin)
        x, s3 = _ffn_fwd(t + "_ffn2", x, w["ffn2_norm"], *ffn_w("ffn2"), l)
        saved.append((s1, s2, s3))

    dx, d_final, loss, dxb = _loss_head(x, small["final_norm"][None, :], target)

    mixer_g = [None] * DEPTH
    small_g = [None] * DEPTH
    ffn_g = {"ffn1": None, "ffn2": None}
    received = {}
    pending = None

    def exchange_of(blocks, l, tag):
        names = list(blocks)
        prev = [received[k] for k in names] if all(k in received for k in names) else None
        return names, _ChipExchange(_chip_sums(blocks, l, tag), l, prev)

    def took(job, results):
        received.update(zip(job[0], results))

    def ffn_blocks(f, l):
        return {f + name: (buf, l) for name, buf in zip(("_w_gate", "_w_up", "_w_down"), ffn_g[f])}

    for l in reversed(range(DEPTH)):
        w, (s1, s2, s3) = layers[l], saved[l]
        t = "l%d" % l
        sg = {}
        dx, dxb, sg["ffn2_norm"], ffn_g["ffn2"] = _ffn_bwd(t + "_ffn2", dx, dxb, s3, w["ffn2_norm"], *ffn_w("ffn2"), l,
                                                           ffn_g["ffn2"])
        dx, dxb, g, exchanged = _mixer_bwd(t + "_mix", dx, dxb, s2, w, cos, sin,
                                           side=None if pending is None else pending[1])
        if pending is not None:
            took(pending, exchanged)
            pending = None
        mixer_g[l] = {
            "w_in": _full_to_col_blocks(_in_from_kernel(g["w_in"])),
            "mla_w_q_b": _full_to_col_blocks(_split_from_kernel(g["mla_w_q_b"], MLA_NOPE, MLA_ROPE)),
            "mla_w_kv_b": _full_to_col_blocks(_split_from_kernel(g["mla_w_kv_b"], MLA_NOPE, MLA_V)),
            "w_out": g["w_out"].astype(BF16).reshape(N_DEV, -1, g["w_out"].shape[1]),
        }
        job = None
        if exchange:
            mixer_blocks = {k: (mixer_g[l][k][:, None], 0) for k in MIXER_BIG}
            blocks = ffn_blocks("ffn2", l)
            if l == 0:
                blocks.update(mixer_blocks)
            job = exchange_of(blocks, l, t + "a")
        res = _ffn_bwd(t + "_ffn1", dx, dxb, s1, w["ffn1_norm"], *ffn_w("ffn1"), l, ffn_g["ffn1"],
                       side=None if job is None else job[1])
        dx, dxb, sg["ffn1_norm"], ffn_g["ffn1"] = res[:4]
        if job is not None:
            took(job, res[4])
            blocks = ffn_blocks("ffn1", l)
            if l > 0:
                blocks.update(mixer_blocks)
            pending = exchange_of(blocks, l, t + "b")
        for k in SMALL:
            if k in g:
                sg[k] = g[k]
        small_g[l] = {k: v.reshape(-1) for k, v in sg.items()}

    sm = {k: jnp.stack([small_g[l][k] for l in range(DEPTH)]) for k in small_g[0]}
    sm["final_norm"] = d_final.reshape(-1)
    if exchange:
        took(pending, pending[1].call())
        return loss, dx, received, sm
    big = {k: jnp.stack([mixer_g[l][k] for l in range(DEPTH)], axis=1) for k in MIXER_BIG}
    for f in ("ffn1", "ffn2"):
        big[f + "_w_gate"], big[f + "_w_up"], big[f + "_w_down"] = ffn_g[f]
    return loss, dx, big, sm


def _flat_rows(a):
    return a.reshape(-1, LANES)


def kernel(x, positions, ffn1_norm, ffn1_w_gate, ffn1_w_up, ffn1_w_down, mix_norm, w_in, mla_q_norm, mla_w_q_b, mla_kv_norm, mla_w_kv_b, swa_sinks, fox_forget_bias, w_out, ffn2_norm, ffn2_w_gate, ffn2_w_up, ffn2_w_down, final_norm, loss_target, m_ffn1_norm, m_ffn1_w_gate, m_ffn1_w_up, m_ffn1_w_down, m_mix_norm, m_w_in, m_mla_q_norm, m_mla_w_q_b, m_mla_kv_norm, m_mla_w_kv_b, m_swa_sinks, m_fox_forget_bias, m_w_out, m_ffn2_norm, m_ffn2_w_gate, m_ffn2_w_up, m_ffn2_w_down, m_final_norm, v_ffn1_norm, v_ffn1_w_gate, v_ffn1_w_up, v_ffn1_w_down, v_mix_norm, v_w_in, v_mla_q_norm, v_mla_w_q_b, v_mla_kv_norm, v_mla_w_kv_b, v_swa_sinks, v_fox_forget_bias, v_w_out, v_ffn2_norm, v_ffn2_w_gate, v_ffn2_w_up, v_ffn2_w_down, v_final_norm):
    given = dict(locals())
    weights = {k: given[k] for k in WEIGHTS}
    mom_m = {k: given["m_" + k] for k in WEIGHTS}
    mom_v = {k: given["v_" + k] for k in WEIGHTS}

    gathered = dict(zip(BIG, _all_gather([weights[k].astype(BF16) for k in BIG])))

    small = {k: weights[k] for k in SMALL}
    loss, grad_x, received, small_g = _local_step(x[0], positions[0][:, None], loss_target[0], gathered, small,
                                                  exchange=True)

    small_sizes = [weights[k].size for k in SMALL]
    sbuf = jnp.concatenate([small_g[k].reshape(-1) for k in SMALL] + [loss.reshape(-1)])
    pad = (-sbuf.size) % (8 * LANES)
    sbuf = jnp.concatenate([sbuf, jnp.zeros((pad,), F32)])
    stot = _all_reduce_small(_flat_rows(sbuf)).reshape(-1)

    grad_w, delta, new_m, new_v = {}, {}, {}, {}
    for k in BIG:
        shape = weights[k].shape
        two_d = (-1, shape[-1])
        g, d, m, v = _adamw_slots("adamw_" + k, received[k].reshape(N_CHIP, -1, shape[-1]), weights[k].reshape(two_d),
                                  mom_m[k].reshape(two_d), mom_v[k].reshape(two_d))
        grad_w[k], delta[k], new_m[k], new_v[k] = (t.reshape(shape) for t in (g, d, m, v))
    off = 0
    for k, n in zip(SMALL, small_sizes):
        shape = weights[k].shape
        two_d = (-1, shape[-1]) if len(shape) > 1 else (1, -1)
        grad_w[k] = stot[off:off + n].reshape(shape)
        off += n
        d, m, v = _adamw("adamw_" + k, weights[k].reshape(two_d), grad_w[k].reshape(two_d),
                         mom_m[k].reshape(two_d), mom_v[k].reshape(two_d))
        delta[k], new_m[k], new_v[k] = d.reshape(shape), m.reshape(shape), v.reshape(shape)
    loss_total = stot[off]

    return (loss_total, grad_x[None], *[grad_w[k] for k in WEIGHTS], *[delta[k] for k in WEIGHTS],
            *[new_m[k] for k in WEIGHTS], *[new_v[k] for k in WEIGHTS])
```
